```python
import jax, jax.numpy as jnp
from jax import lax
import numpy as np

D_MODEL = 1024
BATCH = 4
SEQ = 8192
DEPTH = 2

GRID_W = 64
CTX_LEN = 256
N_EVEN = (DEPTH + 1) // 2
N_ODD = DEPTH // 2

HEAD_DIM = 64
ATT_HEADS = D_MODEL // (2 * HEAD_DIM)
KV_HEADS = ATT_HEADS // 4
Q_DIM = ATT_HEADS * HEAD_DIM
KV_DIM = KV_HEADS * HEAD_DIM
ATT_COLS = Q_DIM + 2 * KV_DIM
WINDOW = 128
BLOCK = 128
SPAN = BLOCK + 2 * WINDOW
AXIS_DIM = HEAD_DIM // 2
ROPE_BASE = 10000.0

RWKV_N = 64
RWKV_HEADS = D_MODEL // (2 * RWKV_N)
RWKV_DIM = RWKV_HEADS * RWKV_N
DECAY_LORA = 64
ICLR_LORA = 64
GATE_LORA = 128
RWKV_COLS = 3 * RWKV_DIM + DECAY_LORA + ICLR_LORA + GATE_LORA
RWKV_SPLITS = [RWKV_DIM, 2 * RWKV_DIM, 3 * RWKV_DIM,
               3 * RWKV_DIM + DECAY_LORA, 3 * RWKV_DIM + DECAY_LORA + ICLR_LORA]
IN_COLS = ATT_COLS + RWKV_COLS

FOURIER_GROUPS = 4

D_FF = 4 * D_MODEL
N_MOD = 6
NORM_EPS = 1e-6
GN_EPS = 64e-5
NEG_INF = -1e30

kernel_name = "hybrid_swa_rwkv7_fnet_dit"


def rmsnorm(z, g):
    zf = z.astype(jnp.float32)
    zf = zf * lax.rsqrt(jnp.mean(zf * zf, axis=-1, keepdims=True) + NORM_EPS)
    return zf.astype(z.dtype) * g


def modulate(h, shift, scale):
    return h * (1 + scale) + shift


def axial_angles(T):
    rows = T // GRID_W
    row = jnp.broadcast_to(jnp.arange(rows, dtype=jnp.float32)[:, None], (rows, GRID_W)).reshape(T)
    col = jnp.broadcast_to(jnp.arange(GRID_W, dtype=jnp.float32)[None, :], (rows, GRID_W)).reshape(T)
    inv = ROPE_BASE ** (-jnp.arange(0, AXIS_DIM, 2, dtype=jnp.float32) / AXIS_DIM)
    return row[:, None] * inv, col[:, None] * inv


def rope_half(x, ang):
    cos = jnp.cos(ang)[None, :, None, :].astype(x.dtype)
    sin = jnp.sin(ang)[None, :, None, :].astype(x.dtype)
    x1, x2 = x[..., :AXIS_DIM // 2], x[..., AXIS_DIM // 2:]
    return jnp.concatenate([x1 * cos - x2 * sin, x1 * sin + x2 * cos], axis=-1)


def axial_rope(x, ang_r, ang_c):
    return jnp.concatenate([rope_half(x[..., :AXIS_DIM], ang_r),
                            rope_half(x[..., AXIS_DIM:], ang_c)], axis=-1)


def window_attention(q, k, v, kc, vc, sink):
    B, T = q.shape[0], q.shape[1]
    C = kc.shape[1]
    G = ATT_HEADS // KV_HEADS
    nb = T // BLOCK
    scale = HEAD_DIM ** -0.5
    qb = jnp.moveaxis(q.reshape(B, nb, BLOCK, KV_HEADS, G, HEAD_DIM), 1, 0)
    pad = ((0, 0), (WINDOW, WINDOW), (0, 0), (0, 0))
    kp = jnp.pad(k, pad)
    vp = jnp.pad(v, pad)
    sink_logit = jnp.broadcast_to(sink.astype(jnp.float32).reshape(1, KV_HEADS, G, 1, 1),
                                  (B, KV_HEADS, G, BLOCK, 1))
    offs_q = jnp.arange(BLOCK)
    offs_k = jnp.arange(SPAN) - WINDOW

    def one_block(args):
        i, qi = args
        start = i * BLOCK
        ki = lax.dynamic_slice_in_dim(kp, start, SPAN, axis=1)
        vi = lax.dynamic_slice_in_dim(vp, start, SPAN, axis=1)
        qpos = start + offs_q
        kpos = start + offs_k
        valid = ((jnp.abs(kpos[None, :] - qpos[:, None]) <= WINDOW)
                 & (kpos >= 0)[None, :] & (kpos < T)[None, :])
        s_loc = jnp.einsum('bqhgd,bkhd->bhgqk', qi, ki).astype(jnp.float32) * scale
        s_loc = jnp.where(valid, s_loc, NEG_INF)
        s_ctx = jnp.einsum('bqhgd,bkhd->bhgqk', qi, kc).astype(jnp.float32) * scale
        p = jax.nn.softmax(jnp.concatenate([s_loc, s_ctx, sink_logit], axis=-1), axis=-1).astype(vi.dtype)
        return (jnp.einsum('bhgqk,bkhd->bqhgd', p[..., :SPAN], vi)
                + jnp.einsum('bhgqk,bkhd->bqhgd', p[..., SPAN:SPAN + C], vc))

    out = lax.map(one_block, (jnp.arange(nb), qb))
    return jnp.moveaxis(out, 0, 1).reshape(B, T, Q_DIM)


def context_attention(qc, kc, vc, sink):
    B, C = qc.shape[0], qc.shape[1]
    G = ATT_HEADS // KV_HEADS
    qg = qc.reshape(B, C, KV_HEADS, G, HEAD_DIM)
    s = jnp.einsum('bqhgd,bkhd->bhgqk', qg, kc).astype(jnp.float32) * HEAD_DIM ** -0.5
    sink_logit = jnp.broadcast_to(sink.astype(jnp.float32).reshape(1, KV_HEADS, G, 1, 1),
                                  (B, KV_HEADS, G, C, 1))
    p = jax.nn.softmax(jnp.concatenate([s, sink_logit], axis=-1), axis=-1)[..., :C].astype(vc.dtype)
    return jnp.einsum('bhgqk,bkhd->bqhgd', p, vc).reshape(B, C, Q_DIM)


def centred_shift(z, mu_prev, mu_next):
    prev = jnp.pad(z, ((0, 0), (1, 0), (0, 0)))[:, :-1]
    nxt = jnp.pad(z, ((0, 0), (0, 1), (0, 0)))[:, 1:]
    return z + mu_prev * (prev - z) + mu_next * (nxt - z)


def rwkv_prep(zb, w0, w2, a0, a2, k_k, k_a):
    zb = zb.astype(jnp.float32)
    B, L = zb.shape[0], zb.shape[1]
    r, k, v, wl, al, gl = jnp.split(zb, RWKV_SPLITS, axis=-1)
    heads = lambda t: t.reshape(B, L, RWKV_HEADS, RWKV_N)
    kk = heads(k * k_k)
    kk = kk * lax.rsqrt(jnp.sum(kk * kk, axis=-1, keepdims=True) + 1e-12)
    dirs = []
    for d in range(2):
        w_log = -jax.nn.softplus(-(w0[d] + jnp.tanh(wl) @ w2[d])) - 0.5
        decay = jnp.exp(-jnp.exp(w_log))
        a = jax.nn.sigmoid(a0[d] + al @ a2[d])
        kd = k * (1 + (a - 1) * k_a)
        dirs.append((heads(decay), heads(kd), heads(a)))
    return heads(r), heads(k), heads(v), kk, dirs, gl


def rwkv_scan(state0, r, decay, k, v, kk, a, reverse):
    def step(S, inp):
        r_t, w_t, k_t, v_t, kk_t, a_t = inp
        s_kk = jnp.einsum('bhvk,bhk->bhv', S, kk_t)
        S = (S * w_t[:, :, None, :]
             - s_kk[..., None] * (kk_t * a_t)[:, :, None, :]
             + v_t[..., None] * k_t[:, :, None, :])
        return S, jnp.einsum('bhvk,bhk->bhv', S, r_t)
    xs = tuple(jnp.moveaxis(t, 1, 0) for t in (r, decay, k, v, kk, a))
    state, ys = lax.scan(step, state0, xs, reverse=reverse)
    return state, jnp.moveaxis(ys, 0, 1)


def rwkv_readout(y, r, k, v, gl, g2, r_k, lnx_g, lnx_b):
    B, L = y.shape[0], y.shape[1]
    mean = jnp.mean(y, axis=-1, keepdims=True)
    var = jnp.mean(jnp.square(y - mean), axis=-1, keepdims=True)
    yn = ((y - mean) * lax.rsqrt(var + GN_EPS)).reshape(B, L, RWKV_DIM) * lnx_g + lnx_b
    bonus = (jnp.sum(r * k * r_k, axis=-1, keepdims=True) * v).reshape(B, L, RWKV_DIM)
    g = jax.nn.sigmoid(gl) @ g2
    return (yn + bonus) * g


def rwkv_mixer(zb, zbc, w0, w2, a0, a2, g2, k_k, k_a, r_k, lnx_g, lnx_b, need_ctx):
    rc, kc, vc, kkc, dirs_c, glc = rwkv_prep(zbc, w0, w2, a0, a2, k_k, k_a)
    r, k, v, kk, dirs, gl = rwkv_prep(zb, w0, w2, a0, a2, k_k, k_a)
    B = zb.shape[0]
    s0 = jnp.zeros((B, RWKV_HEADS, RWKV_N, RWKV_N), jnp.float32)
    y = 0.0
    yc = 0.0
    for d, rev in ((0, False), (1, True)):
        dec_c, kd_c, a_c = dirs_c[d]
        s_ctx, yc_d = rwkv_scan(s0, rc, dec_c, kd_c, vc, kkc, a_c, rev)
        dec, kd, a = dirs[d]
        _, y_d = rwkv_scan(s_ctx, r, dec, kd, v, kk, a, rev)
        y = y + y_d
        yc = yc + yc_d
    out = rwkv_readout(y, r, k, v, gl, g2, r_k, lnx_g, lnx_b)
    out_c = rwkv_readout(yc, rc, kc, vc, glc, g2, r_k, lnx_g, lnx_b) if need_ctx else None
    return out, out_c


def hybrid_ab_mixer(h, hc, w_in, w_out, sink, mu_prev, mu_next, w0, w2, a0, a2, g2,
                    k_k, k_a, r_k, lnx_g, lnx_b, ang_r, ang_c, need_ctx):
    B, T = h.shape[0], h.shape[1]
    C = hc.shape[1]
    z = h @ w_in
    zc = hc @ w_in
    q = axial_rope(z[..., :Q_DIM].reshape(B, T, ATT_HEADS, HEAD_DIM), ang_r, ang_c)
    k = axial_rope(z[..., Q_DIM:Q_DIM + KV_DIM].reshape(B, T, KV_HEADS, HEAD_DIM), ang_r, ang_c)
    v = z[..., Q_DIM + KV_DIM:ATT_COLS].reshape(B, T, KV_HEADS, HEAD_DIM)
    qc = zc[..., :Q_DIM].reshape(B, C, ATT_HEADS, HEAD_DIM)
    kc = zc[..., Q_DIM:Q_DIM + KV_DIM].reshape(B, C, KV_HEADS, HEAD_DIM)
    vc = zc[..., Q_DIM + KV_DIM:ATT_COLS].reshape(B, C, KV_HEADS, HEAD_DIM)
    att = window_attention(q, k, v, kc, vc, sink)
    rw, rwc = rwkv_mixer(centred_shift(z[..., ATT_COLS:], mu_prev, mu_next),
                         centred_shift(zc[..., ATT_COLS:], mu_prev, mu_next),
                         w0, w2, a0, a2, g2, k_k, k_a, r_k, lnx_g, lnx_b, need_ctx)
    out = jnp.concatenate([att, rw.astype(att.dtype)], axis=-1) @ w_out
    if need_ctx:
        att_c = context_attention(qc, kc, vc, sink)
        out_c = jnp.concatenate([att_c, rwc.astype(att_c.dtype)], axis=-1) @ w_out
    else:
        out_c = None
    return out, out_c


def fourier_mixer(h, w_out):
    B, L, D = h.shape
    hg = h.astype(jnp.float32).reshape(B, L, FOURIER_GROUPS, D // FOURIER_GROUPS)
    f = jnp.fft.fft2(hg, axes=(1, 3), norm="ortho").real
    return f.reshape(B, L, D).astype(h.dtype) @ w_out


def channel_mlp(h, w1, w2):
    return jnp.square(jax.nn.relu(h @ w1)) @ w2


def setup_inputs(seed: int = 0) -> dict:
    key = jax.random.key(seed)
    ks = iter(jax.random.split(key, 40))
    nrm = lambda shape, s: jax.random.normal(next(ks), shape, jnp.float32) * s
    uni = lambda shape: jax.random.uniform(next(ks), shape, jnp.float32, 0.0, 0.6)
    D = D_MODEL
    return {
        "x": nrm((BATCH, SEQ, D), 1.0),
        "c": nrm((BATCH, D), 1.0),
        "ctx": nrm((BATCH, CTX_LEN, D), 1.0),
        "c_ctx": nrm((D,), 1.0),
        "ada_w": nrm((DEPTH, D, N_MOD * D), 0.5 * D ** -0.5),
        "ada_b": nrm((DEPTH, N_MOD * D), 0.01),
        "norm1_g": 1.0 + nrm((DEPTH, D), 0.02),
        "norm2_g": 1.0 + nrm((DEPTH, D), 0.02),
        "mix_w_in": nrm((N_EVEN, D, IN_COLS), D ** -0.5),
        "mix_w_out": nrm((N_EVEN, Q_DIM + RWKV_DIM, D), (Q_DIM + RWKV_DIM) ** -0.5),
        "attn_sink": nrm((N_EVEN, ATT_HEADS), 0.5),
        "shift_mu_prev": uni((N_EVEN, RWKV_COLS)),
        "shift_mu_next": uni((N_EVEN, RWKV_COLS)),
        "decay_w0": nrm((N_EVEN, 2, RWKV_DIM), 0.5),
        "decay_w2": nrm((N_EVEN, 2, DECAY_LORA, RWKV_DIM), 0.1),
        "iclr_a0": nrm((N_EVEN, 2, RWKV_DIM), 0.5),
        "iclr_a2": nrm((N_EVEN, 2, ICLR_LORA, RWKV_DIM), 0.1),
        "gate_g2": nrm((N_EVEN, GATE_LORA, RWKV_DIM), GATE_LORA ** -0.5),
        "key_kk": 0.85 + nrm((N_EVEN, RWKV_DIM), 0.05),
        "key_ka": 1.0 + nrm((N_EVEN, RWKV_DIM), 0.05),
        "bonus_rk": nrm((N_EVEN, RWKV_HEADS, RWKV_N), 0.1),
        "lnx_g": 1.0 + nrm((N_EVEN, RWKV_DIM), 0.02),
        "lnx_b": nrm((N_EVEN, RWKV_DIM), 0.01),
        "fourier_w_out": nrm((N_ODD, D, D), D ** -0.5),
        "mlp_w1": nrm((DEPTH, D, D_FF), D ** -0.5),
        "mlp_w2": nrm((DEPTH, D_FF, D), D_FF ** -0.5),
        "final_g": 1.0 + nrm((D,), 0.02),
    }


def reference(x, c, ctx, c_ctx, ada_w, ada_b, norm1_g, norm2_g, mix_w_in, mix_w_out,
              attn_sink, shift_mu_prev, shift_mu_next, decay_w0, decay_w2, iclr_a0, iclr_a2,
              gate_g2, key_kk, key_ka, bonus_rk, lnx_g, lnx_b, fourier_w_out, mlp_w1, mlp_w2,
              final_g):
    T = x.shape[1]
    ang_r, ang_c = axial_angles(T)
    s_lat = jax.nn.silu(c)
    s_ctx = jax.nn.silu(c_ctx)
    for i in range(DEPTH):
        need_ctx = i < DEPTH - 1
        even = i % 2 == 0
        j = i // 2
        mod = (s_lat @ ada_w[i] + ada_b[i])[:, None, :]
        sh1, sc1, gt1, sh2, sc2, gt2 = jnp.split(mod, N_MOD, axis=-1)
        h = modulate(rmsnorm(x, norm1_g[i]), sh1, sc1)
        if even or need_ctx:
            mod_c = (s_ctx @ ada_w[i] + ada_b[i])[None, None, :]
            csh1, csc1, cgt1, csh2, csc2, cgt2 = jnp.split(mod_c, N_MOD, axis=-1)
            hc = modulate(rmsnorm(ctx, norm1_g[i]), csh1, csc1)
        if even:
            y, yc = hybrid_ab_mixer(h, hc, mix_w_in[j], mix_w_out[j], attn_sink[j],
                                    shift_mu_prev[j], shift_mu_next[j], decay_w0[j], decay_w2[j],
                                    iclr_a0[j], iclr_a2[j], gate_g2[j], key_kk[j], key_ka[j],
                                    bonus_rk[j], lnx_g[j], lnx_b[j], ang_r, ang_c, need_ctx)
        else:
            y = fourier_mixer(h, fourier_w_out[j])
            yc = fourier_mixer(hc, fourier_w_out[j]) if need_ctx else None
        x = x + gt1 * y
        x = x + gt2 * channel_mlp(modulate(rmsnorm(x, norm2_g[i]), sh2, sc2), mlp_w1[i], mlp_w2[i])
        if need_ctx:
            ctx = ctx + cgt1 * yc
            ctx = ctx + cgt2 * channel_mlp(modulate(rmsnorm(ctx, norm2_g[i]), csh2, csc2),
                                           mlp_w1[i], mlp_w2[i])
    return rmsnorm(x, final_g)
```

```python
import functools

import numpy as np
import jax
import jax.numpy as jnp
from jax import lax
from jax.experimental import pallas as pl
from jax.experimental.pallas import tpu as pltpu

F32 = jnp.float32
BF16 = jnp.bfloat16
HIGHEST = lax.Precision.HIGHEST

HEAD_DIM = 64
WINDOW = 128
QBLK = 128
GRID_W = 64
ROPE_BASE = 10000.0
RWKV_N = 64
DECAY_LORA = 64
ICLR_LORA = 64
GATE_LORA = 128
FOURIER_GROUPS = 4
N_MOD = 6
NORM_EPS = 1e-6
GN_EPS = 64e-5
NEG_INF = -1e30

CHUNK = 64
LANES = 128
FFT_L2 = 128
VMEM_LIMIT = 48 * 1024 * 1024


def _cparams(sem):
    return pltpu.CompilerParams(dimension_semantics=sem, vmem_limit_bytes=VMEM_LIMIT)


def _dot(a, b, **kw):
    return jnp.dot(a, b, preferred_element_type=F32, **kw)


def _dot_nt(a, b):
    return lax.dot_general(a, b, (((1,), (1,)), ((), ())), preferred_element_type=F32)


def _dot_tn(a, b):
    return lax.dot_general(a, b, (((0,), (0,)), ((), ())), preferred_element_type=F32)


def _split_dot(x, m_bf16, passes):
    acc = None
    rem = x
    for _ in range(passes):
        piece = rem.astype(BF16)
        term = _dot(piece, m_bf16)
        acc = term if acc is None else acc + term
        rem = rem - piece.astype(F32)
    return acc


def _rms_mod(x, g, sh, sc):
    ms = jnp.mean(x * x, axis=-1, keepdims=True)
    return (x * lax.rsqrt(ms + NORM_EPS)) * g * (1.0 + sc) + sh


def _ada_kernel(cond_ref, w_ref, b_ref, o_ref):
    s = cond_ref[...]
    s = s * jax.nn.sigmoid(s)
    o_ref[0] = _dot(s, w_ref[0], precision=HIGHEST) + b_ref[0]


def _ada_call(cond, ada_w, ada_b):
    depth, d, n = ada_w.shape
    tn = 1536
    return pl.pallas_call(
        _ada_kernel,
        grid=(depth, n // tn),
        in_specs=[pl.BlockSpec((8, d), lambda i, j: (0, 0)),
                  pl.BlockSpec((1, d, tn), lambda i, j: (i, 0, j)),
                  pl.BlockSpec((1, 1, tn), lambda i, j: (i, 0, j))],
        out_specs=pl.BlockSpec((1, 8, tn), lambda i, j: (i, 0, j)),
        out_shape=jax.ShapeDtypeStruct((depth, 8, n), F32),
        compiler_params=_cparams(("parallel", "parallel")),
    )(cond, ada_w, ada_b.reshape(depth, 1, n))


def _inproj_kernel(x_ref, g_ref, sh_ref, sc_ref, cos_ref, sin_ref, wa_ref, wr_ref,
                   q_ref, k_ref, v_ref, z_ref):
    h = _rms_mod(x_ref[...], g_ref[...], sh_ref[0], sc_ref[0]).astype(BF16)
    z_ref[...] = _dot(h, wr_ref[...])
    za = _dot(h, wa_ref[...])
    cos = cos_ref[...]
    sin = sin_ref[...]
    lane = lax.broadcasted_iota(jnp.int32, cos.shape, 1)
    first = (lane % 32) < 16
    nq = q_ref.shape[1] // LANES
    nk = k_ref.shape[1] // LANES
    for c in range(nq + nk):
        s = za[:, c * LANES:(c + 1) * LANES]
        partner = jnp.where(first, pltpu.roll(s, LANES - 16, 1), pltpu.roll(s, 16, 1))
        ro = s * cos + partner * sin
        if c < nq:
            q_ref[:, c * LANES:(c + 1) * LANES] = (ro * (HEAD_DIM ** -0.5)).astype(BF16)
        else:
            k_ref[:, (c - nq) * LANES:(c - nq + 1) * LANES] = ro.astype(BF16)
    v_ref[...] = za[:, (nq + nk) * LANES:].astype(BF16)


def _inproj_call(x2, g1, sh, sc, cos, sin, w_att, w_rw, rows_per_group, tm):
    r, d = x2.shape
    period = cos.shape[0]
    n_per = period // tm
    per_group = rows_per_group // tm
    na = w_att.shape[1]
    nr = w_rw.shape[1]
    nq, nkd = 512, 256
    row = lambda i: (i, 0)
    grp = lambda i: (i // per_group, 0, 0)
    return pl.pallas_call(
        _inproj_kernel,
        grid=(r // tm,),
        in_specs=[pl.BlockSpec((tm, d), row),
                  pl.BlockSpec((1, d), lambda i: (0, 0)),
                  pl.BlockSpec((1, 1, d), grp),
                  pl.BlockSpec((1, 1, d), grp),
                  pl.BlockSpec((tm, LANES), lambda i: (i % n_per, 0)),
                  pl.BlockSpec((tm, LANES), lambda i: (i % n_per, 0)),
                  pl.BlockSpec((d, na), lambda i: (0, 0)),
                  pl.BlockSpec((d, nr), lambda i: (0, 0))],
        out_specs=[pl.BlockSpec((tm, nq), row), pl.BlockSpec((tm, nkd), row),
                   pl.BlockSpec((tm, nkd), row), pl.BlockSpec((tm, nr), row)],
        out_shape=[jax.ShapeDtypeStruct((r, nq), BF16), jax.ShapeDtypeStruct((r, nkd), BF16),
                   jax.ShapeDtypeStruct((r, nkd), BF16), jax.ShapeDtypeStruct((r, nr), F32)],
        compiler_params=_cparams(("parallel",)),
    )(x2, g1, sh, sc, cos, sin, w_att, w_rw)


def _attn_kernel(*refs, n_loc, seq_len):
    q_ref = refs[0]
    k_refs = refs[1:1 + n_loc]
    v_refs = refs[1 + n_loc:1 + 2 * n_loc]
    kc_ref, vc_ref, sink_ref, o_ref = refs[1 + 2 * n_loc:]
    i = pl.program_id(1)
    nctx = kc_ref.shape[1]
    nkeys = n_loc * QBLK + nctx
    lane = lax.broadcasted_iota(jnp.int32, (QBLK, LANES), 1)
    low = lane < HEAD_DIM
    if n_loc:
        row = lax.broadcasted_iota(jnp.int32, (QBLK, nkeys), 0)
        col = lax.broadcasted_iota(jnp.int32, (QBLK, nkeys), 1)
        kpos = (i - 1) * QBLK + col
        qpos = i * QBLK + row
        valid = (col >= n_loc * QBLK) | ((jnp.abs(kpos - qpos) <= WINDOW) & (kpos >= 0) & (kpos < seq_len))
    zero = jnp.zeros((), BF16)
    for hk in range(2):
        ksl = slice(hk * LANES, (hk + 1) * LANES)
        keys = jnp.concatenate([kr[0, :, ksl] for kr in k_refs] + [kc_ref[0, :, ksl]], axis=0)
        vals = jnp.concatenate([vr[0, :, ksl] for vr in v_refs] + [vc_ref[0, :, ksl]], axis=0)
        for pp in range(2):
            p = hk * 2 + pp
            qp = q_ref[0, :, p * LANES:(p + 1) * LANES]
            outs = []
            for e in range(2):
                h = 2 * p + e
                qh = jnp.where(low if e == 0 else ~low, qp, zero)
                s = _dot_nt(qh, keys)
                if n_loc:
                    s = jnp.where(valid, s, NEG_INF)
                sk = sink_ref[h:h + 1, 0:1]
                m = jnp.maximum(jnp.max(s, axis=-1, keepdims=True), sk)
                pr = jnp.exp(s - m)
                den = jnp.sum(pr, axis=-1, keepdims=True) + jnp.exp(sk - m)
                o = _dot(pr.astype(BF16), vals)
                outs.append(o / den)
            o_ref[0, :, p * LANES:(p + 1) * LANES] = jnp.where(low, outs[0], outs[1]).astype(BF16)


def _attn_call(q, kd, vd, kc, vc, sinkb, n_loc):
    b, t, nq = q.shape
    nb = t // QBLK
    nctx = kc.shape[1]
    kw = kd.shape[2]
    qspec = pl.BlockSpec((1, QBLK, nq), lambda bb, i: (bb, i, 0))
    loc = []
    for off in (-1, 0, 1)[:n_loc]:
        loc.append(pl.BlockSpec((1, QBLK, kw), functools.partial(
            lambda bb, i, off: (bb, jnp.clip(i + off, 0, nb - 1), 0), off=off)))
    cspec = pl.BlockSpec((1, nctx, kw), lambda bb, i: (bb, 0, 0))
    args = [q] + [kd] * n_loc + [vd] * n_loc + [kc, vc, sinkb]
    return pl.pallas_call(
        functools.partial(_attn_kernel, n_loc=n_loc, seq_len=t),
        grid=(b, nb),
        in_specs=[qspec] + loc + loc + [cspec, cspec, pl.BlockSpec(sinkb.shape, lambda bb, i: (0, 0))],
        out_specs=qspec,
        out_shape=jax.ShapeDtypeStruct((b, t, nq), BF16),
        compiler_params=_cparams(("parallel", "parallel")),
    )(*args)


def _rwkv_dir(z, prow, nrow, is_first, is_last, mu, wa, w0a0, k_k, k_a, bd, tri, keep, eye,
              state_ref, d):
    c = z.shape[0]
    hd = k_k.shape[1]
    rowi = lax.broadcasted_iota(jnp.int32, z.shape, 0)
    zp = jnp.where(rowi == 0, jnp.where(is_first, 0.0, prow), pltpu.roll(z, 1, 0))
    zn = jnp.where(rowi == c - 1, jnp.where(is_last, 0.0, nrow), pltpu.roll(z, c - 1, 0))
    zs = z + mu[0:1] * (zp - z) + mu[1:2] * (zn - z)
    r = zs[:, 0:hd]
    k = zs[:, hd:2 * hd]
    v = zs[:, 2 * hd:3 * hd]
    wa_in = zs[:, 3 * hd:3 * hd + LANES]
    gl = zs[:, 3 * hd + LANES:]
    lane = lax.broadcasted_iota(jnp.int32, (c, LANES), 1)
    low = lane < DECAY_LORA
    tw = jnp.where(low, jnp.tanh(wa_in), wa_in)
    xwa = _dot(tw, wa, precision=HIGHEST) + w0a0
    xw = xwa[:, :hd]
    a = jax.nn.sigmoid(xwa[:, hd:])
    w_log = jnp.minimum(xw, 0.0) - jnp.log1p(jnp.exp(-jnp.abs(xw))) - 0.5
    lw = -jnp.exp(w_log)
    kkr = k * k_k
    kk = kkr * lax.rsqrt(_split_dot(kkr * kkr, bd, 2) + 1e-12)
    kd = k * (1.0 + (a - 1.0) * k_a)
    bb = kk * a
    cum = _split_dot_left(tri, lw, 3)
    total = cum[0:1] if d else cum[c - 1:c]
    e_in = jnp.exp(cum)
    e_ex = jnp.exp(cum - lw)
    e_neg = jnp.exp(-cum)
    e_rem = jnp.exp(total - cum)
    pc = jnp.exp(total)
    kt = kk * e_ex
    rt = r * e_in
    bt = bb * e_neg
    kdt = kd * e_neg
    bp = bb * e_rem
    kp = kd * e_rem

    def stack(xp):
        return jnp.concatenate([jnp.where(low, xp, 0.0), jnp.where(low, 0.0, xp)], axis=0)

    c2 = 2 * c
    ys = []
    for p in range(hd // LANES):
        sl = slice(p * LANES, (p + 1) * LANES)
        kts, rts, bts, kdts = stack(kt[:, sl]), stack(rt[:, sl]), stack(bt[:, sl]), stack(kdt[:, sl])
        vs, bps, kps = stack(v[:, sl]), stack(bp[:, sl]), stack(kp[:, sl])
        aa = _dot_nt(jnp.concatenate([kts, rts], axis=0).astype(BF16),
                     jnp.concatenate([bts, kdts], axis=0).astype(BF16))
        aa = jnp.where(keep, aa, 0.0)
        x = -aa[:c2, :c2]
        auk = aa[:c2, c2:]
        arr = aa[c2:, :]
        t = eye + x
        xp = x
        for _ in range(5):
            xb = xp.astype(BF16)
            xp = _dot(xb, xb)
            t = t + _dot(t.astype(BF16), xp.astype(BF16))
        av = _dot(auk.astype(BF16), vs.astype(BF16))
        twm = _dot(t.astype(BF16), jnp.concatenate([kts, av], axis=1).astype(BF16))
        w = twm[:, :LANES]
        uloc = -twm[:, LANES:]
        rhs5 = jnp.concatenate([jnp.concatenate([uloc, -w], axis=1),
                                jnp.concatenate([vs, jnp.zeros_like(vs)], axis=1)], axis=0)
        o5 = _dot(arr.astype(BF16), rhs5.astype(BF16))
        yloc = o5[:, :LANES]
        rhat = rts + o5[:, LANES:]
        s0 = state_ref[0, d, p]
        ws = _dot_nt(jnp.concatenate([w, rhat], axis=0).astype(BF16), s0.astype(BF16))
        u = uloc - ws[:c2]
        y = yloc + ws[c2:]
        snew = s0 * pc[:, sl] + _dot_tn(jnp.concatenate([u, vs], axis=0).astype(BF16),
                                         jnp.concatenate([bps, kps], axis=0).astype(BF16))
        state_ref[0, d, p] = snew
        ys.append(y[:c] + y[c:])
    return jnp.concatenate(ys, axis=1), r, k, v, gl


def _split_dot_left(m_bf16, x, passes):
    acc = None
    rem = x
    for _ in range(passes):
        piece = rem.astype(BF16)
        term = _dot(m_bf16, piece)
        acc = term if acc is None else acc + term
        rem = rem - piece.astype(F32)
    return acc


def _rwkv_kernel(zf_ref, zfp_ref, zfn_ref, zb_ref, zbp_ref, zbn_ref, s0_ref, mu_ref, wa_ref,
                 w0a0_ref, kk_ref, ka_ref, rk_ref, g2_ref, bd_ref,
                 yf_ref, yb_ref, bonus_ref, gate_ref, state_ref):
    j = pl.program_id(1)
    nc = pl.num_programs(1)
    c = zf_ref.shape[1]

    @pl.when(j == 0)
    def _():
        state_ref[...] = s0_ref[...]

    ri = lax.broadcasted_iota(jnp.int32, (c, c), 0)
    ci = lax.broadcasted_iota(jnp.int32, (c, c), 1)
    r4 = lax.broadcasted_iota(jnp.int32, (4 * c, 4 * c), 0)
    c4 = lax.broadcasted_iota(jnp.int32, (4 * c, 4 * c), 1)
    tt = r4 % c
    ss = c4 % c
    incl = jnp.where(r4 < 2 * c, 0, 1)
    r2 =lax.broadcasted_iota(jnp.int32, (2 * c, 2 * c), 0)
    c2 = lax.broadcasted_iota(jnp.int32, (2 * c, 2 * c), 1)
    eye = jnp.where(r2 == c2, 1.0, 0.0).astype(F32)
    mu = mu_ref[...]
    k_k = kk_ref[...]
    k_a = ka_ref[...]
    bd = bd_ref[...]

    tri_f = jnp.where(ci <= ri, 1.0, 0.0).astype(BF16)
    keep_f = ss < tt + incl
    y, r, k, v, gl = _rwkv_dir(zf_ref[0], zfp_ref[0, 7:8], zfn_ref[0, 0:1], j == 0, j == nc - 1,
                               mu, wa_ref[0], w0a0_ref[0], k_k, k_a, bd, tri_f, keep_f, eye,
                               state_ref, 0)
    yf_ref[0] = y
    bonus_ref[0] = _split_dot(r * k * rk_ref[...], bd, 2) * v
    gate_ref[0] = _dot(jax.nn.sigmoid(gl).astype(BF16), g2_ref[...])

    tri_b = jnp.where(ci >= ri, 1.0, 0.0).astype(BF16)
    keep_b = ss > tt - incl
    y, _, _, _, _ = _rwkv_dir(zb_ref[0], zbp_ref[0, 7:8], zbn_ref[0, 0:1], j == nc - 1, j == 0,
                              mu, wa_ref[1], w0a0_ref[1], k_k, k_a, bd, tri_b, keep_b, eye,
                              state_ref, 1)
    yb_ref[0] = y


def _rwkv_call(z, s0, mu, wa, w0a0, k_k, k_a, r_k, g2, bd):
    b, t, nz = z.shape
    c = CHUNK
    nc = t // c
    hd = k_k.shape[1]
    cb = c // 8
    nb8 = t // 8
    fwd = lambda bb, j: (bb, j, 0)
    bwd = lambda bb, j: (bb, nc - 1 - j, 0)
    const2 = lambda bb, j: (0, 0)
    const3 = lambda bb, j: (0, 0, 0)
    st = pl.BlockSpec((1,) + s0.shape[1:], lambda bb, j: (bb, 0, 0, 0, 0))
    ychunk = pl.BlockSpec((1, c, hd), fwd)
    return pl.pallas_call(
        _rwkv_kernel,
        grid=(b, nc),
        in_specs=[pl.BlockSpec((1, c, nz), fwd),
                  pl.BlockSpec((1, 8, nz), lambda bb, j: (bb, jnp.maximum(j * cb - 1, 0), 0)),
                  pl.BlockSpec((1, 8, nz), lambda bb, j: (bb, jnp.minimum((j + 1) * cb, nb8 - 1), 0)),
                  pl.BlockSpec((1, c, nz), bwd),
                  pl.BlockSpec((1, 8, nz), lambda bb, j: (bb, jnp.maximum((nc - 1 - j) * cb - 1, 0), 0)),
                  pl.BlockSpec((1, 8, nz), lambda bb, j: (bb, jnp.minimum((nc - j) * cb, nb8 - 1), 0)),
                  st,
                  pl.BlockSpec(mu.shape, const2),
                  pl.BlockSpec(wa.shape, const3),
                  pl.BlockSpec(w0a0.shape, const3),
                  pl.BlockSpec(k_k.shape, const2),
                  pl.BlockSpec(k_a.shape, const2),
                  pl.BlockSpec(r_k.shape, const2),
                  pl.BlockSpec(g2.shape, const2),
                  pl.BlockSpec(bd.shape, const2)],
        out_specs=[ychunk, pl.BlockSpec((1, c, hd), bwd), ychunk, ychunk, st],
        out_shape=[jax.ShapeDtypeStruct((b, t, hd), F32)] * 4 + [jax.ShapeDtypeStruct(s0.shape, F32)],
        compiler_params=_cparams(("parallel", "arbitrary")),
    )(z, z, z, z, z, z, s0, mu, wa, w0a0, k_k, k_a, r_k, g2, bd)


def _readout_kernel(att_ref, yf_ref, yb_ref, bonus_ref, gate_ref, x_ref, lg_ref, lb_ref, bd_ref,
                    wo_ref, gt_ref, g2_ref, sh_ref, sc_ref, xo_ref, h_ref):
    bd = bd_ref[...]
    inv_n = 1.0 / RWKV_N
    y = yf_ref[...] + yb_ref[...]
    mean = _split_dot(y, bd, 3) * inv_n
    yc = y - mean
    var = _split_dot(yc * yc, bd, 2) * inv_n
    yn = yc * lax.rsqrt(var + GN_EPS) * lg_ref[...] + lb_ref[...]
    rw = (yn + bonus_ref[...]) * gate_ref[...]
    cat = jnp.concatenate([att_ref[...], rw.astype(BF16)], axis=1)
    xm = x_ref[...] + gt_ref[0] * _dot(cat, wo_ref[...])
    xo_ref[...] = xm
    h_ref[...] = _rms_mod(xm, g2_ref[...], sh_ref[0], sc_ref[0]).astype(BF16)


def _readout_call(att, yf, yb, bonus, gate, x2, lnx_g, lnx_b, bd, w_out, gt1, g2, sh2, sc2,
                  rows_per_group, tm):
    r, d = x2.shape
    hd = yf.shape[1]
    per_group = rows_per_group // tm
    row = lambda i: (i, 0)
    c2 = lambda i: (0, 0)
    grp = lambda i: (i // per_group, 0, 0)
    half = pl.BlockSpec((tm, hd), row)
    full = pl.BlockSpec((tm, d), row)
    vec = pl.BlockSpec((1, 1, d), grp)
    return pl.pallas_call(
        _readout_kernel,
        grid=(r // tm,),
        in_specs=[half, half, half, half, half, full,
                  pl.BlockSpec((1, hd), c2), pl.BlockSpec((1, hd), c2), pl.BlockSpec(bd.shape, c2),
                  pl.BlockSpec(w_out.shape, c2), vec, pl.BlockSpec((1, d), c2), vec, vec],
        out_specs=[full, full],
        out_shape=[jax.ShapeDtypeStruct((r, d), F32), jax.ShapeDtypeStruct((r, d), BF16)],
        compiler_params=_cparams(("parallel",)),
    )(att, yf, yb, bonus, gate, x2, lnx_g, lnx_b, bd, w_out, gt1, g2, sh2, sc2)


def _mlp_kernel(h_ref, x_ref, w1_ref, w2_ref, gt_ref, gn_ref, shn_ref, scn_ref, xo_ref, ho_ref,
                acc_ref, *, final):
    j = pl.program_id(1)

    @pl.when(j == 0)
    def _():
        acc_ref[...] = jnp.zeros_like(acc_ref)

    a = jnp.maximum(_dot(h_ref[...], w1_ref[...]), 0.0)
    acc_ref[...] += _dot((a * a).astype(BF16), w2_ref[...])

    @pl.when(j == pl.num_programs(1) - 1)
    def _():
        xo = x_ref[...] + gt_ref[0] * acc_ref[...]
        if final:
            ms = jnp.mean(xo * xo, axis=-1, keepdims=True)
            xo_ref[...] = xo * lax.rsqrt(ms + NORM_EPS) * gn_ref[...]
            ho_ref[...] = jnp.zeros_like(ho_ref)
        else:
            xo_ref[...] = xo
            ho_ref[...] = _rms_mod(xo, gn_ref[...], shn_ref[0], scn_ref[0]).astype(BF16)


def _mlp_call(h, x2, w1, w2, gt2, g_next, sh_next, sc_next, rows_per_group, tm, tf, final):
    r, d = x2.shape
    ff = w1.shape[1]
    per_group = rows_per_group // tm
    row = lambda i, j: (i, 0)
    grp = lambda i, j: (i // per_group, 0, 0)
    vec = pl.BlockSpec((1, 1, d), grp)
    return pl.pallas_call(
        functools.partial(_mlp_kernel, final=final),
        grid=(r // tm, ff // tf),
        in_specs=[pl.BlockSpec((tm, d), row), pl.BlockSpec((tm, d), row),
                  pl.BlockSpec((d, tf), lambda i, j: (0, j)), pl.BlockSpec((tf, d), lambda i, j: (j, 0)),
                  vec, pl.BlockSpec((1, d), lambda i, j: (0, 0)), vec, vec],
        out_specs=[pl.BlockSpec((tm, d), row), pl.BlockSpec((tm, d), row)],
        out_shape=[jax.ShapeDtypeStruct((r, d), F32), jax.ShapeDtypeStruct((r, d), BF16)],
        scratch_shapes=[pltpu.VMEM((tm, d), F32)],
        compiler_params=_cparams(("parallel", "arbitrary")),
    )(h, x2, w1, w2, gt2, g_next, sh_next, sc_next)


def _fft1_kernel(h_ref, kr_ref, wc_ref, z_ref):
    l1, nb, d = h_ref.shape[1:]
    rows = l1 * nb
    hf = h_ref[0].reshape(rows, d)
    p = _dot(kr_ref[...], hf)
    gw = d // FOURIER_GROUPS
    wc = wc_ref[...]
    zr, zi = [], []
    for g in range(FOURIER_GROUPS):
        ap = jnp.concatenate([p[:rows, g * gw:(g + 1) * gw], p[rows:, g * gw:(g + 1) * gw]],
                             axis=1).astype(BF16)
        zz = _dot(ap, wc)
        zr.append(zz[:, :gw])
        zi.append(zz[:, gw:])
    z = jnp.concatenate(zr + zi, axis=1).astype(BF16)
    z_ref[0] = z.reshape(l1, nb, 2 * d)


def _fft1_call(h4, kr1, wc, nb):
    b, l1, l2, d = h4.shape
    return pl.pallas_call(
        _fft1_kernel,
        grid=(b, l2 // nb),
        in_specs=[pl.BlockSpec((1, l1, nb, d), lambda bb, j: (bb, 0, j, 0)),
                  pl.BlockSpec(kr1.shape, lambda bb, j: (0, 0)),
                  pl.BlockSpec(wc.shape, lambda bb, j: (0, 0))],
        out_specs=pl.BlockSpec((1, l1, nb, 2 * d), lambda bb, j: (bb, 0, j, 0)),
        out_shape=jax.ShapeDtypeStruct((b, l1, l2, 2 * d), BF16),
        compiler_params=_cparams(("parallel", "parallel")),
    )(h4, kr1, wc)


def _fft2_kernel(z_ref, gk_ref, x_ref, wo_ref, gt_ref, g2_ref, sh_ref, sc_ref, xo_ref, h_ref):
    mb, l2, d2 = z_ref.shape[1:]
    m2b = x_ref.shape[1]
    d = d2 // 2
    z = z_ref[0].reshape(mb * l2, d2)
    rhs = jnp.concatenate([z[:, :d], z[:, d:]], axis=0)
    f = _dot(gk_ref[0], rhs)
    y = _dot(f.astype(BF16), wo_ref[...])
    xm = x_ref[0].reshape(m2b * mb, d) + gt_ref[0] * y
    xo_ref[0] = xm.reshape(m2b, mb, d)
    h_ref[0] = _rms_mod(xm, g2_ref[...], sh_ref[0], sc_ref[0]).astype(BF16).reshape(m2b, mb, d)


def _fft2_call(z4, gk, x4, w_out, gt1, g2, sh2, sc2, mb, m2b):
    b, l1, l2, d2 = z4.shape
    d = d2 // 2
    nblk = l1 // mb
    xspec = pl.BlockSpec((1, m2b, mb, d), lambda m, bb, h: (bb, h, m, 0))
    vec = pl.BlockSpec((1, 1, d), lambda m, bb, h: (bb, 0, 0))
    return pl.pallas_call(
        _fft2_kernel,
        grid=(nblk, b, l2 // m2b),
        in_specs=[pl.BlockSpec((1, mb, l2, d2), lambda m, bb, h: (bb, m, 0, 0)),
                  pl.BlockSpec((1, m2b * mb, gk.shape[2]), lambda m, bb, h: (m, h, 0)),
                  xspec,
                  pl.BlockSpec(w_out.shape, lambda m, bb, h: (0, 0)),
                  vec, pl.BlockSpec((1, d), lambda m, bb, h: (0, 0)), vec, vec],
        out_specs=[xspec, xspec],
        out_shape=[jax.ShapeDtypeStruct((b, l2, l1, d), F32), jax.ShapeDtypeStruct((b, l2, l1, d), BF16)],
        compiler_params=_cparams(("parallel", "parallel", "parallel")),
    )(z4, gk, x4, w_out, gt1, g2, sh2, sc2)


def _rope_tables(t):
    axis_dim = HEAD_DIM // 2
    rows = t // GRID_W
    row = jnp.broadcast_to(jnp.arange(rows, dtype=F32)[:, None], (rows, GRID_W)).reshape(t)
    col = jnp.broadcast_to(jnp.arange(GRID_W, dtype=F32)[None, :], (rows, GRID_W)).reshape(t)
    inv = ROPE_BASE ** (-jnp.arange(0, axis_dim, 2, dtype=F32) / axis_dim)
    ang_r, ang_c = row[:, None] * inv, col[:, None] * inv
    cos = jnp.concatenate([jnp.cos(ang_r), jnp.cos(ang_r), jnp.cos(ang_c), jnp.cos(ang_c)], axis=1)
    sin = jnp.concatenate([-jnp.sin(ang_r), jnp.sin(ang_r), -jnp.sin(ang_c), jnp.sin(ang_c)], axis=1)
    return jnp.tile(cos, (1, 2)), jnp.tile(sin, (1, 2))


def _fft_tables(t, gw, nb, mb):
    l1 = t // FFT_L2
    n1 = np.arange(l1)
    ang1 = 2.0 * np.pi * np.outer(n1, n1) / l1
    eye = np.eye(nb)
    kr1 = np.concatenate([np.kron(np.cos(ang1), eye), np.kron(np.sin(ang1), eye)], axis=0)
    ch = np.arange(gw)
    angc = 2.0 * np.pi * np.outer(ch, ch) / gw
    cg, sg = np.cos(angc), np.sin(angc)
    wc = np.block([[cg, -sg], [-sg, -cg]])
    scale = 1.0 / np.sqrt(float(t) * gw)
    m = np.arange(t)
    n2 = np.arange(FFT_L2)
    theta = 2.0 * np.pi * np.outer(m, n2) / t
    cs = np.stack([np.cos(theta), np.sin(theta)], axis=0) * scale
    cs = cs.reshape(2, FFT_L2, l1 // mb, mb, FFT_L2)
    return (jnp.asarray(kr1, F32).astype(BF16), jnp.asarray(wc, F32).astype(BF16),
            jnp.asarray(cs, F32))


def _expand_gk(cs, mb):
    eye = jnp.eye(mb, dtype=F32)
    g = jnp.einsum("rmbpn,pq->bmprqn", cs, eye)
    nblk = cs.shape[2]
    return g.reshape(nblk, FFT_L2 * mb, 2 * mb * FFT_L2).astype(BF16)


def kernel(x, c, ctx, c_ctx, ada_w, ada_b, norm1_g, norm2_g, mix_w_in, mix_w_out, attn_sink,
           shift_mu_prev, shift_mu_next, decay_w0, decay_w2, iclr_a0, iclr_a2, gate_g2, key_kk,
           key_ka, bonus_rk, lnx_g, lnx_b, fourier_w_out, mlp_w1, mlp_w2, final_g):
    b, t, d = x.shape
    nctx = ctx.shape[1]
    hd = key_kk.shape[1]
    q_dim = d - hd
    n_heads = q_dim // HEAD_DIM
    kv_dim = (n_heads // 4) * HEAD_DIM
    att_cols = q_dim + 2 * kv_dim

    cond = jnp.zeros((8, d), F32).at[:b].set(c).at[b].set(c_ctx)
    mods = _ada_call(cond, ada_w, ada_b)
    lat = [mods[i, :b].reshape(b, N_MOD, 1, d) for i in range(2)]
    cmod = [mods[i, b:b + 1].reshape(1, N_MOD, 1, d) for i in range(2)]
    lm = lambda i, k: lat[i][:, k]
    cm = lambda i, k: cmod[i][:, k]
    row1 = lambda a: a.reshape(1, -1)

    w_in = mix_w_in[0]
    wk = w_in[:, q_dim:q_dim + kv_dim].reshape(d, kv_dim // HEAD_DIM, 1, HEAD_DIM)
    wv = w_in[:, q_dim + kv_dim:att_cols].reshape(d, kv_dim // HEAD_DIM, 1, HEAD_DIM)
    dup = lambda w: jnp.broadcast_to(w, (d, kv_dim // HEAD_DIM, 2, HEAD_DIM)).reshape(d, 2 * kv_dim)
    w_att = jnp.concatenate([w_in[:, :q_dim], dup(wk), dup(wv)], axis=1).astype(BF16)
    w_rw = w_in[:, att_cols:].astype(BF16)
    cos_t, sin_t = _rope_tables(t)
    cos_c, sin_c = jnp.ones((nctx, LANES), F32), jnp.zeros((nctx, LANES), F32)
    g1 = row1(norm1_g[0])
    x2 = x.reshape(b * t, d)
    ctx2 = ctx.reshape(b * nctx, d)
    tm_in = min(512, t)
    q, kd, vd, zrw = _inproj_call(x2, g1, lm(0, 0), lm(0, 1), cos_t, sin_t, w_att, w_rw, t, tm_in)
    qc, kc, vc, zrwc = _inproj_call(ctx2, g1, cm(0, 0), cm(0, 1), cos_c, sin_c, w_att, w_rw,
                                    b * nctx, nctx)
    q, kd, vd = (a.reshape(b, t, -1) for a in (q, kd, vd))
    qc, kc, vc = (a.reshape(b, nctx, -1) for a in (qc, kc, vc))
    sinkb = jnp.broadcast_to(attn_sink[0][:, None], (n_heads, LANES)).astype(F32)
    att = _attn_call(q, kd, vd, kc, vc, sinkb, 3)
    att_c = _attn_call(qc, kc, vc, kc, vc, sinkb, 0)

    mu = jnp.stack([shift_mu_prev[0], shift_mu_next[0]])
    zl = jnp.zeros((DECAY_LORA, hd), F32)
    wa = jnp.stack([jnp.concatenate([jnp.concatenate([decay_w2[0, dd], zl], axis=1),
                                     jnp.concatenate([zl, iclr_a2[0, dd]], axis=1)], axis=0)
                    for dd in range(2)])
    w0a0 = jnp.concatenate([decay_w0[0], iclr_a0[0]], axis=1).reshape(2, 1, 2 * hd)
    seg = np.arange(hd) // RWKV_N
    bd = jnp.asarray(seg[:, None] == seg[None, :], F32).astype(BF16)
    k_k, k_a, r_k = row1(key_kk[0]), row1(key_ka[0]), row1(bonus_rk[0])
    g2w = gate_g2[0].astype(BF16)
    s_zero = jnp.zeros((b, 2, hd // LANES, LANES, LANES), F32)
    yfc, ybc, bonus_c, gate_c, s_ctx = _rwkv_call(zrwc.reshape(b, nctx, -1), s_zero, mu, wa, w0a0,
                                                  k_k, k_a, r_k, g2w, bd)
    yf, yb, bonus, gate, _ = _rwkv_call(zrw.reshape(b, t, -1), s_ctx, mu, wa, w0a0,
                                        k_k, k_a, r_k, g2w, bd)

    w_out = mix_w_out[0].astype(BF16)
    n2g = row1(norm2_g[0])
    flat = lambda a: a.reshape(-1, a.shape[-1])
    xm, h2 = _readout_call(flat(att), flat(yf), flat(yb), flat(bonus), flat(gate), x2,
                           row1(lnx_g[0]), row1(lnx_b[0]), bd, w_out, lm(0, 2), n2g,
                           lm(0, 3), lm(0, 4), t, tm_in)
    xmc, h2c = _readout_call(flat(att_c), flat(yfc), flat(ybc), flat(bonus_c), flat(gate_c), ctx2,
                             row1(lnx_g[0]), row1(lnx_b[0]), bd, w_out, cm(0, 2), n2g,
                             cm(0, 3), cm(0, 4), b * nctx, nctx)
    w1 = mlp_w1[0].astype(BF16)
    w2 = mlp_w2[0].astype(BF16)
    g1n = row1(norm1_g[1])
    tm_mlp = min(1024, t)
    x1, h1 = _mlp_call(h2, xm, w1, w2, lm(0, 5), g1n, lm(1, 0), lm(1, 1), t, tm_mlp, 512, False)
    ctx1, _ = _mlp_call(h2c, xmc, w1, w2, cm(0, 5), g1n, cm(1, 0), cm(1, 1), b * nctx, nctx, 512,
                        False)
    del ctx1

    l1 = t // FFT_L2
    nb = 16
    mb = min(8, l1)
    gw = d // FOURIER_GROUPS
    kr1, wc, cs = _fft_tables(t, gw, nb, mb)
    gk = _expand_gk(cs, mb)
    z4 = _fft1_call(h1.reshape(b, l1, FFT_L2, d), kr1, wc, nb)
    xm4, h24 = _fft2_call(z4, gk, x1.reshape(b, FFT_L2, l1, d), fourier_w_out[0].astype(BF16),
                          lm(1, 2), row1(norm2_g[1]), lm(1, 3), lm(1, 4), mb, 64)
    out, _ = _mlp_call(h24.reshape(b * t, d), xm4.reshape(b * t, d), mlp_w1[1].astype(BF16),
                       mlp_w2[1].astype(BF16), lm(1, 5), row1(final_g), lm(1, 0), lm(1, 1),
                       t, tm_mlp, 512, True)
    return out.reshape(b, t, d)
```

```python
import functools

import numpy as np
import jax
import jax.numpy as jnp
from jax import lax
from jax.experimental import pallas as pl
from jax.experimental.pallas import tpu as pltpu

F32 = jnp.float32
BF16 = jnp.bfloat16
HIGHEST = lax.Precision.HIGHEST

HEAD_DIM = 64
WINDOW = 128
QBLK = 128
GRID_W = 64
ROPE_BASE = 10000.0
RWKV_N = 64
DECAY_LORA = 64
ICLR_LORA = 64
GATE_LORA = 128
FOURIER_GROUPS = 4
N_MOD = 6
NORM_EPS = 1e-6
GN_EPS = 64e-5
NEG_INF = -1e30

CHUNK = 64
LANES = 128
FFT_L2 = 128
VMEM_LIMIT = 48 * 1024 * 1024


def _cparams(sem):
    return pltpu.CompilerParams(dimension_semantics=sem, vmem_limit_bytes=VMEM_LIMIT)


def _dot(a, b, **kw):
    return jnp.dot(a, b, preferred_element_type=F32, **kw)


def _dot_nt(a, b):
    return lax.dot_general(a, b, (((1,), (1,)), ((), ())), preferred_element_type=F32)


def _dot_tn(a, b):
    return lax.dot_general(a, b, (((0,), (0,)), ((), ())), preferred_element_type=F32)


def _split_dot(x, m_bf16, passes):
    acc = None
    rem = x
    for _ in range(passes):
        piece = rem.astype(BF16)
        term = _dot(piece, m_bf16)
        acc = term if acc is None else acc + term
        rem = rem - piece.astype(F32)
    return acc


def _rms_mod(x, g, sh, sc):
    ms = jnp.mean(x * x, axis=-1, keepdims=True)
    return (x * lax.rsqrt(ms + NORM_EPS)) * g * (1.0 + sc) + sh


def _ada_kernel(cond_ref, w_ref, b_ref, o_ref):
    s = cond_ref[...]
    s = s * jax.nn.sigmoid(s)
    o_ref[0] = _dot(s, w_ref[0], precision=HIGHEST) + b_ref[0]


def _ada_call(cond, ada_w, ada_b):
    depth, d, n = ada_w.shape
    tn = 1536
    return pl.pallas_call(
        _ada_kernel,
        grid=(depth, n // tn),
        in_specs=[pl.BlockSpec((8, d), lambda i, j: (0, 0)),
                  pl.BlockSpec((1, d, tn), lambda i, j: (i, 0, j)),
                  pl.BlockSpec((1, 1, tn), lambda i, j: (i, 0, j))],
        out_specs=pl.BlockSpec((1, 8, tn), lambda i, j: (i, 0, j)),
        out_shape=jax.ShapeDtypeStruct((depth, 8, n), F32),
        compiler_params=_cparams(("parallel", "parallel")),
    )(cond, ada_w, ada_b.reshape(depth, 1, n))


def _inproj_kernel(x_ref, g_ref, sh_ref, sc_ref, cos_ref, sin_ref, wa_ref, wr_ref,
                   q_ref, k_ref, v_ref, z_ref):
    h = _rms_mod(x_ref[...], g_ref[...], sh_ref[0], sc_ref[0]).astype(BF16)
    z_ref[...] = _dot(h, wr_ref[...])
    za = _dot(h, wa_ref[...])
    cos = cos_ref[...]
    sin = sin_ref[...]
    lane = lax.broadcasted_iota(jnp.int32, cos.shape, 1)
    first = (lane % 32) < 16
    nq = q_ref.shape[1] // LANES
    nk = k_ref.shape[1] // LANES
    for c in range(nq + nk):
        s = za[:, c * LANES:(c + 1) * LANES]
        partner = jnp.where(first, pltpu.roll(s, LANES - 16, 1), pltpu.roll(s, 16, 1))
        ro = s * cos + partner * sin
        if c < nq:
            q_ref[:, c * LANES:(c + 1) * LANES] = (ro * (HEAD_DIM ** -0.5)).astype(BF16)
        else:
            k_ref[:, (c - nq) * LANES:(c - nq + 1) * LANES] = ro.astype(BF16)
    v_ref[...] = za[:, (nq + nk) * LANES:].astype(BF16)


def _inproj_call(x2, g1, sh, sc, cos, sin, w_att, w_rw, rows_per_group, tm):
    r, d = x2.shape
    period = cos.shape[0]
    n_per = period // tm
    per_group = rows_per_group // tm
    na = w_att.shape[1]
    nr = w_rw.shape[1]
    nq, nkd = 512, 256
    row = lambda i: (i, 0)
    grp = lambda i: (i // per_group, 0, 0)
    return pl.pallas_call(
        _inproj_kernel,
        grid=(r // tm,),
        in_specs=[pl.BlockSpec((tm, d), row),
                  pl.BlockSpec((1, d), lambda i: (0, 0)),
                  pl.BlockSpec((1, 1, d), grp),
                  pl.BlockSpec((1, 1, d), grp),
                  pl.BlockSpec((tm, LANES), lambda i: (i % n_per, 0)),
                  pl.BlockSpec((tm, LANES), lambda i: (i % n_per, 0)),
                  pl.BlockSpec((d, na), lambda i: (0, 0)),
                  pl.BlockSpec((d, nr), lambda i: (0, 0))],
        out_specs=[pl.BlockSpec((tm, nq), row), pl.BlockSpec((tm, nkd), row),
                   pl.BlockSpec((tm, nkd), row), pl.BlockSpec((tm, nr), row)],
        out_shape=[jax.ShapeDtypeStruct((r, nq), BF16), jax.ShapeDtypeStruct((r, nkd), BF16),
                   jax.ShapeDtypeStruct((r, nkd), BF16), jax.ShapeDtypeStruct((r, nr), F32)],
        compiler_params=_cparams(("parallel",)),
    )(x2, g1, sh, sc, cos, sin, w_att, w_rw)


def _attn_kernel(*refs, n_loc, seq_len):
    q_ref = refs[0]
    k_refs = refs[1:1 + n_loc]
    v_refs = refs[1 + n_loc:1 + 2 * n_loc]
    kc_ref, vc_ref, sink_ref, o_ref = refs[1 + 2 * n_loc:]
    i = pl.program_id(1)
    nctx = kc_ref.shape[1]
    nkeys = n_loc * QBLK + nctx
    lane = lax.broadcasted_iota(jnp.int32, (QBLK, LANES), 1)
    low = lane < HEAD_DIM
    if n_loc:
        row = lax.broadcasted_iota(jnp.int32, (QBLK, nkeys), 0)
        col = lax.broadcasted_iota(jnp.int32, (QBLK, nkeys), 1)
        kpos = (i - 1) * QBLK + col
        qpos = i * QBLK + row
        valid = (col >= n_loc * QBLK) | ((jnp.abs(kpos - qpos) <= WINDOW) & (kpos >= 0) & (kpos < seq_len))
    zero = jnp.zeros((), BF16)
    for hk in range(2):
        ksl = slice(hk * LANES, (hk + 1) * LANES)
        keys = jnp.concatenate([kr[0, :, ksl] for kr in k_refs] + [kc_ref[0, :, ksl]], axis=0)
        vals = jnp.concatenate([vr[0, :, ksl] for vr in v_refs] + [vc_ref[0, :, ksl]], axis=0)
        for pp in range(2):
            p = hk * 2 + pp
            qp = q_ref[0, :, p * LANES:(p + 1) * LANES]
            outs = []
            for e in range(2):
                h = 2 * p + e
                qh = jnp.where(low if e == 0 else ~low, qp, zero)
                s = _dot_nt(qh, keys)
                if n_loc:
                    s = jnp.where(valid, s, NEG_INF)
                sk = sink_ref[h:h + 1, 0:1]
                m = jnp.maximum(jnp.max(s, axis=-1, keepdims=True), sk)
                pr = jnp.exp(s - m)
                den = jnp.sum(pr, axis=-1, keepdims=True) + jnp.exp(sk - m)
                o = _dot(pr.astype(BF16), vals)
                outs.append(o / den)
            o_ref[0, :, p * LANES:(p + 1) * LANES] = jnp.where(low, outs[0], outs[1]).astype(BF16)


def _attn_call(q, kd, vd, kc, vc, sinkb, n_loc):
    b, t, nq = q.shape
    nb = t // QBLK
    nctx = kc.shape[1]
    kw = kd.shape[2]
    qspec = pl.BlockSpec((1, QBLK, nq), lambda bb, i: (bb, i, 0))
    loc = []
    for off in (-1, 0, 1)[:n_loc]:
        loc.append(pl.BlockSpec((1, QBLK, kw), functools.partial(
            lambda bb, i, off: (bb, jnp.clip(i + off, 0, nb - 1), 0), off=off)))
    cspec = pl.BlockSpec((1, nctx, kw), lambda bb, i: (bb, 0, 0))
    args = [q] + [kd] * n_loc + [vd] * n_loc + [kc, vc, sinkb]
    return pl.pallas_call(
        functools.partial(_attn_kernel, n_loc=n_loc, seq_len=t),
        grid=(b, nb),
        in_specs=[qspec] + loc + loc + [cspec, cspec, pl.BlockSpec(sinkb.shape, lambda bb, i: (0, 0))],
        out_specs=qspec,
        out_shape=jax.ShapeDtypeStruct((b, t, nq), BF16),
        compiler_params=_cparams(("parallel", "parallel")),
    )(*args)


def _rwkv_prep(z, prow, nrow, is_first, is_last, mu, wa, w0a0, k_k, k_a, bd, tri, keep, d):
    c = z.shape[0]
    hd = k_k.shape[1]
    rowi = lax.broadcasted_iota(jnp.int32, z.shape, 0)
    zp = jnp.where(rowi == 0, jnp.where(is_first, 0.0, prow), pltpu.roll(z, 1, 0))
    zn = jnp.where(rowi == c - 1, jnp.where(is_last, 0.0, nrow), pltpu.roll(z, c - 1, 0))
    zs = z + mu[0:1] * (zp - z) + mu[1:2] * (zn - z)
    r = zs[:, 0:hd]
    k = zs[:, hd:2 * hd]
    v = zs[:, 2 * hd:3 * hd]
    wa_in = zs[:, 3 * hd:3 * hd + LANES]
    gl = zs[:, 3 * hd + LANES:]
    lane = lax.broadcasted_iota(jnp.int32, (c, LANES), 1)
    low = lane < DECAY_LORA
    tw = jnp.where(low, jnp.tanh(wa_in), wa_in)
    xwa = _dot(tw, wa, precision=HIGHEST) + w0a0
    xw = xwa[:, :hd]
    a = jax.nn.sigmoid(xwa[:, hd:])
    w_log = jnp.minimum(xw, 0.0) - jnp.log1p(jnp.exp(-jnp.abs(xw))) - 0.5
    lw = -jnp.exp(w_log)
    kkr = k * k_k
    kk = kkr * lax.rsqrt(_split_dot(kkr * kkr, bd, 2) + 1e-12)
    kd = k * (1.0 + (a - 1.0) * k_a)
    bb = kk * a
    cum = _split_dot_left(tri, lw, 3)
    total = cum[0:1] if d else cum[c - 1:c]
    e_in = jnp.exp(cum)
    e_ex = jnp.exp(cum - lw)
    e_neg = jnp.exp(-cum)
    e_rem = jnp.exp(total - cum)
    pc = jnp.exp(total)
    kt = kk * e_ex
    rt = r * e_in
    bt = bb * e_neg
    kdt = kd * e_neg
    bp = bb * e_rem
    kp = kd * e_rem

    def stack(xp):
        return jnp.concatenate([jnp.where(low, xp, 0.0), jnp.where(low, 0.0, xp)], axis=0)

    chains = []
    for p in range(hd // LANES):
        sl = slice(p * LANES, (p + 1) * LANES)
        chains.append(dict(
            d=d, p=p, keep=keep, pc=pc[:, sl],
            kts=stack(kt[:, sl]), rts=stack(rt[:, sl]), bts=stack(bt[:, sl]), kdts=stack(kdt[:, sl]),
            vs=stack(v[:, sl]), bps=stack(bp[:, sl]), kps=stack(kp[:, sl])))
    return chains, r, k, v, gl


def _rwkv_solve(chains, eye, state_ref, c):
    c2 = 2 * c
    bf = lambda a: a.astype(BF16)
    for ch in chains:
        aa = _dot_nt(bf(jnp.concatenate([ch["kts"], ch["rts"]], axis=0)),
                     bf(jnp.concatenate([ch["bts"], ch["kdts"]], axis=0)))
        aa = jnp.where(ch["keep"], aa, 0.0)
        ch["xp"] = -aa[:c2, :c2]
        ch["auk"] = aa[:c2, c2:]
        ch["arr"] = aa[c2:, :]
        ch["t"] = eye + ch["xp"]
    for _ in range(5):
        for ch in chains:
            xb = bf(ch["xp"])
            ch["xp"] = _dot(xb, xb)
        for ch in chains:
            ch["t"] = ch["t"] + _dot(bf(ch["t"]), bf(ch["xp"]))
    for ch in chains:
        ch["av"] = _dot(bf(ch["auk"]), bf(ch["vs"]))
    for ch in chains:
        twm = _dot(bf(ch["t"]), bf(jnp.concatenate([ch["kts"], ch["av"]], axis=1)))
        ch["w"] = twm[:, :LANES]
        ch["uloc"] = -twm[:, LANES:]
    for ch in chains:
        vs = ch["vs"]
        rhs5 = jnp.concatenate([jnp.concatenate([ch["uloc"], -ch["w"]], axis=1),
                                jnp.concatenate([vs, jnp.zeros_like(vs)], axis=1)], axis=0)
        o5 = _dot(bf(ch["arr"]), bf(rhs5))
        ch["yloc"] = o5[:, :LANES]
        ch["rhat"] = ch["rts"] + o5[:, LANES:]
    for ch in chains:
        ch["s0"] = state_ref[0, ch["d"], ch["p"]]
        ch["ws"] = _dot_nt(bf(jnp.concatenate([ch["w"], ch["rhat"]], axis=0)), bf(ch["s0"]))
    ys = {}
    for ch in chains:
        ws = ch["ws"]
        u = ch["uloc"] - ws[:c2]
        y = ch["yloc"] + ws[c2:]
        snew = ch["s0"] * ch["pc"] + _dot_tn(bf(jnp.concatenate([u, ch["vs"]], axis=0)),
                                             bf(jnp.concatenate([ch["bps"], ch["kps"]], axis=0)))
        state_ref[0, ch["d"], ch["p"]] = snew
        ys.setdefault(ch["d"], []).append(y[:c] + y[c:])
    return [jnp.concatenate(ys[d], axis=1) for d in sorted(ys)]


def _split_dot_left(m_bf16, x, passes):
    acc = None
    rem = x
    for _ in range(passes):
        piece = rem.astype(BF16)
        term = _dot(m_bf16, piece)
        acc = term if acc is None else acc + term
        rem = rem - piece.astype(F32)
    return acc


def _rwkv_kernel(zf_ref, zfp_ref, zfn_ref, zb_ref, zbp_ref, zbn_ref, s0_ref, mu_ref, wa_ref,
                 w0a0_ref, kk_ref, ka_ref, rk_ref, g2_ref, bd_ref,
                 yf_ref, yb_ref, bonus_ref, gate_ref, state_ref):
    j = pl.program_id(1)
    nc = pl.num_programs(1)
    c = zf_ref.shape[1]

    @pl.when(j == 0)
    def _():
        state_ref[...] = s0_ref[...]

    ri = lax.broadcasted_iota(jnp.int32, (c, c), 0)
    ci = lax.broadcasted_iota(jnp.int32, (c, c), 1)
    r4 = lax.broadcasted_iota(jnp.int32, (4 * c, 4 * c), 0)
    c4 = lax.broadcasted_iota(jnp.int32, (4 * c, 4 * c), 1)
    tt = r4 % c
    ss = c4 % c
    incl = jnp.where(r4 < 2 * c, 0, 1)
    r2 =lax.broadcasted_iota(jnp.int32, (2 * c, 2 * c), 0)
    c2 = lax.broadcasted_iota(jnp.int32, (2 * c, 2 * c), 1)
    eye = jnp.where(r2 == c2, 1.0, 0.0).astype(F32)
    mu = mu_ref[...]
    k_k = kk_ref[...]
    k_a = ka_ref[...]
    bd = bd_ref[...]

    tri_f = jnp.where(ci <= ri, 1.0, 0.0).astype(BF16)
    keep_f = ss < tt + incl
    ch_f, r, k, v, gl = _rwkv_prep(zf_ref[0], zfp_ref[0, 7:8], zfn_ref[0, 0:1], j == 0, j == nc - 1,
                                   mu, wa_ref[0], w0a0_ref[0], k_k, k_a, bd, tri_f, keep_f, 0)
    bonus_ref[0] = _split_dot(r * k * rk_ref[...], bd, 2) * v
    gate_ref[0] = _dot(jax.nn.sigmoid(gl).astype(BF16), g2_ref[...])

    tri_b = jnp.where(ci >= ri, 1.0, 0.0).astype(BF16)
    keep_b = ss > tt - incl
    ch_b, _, _, _, _ = _rwkv_prep(zb_ref[0], zbp_ref[0, 7:8], zbn_ref[0, 0:1], j == nc - 1, j == 0,
                                  mu, wa_ref[1], w0a0_ref[1], k_k, k_a, bd, tri_b, keep_b, 1)
    chains = [ch for pair in zip(ch_f, ch_b) for ch in pair]
    y_f, y_b = _rwkv_solve(chains, eye, state_ref, c)
    yf_ref[0] = y_f
    yb_ref[0] = y_b


def _rwkv_call(z, s0, mu, wa, w0a0, k_k, k_a, r_k, g2, bd):
    b, t, nz = z.shape
    c = CHUNK
    nc = t // c
    hd = k_k.shape[1]
    cb = c // 8
    nb8 = t // 8
    fwd = lambda bb, j: (bb, j, 0)
    bwd = lambda bb, j: (bb, nc - 1 - j, 0)
    const2 = lambda bb, j: (0, 0)
    const3 = lambda bb, j: (0, 0, 0)
    st = pl.BlockSpec((1,) + s0.shape[1:], lambda bb, j: (bb, 0, 0, 0, 0))
    ychunk = pl.BlockSpec((1, c, hd), fwd)
    return pl.pallas_call(
        _rwkv_kernel,
        grid=(b, nc),
        in_specs=[pl.BlockSpec((1, c, nz), fwd),
                  pl.BlockSpec((1, 8, nz), lambda bb, j: (bb, jnp.maximum(j * cb - 1, 0), 0)),
                  pl.BlockSpec((1, 8, nz), lambda bb, j: (bb, jnp.minimum((j + 1) * cb, nb8 - 1), 0)),
                  pl.BlockSpec((1, c, nz), bwd),
                  pl.BlockSpec((1, 8, nz), lambda bb, j: (bb, jnp.maximum((nc - 1 - j) * cb - 1, 0), 0)),
                  pl.BlockSpec((1, 8, nz), lambda bb, j: (bb, jnp.minimum((nc - j) * cb, nb8 - 1), 0)),
                  st,
                  pl.BlockSpec(mu.shape, const2),
                  pl.BlockSpec(wa.shape, const3),
                  pl.BlockSpec(w0a0.shape, const3),
                  pl.BlockSpec(k_k.shape, const2),
                  pl.BlockSpec(k_a.shape, const2),
                  pl.BlockSpec(r_k.shape, const2),
                  pl.BlockSpec(g2.shape, const2),
                  pl.BlockSpec(bd.shape, const2)],
        out_specs=[ychunk, pl.BlockSpec((1, c, hd), bwd), ychunk, ychunk, st],
        out_shape=[jax.ShapeDtypeStruct((b, t, hd), F32)] * 4 + [jax.ShapeDtypeStruct(s0.shape, F32)],
        compiler_params=_cparams(("parallel", "arbitrary")),
    )(z, z, z, z, z, z, s0, mu, wa, w0a0, k_k, k_a, r_k, g2, bd)


def _readout_kernel(att_ref, yf_ref, yb_ref, bonus_ref, gate_ref, x_ref, lg_ref, lb_ref, bd_ref,
                    wo_ref, gt_ref, g2_ref, sh_ref, sc_ref, xo_ref, h_ref):
    bd = bd_ref[...]
    inv_n = 1.0 / RWKV_N
    y = yf_ref[...] + yb_ref[...]
    mean = _split_dot(y, bd, 3) * inv_n
    yc = y - mean
    var = _split_dot(yc * yc, bd, 2) * inv_n
    yn = yc * lax.rsqrt(var + GN_EPS) * lg_ref[...] + lb_ref[...]
    rw = (yn + bonus_ref[...]) * gate_ref[...]
    cat = jnp.concatenate([att_ref[...], rw.astype(BF16)], axis=1)
    xm = x_ref[...] + gt_ref[0] * _dot(cat, wo_ref[...])
    xo_ref[...] = xm
    h_ref[...] = _rms_mod(xm, g2_ref[...], sh_ref[0], sc_ref[0]).astype(BF16)


def _readout_call(att, yf, yb, bonus, gate, x2, lnx_g, lnx_b, bd, w_out, gt1, g2, sh2, sc2,
                  rows_per_group, tm):
    r, d = x2.shape
    hd = yf.shape[1]
    per_group = rows_per_group // tm
    row = lambda i: (i, 0)
    c2 = lambda i: (0, 0)
    grp = lambda i: (i // per_group, 0, 0)
    half = pl.BlockSpec((tm, hd), row)
    full = pl.BlockSpec((tm, d), row)
    vec = pl.BlockSpec((1, 1, d), grp)
    return pl.pallas_call(
        _readout_kernel,
        grid=(r // tm,),
        in_specs=[half, half, half, half, half, full,
                  pl.BlockSpec((1, hd), c2), pl.BlockSpec((1, hd), c2), pl.BlockSpec(bd.shape, c2),
                  pl.BlockSpec(w_out.shape, c2), vec, pl.BlockSpec((1, d), c2), vec, vec],
        out_specs=[full, full],
        out_shape=[jax.ShapeDtypeStruct((r, d), F32), jax.ShapeDtypeStruct((r, d), BF16)],
        compiler_params=_cparams(("parallel",)),
    )(att, yf, yb, bonus, gate, x2, lnx_g, lnx_b, bd, w_out, gt1, g2, sh2, sc2)


def _mlp_kernel(h_ref, x_ref, w1_ref, w2_ref, gt_ref, gn_ref, shn_ref, scn_ref, xo_ref, ho_ref,
                acc_ref, *, final):
    j = pl.program_id(1)

    @pl.when(j == 0)
    def _():
        acc_ref[...] = jnp.zeros_like(acc_ref)

    a = jnp.maximum(_dot(h_ref[...], w1_ref[...]), 0.0)
    acc_ref[...] += _dot((a * a).astype(BF16), w2_ref[...])

    @pl.when(j == pl.num_programs(1) - 1)
    def _():
        xo = x_ref[...] + gt_ref[0] * acc_ref[...]
        if final:
            ms = jnp.mean(xo * xo, axis=-1, keepdims=True)
            xo_ref[...] = xo * lax.rsqrt(ms + NORM_EPS) * gn_ref[...]
            ho_ref[...] = jnp.zeros_like(ho_ref)
        else:
            xo_ref[...] = xo
            ho_ref[...] = _rms_mod(xo, gn_ref[...], shn_ref[0], scn_ref[0]).astype(BF16)


def _mlp_call(h, x2, w1, w2, gt2, g_next, sh_next, sc_next, rows_per_group, tm, tf, final):
    r, d = x2.shape
    ff = w1.shape[1]
    per_group = rows_per_group // tm
    row = lambda i, j: (i, 0)
    grp = lambda i, j: (i // per_group, 0, 0)
    vec = pl.BlockSpec((1, 1, d), grp)
    return pl.pallas_call(
        functools.partial(_mlp_kernel, final=final),
        grid=(r // tm, ff // tf),
        in_specs=[pl.BlockSpec((tm, d), row), pl.BlockSpec((tm, d), row),
                  pl.BlockSpec((d, tf), lambda i, j: (0, j)), pl.BlockSpec((tf, d), lambda i, j: (j, 0)),
                  vec, pl.BlockSpec((1, d), lambda i, j: (0, 0)), vec, vec],
        out_specs=[pl.BlockSpec((tm, d), row), pl.BlockSpec((tm, d), row)],
        out_shape=[jax.ShapeDtypeStruct((r, d), F32), jax.ShapeDtypeStruct((r, d), BF16)],
        scratch_shapes=[pltpu.VMEM((tm, d), F32)],
        compiler_params=_cparams(("parallel", "arbitrary")),
    )(h, x2, w1, w2, gt2, g_next, sh_next, sc_next)


def _fft1_kernel(h_ref, kr_ref, wc_ref, z_ref):
    l1, nb, d = h_ref.shape[1:]
    rows = l1 * nb
    hf = h_ref[0].reshape(rows, d)
    p = _dot(kr_ref[...], hf)
    gw = d // FOURIER_GROUPS
    wc = wc_ref[...]
    zr, zi = [], []
    for g in range(FOURIER_GROUPS):
        ap = jnp.concatenate([p[:rows, g * gw:(g + 1) * gw], p[rows:, g * gw:(g + 1) * gw]],
                             axis=1).astype(BF16)
        zz = _dot(ap, wc)
        zr.append(zz[:, :gw])
        zi.append(zz[:, gw:])
    z = jnp.concatenate(zr + zi, axis=1).astype(BF16)
    z_ref[0] = z.reshape(l1, nb, 2 * d)


def _fft1_call(h4, kr1, wc, nb):
    b, l1, l2, d = h4.shape
    return pl.pallas_call(
        _fft1_kernel,
        grid=(b, l2 // nb),
        in_specs=[pl.BlockSpec((1, l1, nb, d), lambda bb, j: (bb, 0, j, 0)),
                  pl.BlockSpec(kr1.shape, lambda bb, j: (0, 0)),
                  pl.BlockSpec(wc.shape, lambda bb, j: (0, 0))],
        out_specs=pl.BlockSpec((1, l1, nb, 2 * d), lambda bb, j: (bb, 0, j, 0)),
        out_shape=jax.ShapeDtypeStruct((b, l1, l2, 2 * d), BF16),
        compiler_params=_cparams(("parallel", "parallel")),
    )(h4, kr1, wc)


def _fft2_kernel(z_ref, gk_ref, x_ref, wo_ref, gt_ref, g2_ref, sh_ref, sc_ref, xo_ref, h_ref):
    mb, l2, d2 = z_ref.shape[1:]
    m2b = x_ref.shape[1]
    d = d2 // 2
    z = z_ref[0].reshape(mb * l2, d2)
    rhs = jnp.concatenate([z[:, :d], z[:, d:]], axis=0)
    f = _dot(gk_ref[0], rhs)
    y = _dot(f.astype(BF16), wo_ref[...])
    xm = x_ref[0].reshape(m2b * mb, d) + gt_ref[0] * y
    xo_ref[0] = xm.reshape(m2b, mb, d)
    h_ref[0] = _rms_mod(xm, g2_ref[...], sh_ref[0], sc_ref[0]).astype(BF16).reshape(m2b, mb, d)


def _fft2_call(z4, gk, x4, w_out, gt1, g2, sh2, sc2, mb, m2b):
    b, l1, l2, d2 = z4.shape
    d = d2 // 2
    nblk = l1 // mb
    xspec = pl.BlockSpec((1, m2b, mb, d), lambda m, bb, h: (bb, h, m, 0))
    vec = pl.BlockSpec((1, 1, d), lambda m, bb, h: (bb, 0, 0))
    return pl.pallas_call(
        _fft2_kernel,
        grid=(nblk, b, l2 // m2b),
        in_specs=[pl.BlockSpec((1, mb, l2, d2), lambda m, bb, h: (bb, m, 0, 0)),
                  pl.BlockSpec((1, m2b * mb, gk.shape[2]), lambda m, bb, h: (m, h, 0)),
                  xspec,
                  pl.BlockSpec(w_out.shape, lambda m, bb, h: (0, 0)),
                  vec, pl.BlockSpec((1, d), lambda m, bb, h: (0, 0)), vec, vec],
        out_specs=[xspec, xspec],
        out_shape=[jax.ShapeDtypeStruct((b, l2, l1, d), F32), jax.ShapeDtypeStruct((b, l2, l1, d), BF16)],
        compiler_params=_cparams(("parallel", "parallel", "parallel")),
    )(z4, gk, x4, w_out, gt1, g2, sh2, sc2)


def _rope_tables(t):
    axis_dim = HEAD_DIM // 2
    rows = t // GRID_W
    row = jnp.broadcast_to(jnp.arange(rows, dtype=F32)[:, None], (rows, GRID_W)).reshape(t)
    col = jnp.broadcast_to(jnp.arange(GRID_W, dtype=F32)[None, :], (rows, GRID_W)).reshape(t)
    inv = ROPE_BASE ** (-jnp.arange(0, axis_dim, 2, dtype=F32) / axis_dim)
    ang_r, ang_c = row[:, None] * inv, col[:, None] * inv
    cos = jnp.concatenate([jnp.cos(ang_r), jnp.cos(ang_r), jnp.cos(ang_c), jnp.cos(ang_c)], axis=1)
    sin = jnp.concatenate([-jnp.sin(ang_r), jnp.sin(ang_r), -jnp.sin(ang_c), jnp.sin(ang_c)], axis=1)
    return jnp.tile(cos, (1, 2)), jnp.tile(sin, (1, 2))


def _fft_tables(t, gw, nb, mb):
    l1 = t // FFT_L2
    n1 = np.arange(l1)
    ang1 = 2.0 * np.pi * np.outer(n1, n1) / l1
    eye = np.eye(nb)
    kr1 = np.concatenate([np.kron(np.cos(ang1), eye), np.kron(np.sin(ang1), eye)], axis=0)
    ch = np.arange(gw)
    angc = 2.0 * np.pi * np.outer(ch, ch) / gw
    cg, sg = np.cos(angc), np.sin(angc)
    wc = np.block([[cg, -sg], [-sg, -cg]])
    scale = 1.0 / np.sqrt(float(t) * gw)
    m = np.arange(t)
    n2 = np.arange(FFT_L2)
    theta = 2.0 * np.pi * np.outer(m, n2) / t
    cs = np.stack([np.cos(theta), np.sin(theta)], axis=0) * scale
    cs = cs.reshape(2, FFT_L2, l1 // mb, mb, FFT_L2)
    return (jnp.asarray(kr1, F32).astype(BF16), jnp.asarray(wc, F32).astype(BF16),
            jnp.asarray(cs, F32))


def _expand_gk(cs, mb):
    eye = jnp.eye(mb, dtype=F32)
    g = jnp.einsum("rmbpn,pq->bmprqn", cs, eye)
    nblk = cs.shape[2]
    return g.reshape(nblk, FFT_L2 * mb, 2 * mb * FFT_L2).astype(BF16)


def kernel(x, c, ctx, c_ctx, ada_w, ada_b, norm1_g, norm2_g, mix_w_in, mix_w_out, attn_sink,
           shift_mu_prev, shift_mu_next, decay_w0, decay_w2, iclr_a0, iclr_a2, gate_g2, key_kk,
           key_ka, bonus_rk, lnx_g, lnx_b, fourier_w_out, mlp_w1, mlp_w2, final_g):
    b, t, d = x.shape
    nctx = ctx.shape[1]
    hd = key_kk.shape[1]
    q_dim = d - hd
    n_heads = q_dim // HEAD_DIM
    kv_dim = (n_heads // 4) * HEAD_DIM
    att_cols = q_dim + 2 * kv_dim

    cond = jnp.zeros((8, d), F32).at[:b].set(c).at[b].set(c_ctx)
    mods = _ada_call(cond, ada_w, ada_b)
    lat = [mods[i, :b].reshape(b, N_MOD, 1, d) for i in range(2)]
    cmod = [mods[i, b:b + 1].reshape(1, N_MOD, 1, d) for i in range(2)]
    lm = lambda i, k: lat[i][:, k]
    cm = lambda i, k: cmod[i][:, k]
    row1 = lambda a: a.reshape(1, -1)

    w_in = mix_w_in[0]
    wk = w_in[:, q_dim:q_dim + kv_dim].reshape(d, kv_dim // HEAD_DIM, 1, HEAD_DIM)
    wv = w_in[:, q_dim + kv_dim:att_cols].reshape(d, kv_dim // HEAD_DIM, 1, HEAD_DIM)
    dup = lambda w: jnp.broadcast_to(w, (d, kv_dim // HEAD_DIM, 2, HEAD_DIM)).reshape(d, 2 * kv_dim)
    w_att = jnp.concatenate([w_in[:, :q_dim], dup(wk), dup(wv)], axis=1).astype(BF16)
    w_rw = w_in[:, att_cols:].astype(BF16)
    cos_t, sin_t = _rope_tables(t)
    cos_c, sin_c = jnp.ones((nctx, LANES), F32), jnp.zeros((nctx, LANES), F32)
    g1 = row1(norm1_g[0])
    x2 = x.reshape(b * t, d)
    ctx2 = ctx.reshape(b * nctx, d)
    tm_in = min(512, t)
    q, kd, vd, zrw = _inproj_call(x2, g1, lm(0, 0), lm(0, 1), cos_t, sin_t, w_att, w_rw, t, tm_in)
    qc, kc, vc, zrwc = _inproj_call(ctx2, g1, cm(0, 0), cm(0, 1), cos_c, sin_c, w_att, w_rw,
                                    b * nctx, nctx)
    q, kd, vd = (a.reshape(b, t, -1) for a in (q, kd, vd))
    qc, kc, vc = (a.reshape(b, nctx, -1) for a in (qc, kc, vc))
    sinkb = jnp.broadcast_to(attn_sink[0][:, None], (n_heads, LANES)).astype(F32)
    att = _attn_call(q, kd, vd, kc, vc, sinkb, 3)
    att_c = _attn_call(qc, kc, vc, kc, vc, sinkb, 0)

    mu = jnp.stack([shift_mu_prev[0], shift_mu_next[0]])
    zl = jnp.zeros((DECAY_LORA, hd), F32)
    wa = jnp.stack([jnp.concatenate([jnp.concatenate([decay_w2[0, dd], zl], axis=1),
                                     jnp.concatenate([zl, iclr_a2[0, dd]], axis=1)], axis=0)
                    for dd in range(2)])
    w0a0 = jnp.concatenate([decay_w0[0], iclr_a0[0]], axis=1).reshape(2, 1, 2 * hd)
    seg = np.arange(hd) // RWKV_N
    bd = jnp.asarray(seg[:, None] == seg[None, :], F32).astype(BF16)
    k_k, k_a, r_k = row1(key_kk[0]), row1(key_ka[0]), row1(bonus_rk[0])
    g2w = gate_g2[0].astype(BF16)
    s_zero = jnp.zeros((b, 2, hd // LANES, LANES, LANES), F32)
    yfc, ybc, bonus_c, gate_c, s_ctx = _rwkv_call(zrwc.reshape(b, nctx, -1), s_zero, mu, wa, w0a0,
                                                  k_k, k_a, r_k, g2w, bd)
    yf, yb, bonus, gate, _ = _rwkv_call(zrw.reshape(b, t, -1), s_ctx, mu, wa, w0a0,
                                        k_k, k_a, r_k, g2w, bd)

    w_out = mix_w_out[0].astype(BF16)
    n2g = row1(norm2_g[0])
    flat = lambda a: a.reshape(-1, a.shape[-1])
    xm, h2 = _readout_call(flat(att), flat(yf), flat(yb), flat(bonus), flat(gate), x2,
                           row1(lnx_g[0]), row1(lnx_b[0]), bd, w_out, lm(0, 2), n2g,
                           lm(0, 3), lm(0, 4), t, tm_in)
    xmc, h2c = _readout_call(flat(att_c), flat(yfc), flat(ybc), flat(bonus_c), flat(gate_c), ctx2,
                             row1(lnx_g[0]), row1(lnx_b[0]), bd, w_out, cm(0, 2), n2g,
                             cm(0, 3), cm(0, 4), b * nctx, nctx)
    w1 = mlp_w1[0].astype(BF16)
    w2 = mlp_w2[0].astype(BF16)
    g1n = row1(norm1_g[1])
    tm_mlp = min(1024, t)
    x1, h1 = _mlp_call(h2, xm, w1, w2, lm(0, 5), g1n, lm(1, 0), lm(1, 1), t, tm_mlp, 512, False)
    ctx1, _ = _mlp_call(h2c, xmc, w1, w2, cm(0, 5), g1n, cm(1, 0), cm(1, 1), b * nctx, nctx, 512,
                        False)
    del ctx1

    l1 = t // FFT_L2
    nb = 16
    mb = min(8, l1)
    gw = d // FOURIER_GROUPS
    kr1, wc, cs = _fft_tables(t, gw, nb, mb)
    gk = _expand_gk(cs, mb)
    z4 = _fft1_call(h1.reshape(b, l1, FFT_L2, d), kr1, wc, nb)
    xm4, h24 = _fft2_call(z4, gk, x1.reshape(b, FFT_L2, l1, d), fourier_w_out[0].astype(BF16),
                          lm(1, 2), row1(norm2_g[1]), lm(1, 3), lm(1, 4), mb, 64)
    out, _ = _mlp_call(h24.reshape(b * t, d), xm4.reshape(b * t, d), mlp_w1[1].astype(BF16),
                       mlp_w2[1].astype(BF16), lm(1, 5), row1(final_g), lm(1, 0), lm(1, 1),
                       t, tm_mlp, 512, True)
    return out.reshape(b, t, d)
```

```python
import functools

import numpy as np
import jax
import jax.numpy as jnp
from jax import lax
from jax.experimental import pallas as pl
from jax.experimental.pallas import tpu as pltpu

F32 = jnp.float32
BF16 = jnp.bfloat16
HIGHEST = lax.Precision.HIGHEST

HEAD_DIM = 64
WINDOW = 128
QBLK = 128
GRID_W = 64
ROPE_BASE = 10000.0
RWKV_N = 64
DECAY_LORA = 64
ICLR_LORA = 64
GATE_LORA = 128
FOURIER_GROUPS = 4
N_MOD = 6
NORM_EPS = 1e-6
GN_EPS = 64e-5
NEG_INF = -1e30

CHUNK = 64
LANES = 128
FFT_L2 = 128
VMEM_LIMIT = 48 * 1024 * 1024


def _cparams(sem):
    return pltpu.CompilerParams(dimension_semantics=sem, vmem_limit_bytes=VMEM_LIMIT)


def _dot(a, b, **kw):
    return jnp.dot(a, b, preferred_element_type=F32, **kw)


def _dot_nt(a, b):
    return lax.dot_general(a, b, (((1,), (1,)), ((), ())), preferred_element_type=F32)


def _dot_tn(a, b):
    return lax.dot_general(a, b, (((0,), (0,)), ((), ())), preferred_element_type=F32)


def _split_dot(x, m_bf16, passes):
    acc = None
    rem = x
    for _ in range(passes):
        piece = rem.astype(BF16)
        term = _dot(piece, m_bf16)
        acc = term if acc is None else acc + term
        rem = rem - piece.astype(F32)
    return acc


def _rms_mod(x, g, sh, sc):
    ms = jnp.mean(x * x, axis=-1, keepdims=True)
    return (x * lax.rsqrt(ms + NORM_EPS)) * g * (1.0 + sc) + sh


def _ada_kernel(cond_ref, w_ref, b_ref, o_ref):
    s = cond_ref[...]
    s = s * jax.nn.sigmoid(s)
    o_ref[0] = _dot(s, w_ref[0], precision=HIGHEST) + b_ref[0]


def _ada_call(cond, ada_w, ada_b):
    depth, d, n = ada_w.shape
    tn = 1536
    return pl.pallas_call(
        _ada_kernel,
        grid=(depth, n // tn),
        in_specs=[pl.BlockSpec((8, d), lambda i, j: (0, 0)),
                  pl.BlockSpec((1, d, tn), lambda i, j: (i, 0, j)),
                  pl.BlockSpec((1, 1, tn), lambda i, j: (i, 0, j))],
        out_specs=pl.BlockSpec((1, 8, tn), lambda i, j: (i, 0, j)),
        out_shape=jax.ShapeDtypeStruct((depth, 8, n), F32),
        compiler_params=_cparams(("parallel", "parallel")),
    )(cond, ada_w, ada_b.reshape(depth, 1, n))


def _inproj_kernel(x_ref, g_ref, sh_ref, sc_ref, cos_ref, sin_ref, wa_ref, wr_ref,
                   q_ref, k_ref, v_ref, z_ref):
    h = _rms_mod(x_ref[...], g_ref[...], sh_ref[0], sc_ref[0]).astype(BF16)
    z_ref[...] = _dot(h, wr_ref[...])
    za = _dot(h, wa_ref[...])
    cos = cos_ref[...]
    sin = sin_ref[...]
    lane = lax.broadcasted_iota(jnp.int32, cos.shape, 1)
    first = (lane % 32) < 16
    nq = q_ref.shape[1] // LANES
    nk = k_ref.shape[1] // LANES
    for c in range(nq + nk):
        s = za[:, c * LANES:(c + 1) * LANES]
        partner = jnp.where(first, pltpu.roll(s, LANES - 16, 1), pltpu.roll(s, 16, 1))
        ro = s * cos + partner * sin
        if c < nq:
            q_ref[:, c * LANES:(c + 1) * LANES] = (ro * (HEAD_DIM ** -0.5)).astype(BF16)
        else:
            k_ref[:, (c - nq) * LANES:(c - nq + 1) * LANES] = ro.astype(BF16)
    v_ref[...] = za[:, (nq + nk) * LANES:].astype(BF16)


def _inproj_call(x2, g1, sh, sc, cos, sin, w_att, w_rw, rows_per_group, tm):
    r, d = x2.shape
    period = cos.shape[0]
    n_per = period // tm
    per_group = rows_per_group // tm
    na = w_att.shape[1]
    nr = w_rw.shape[1]
    nq, nkd = 512, 256
    row = lambda i: (i, 0)
    grp = lambda i: (i // per_group, 0, 0)
    return pl.pallas_call(
        _inproj_kernel,
        grid=(r // tm,),
        in_specs=[pl.BlockSpec((tm, d), row),
                  pl.BlockSpec((1, d), lambda i: (0, 0)),
                  pl.BlockSpec((1, 1, d), grp),
                  pl.BlockSpec((1, 1, d), grp),
                  pl.BlockSpec((tm, LANES), lambda i: (i % n_per, 0)),
                  pl.BlockSpec((tm, LANES), lambda i: (i % n_per, 0)),
                  pl.BlockSpec((d, na), lambda i: (0, 0)),
                  pl.BlockSpec((d, nr), lambda i: (0, 0))],
        out_specs=[pl.BlockSpec((tm, nq), row), pl.BlockSpec((tm, nkd), row),
                   pl.BlockSpec((tm, nkd), row), pl.BlockSpec((tm, nr), row)],
        out_shape=[jax.ShapeDtypeStruct((r, nq), BF16), jax.ShapeDtypeStruct((r, nkd), BF16),
                   jax.ShapeDtypeStruct((r, nkd), BF16), jax.ShapeDtypeStruct((r, nr), F32)],
        compiler_params=_cparams(("parallel",)),
    )(x2, g1, sh, sc, cos, sin, w_att, w_rw)


def _attn_kernel(*refs, n_loc, seq_len):
    q_ref = refs[0]
    k_refs = refs[1:1 + n_loc]
    v_refs = refs[1 + n_loc:1 + 2 * n_loc]
    kc_ref, vc_ref, sink_ref, o_ref = refs[1 + 2 * n_loc:]
    i = pl.program_id(1)
    nctx = kc_ref.shape[1]
    nkeys = n_loc * QBLK + nctx
    lane = lax.broadcasted_iota(jnp.int32, (QBLK, LANES), 1)
    low = lane < HEAD_DIM
    if n_loc:
        row = lax.broadcasted_iota(jnp.int32, (QBLK, nkeys), 0)
        col = lax.broadcasted_iota(jnp.int32, (QBLK, nkeys), 1)
        kpos = (i - 1) * QBLK + col
        qpos = i * QBLK + row
        valid = (col >= n_loc * QBLK) | ((jnp.abs(kpos - qpos) <= WINDOW) & (kpos >= 0) & (kpos < seq_len))
    zero = jnp.zeros((), BF16)
    n_heads = q_ref.shape[2] // HEAD_DIM
    vals = []
    scores = []
    for h in range(n_heads):
        hk, p, e = h // 4, h // 2, h % 2
        ksl = slice(hk * LANES, (hk + 1) * LANES)
        if h % 4 == 0:
            keys = jnp.concatenate([kr[0, :, ksl] for kr in k_refs] + [kc_ref[0, :, ksl]], axis=0)
            vals.append(jnp.concatenate([vr[0, :, ksl] for vr in v_refs] + [vc_ref[0, :, ksl]], axis=0))
        qp = q_ref[0, :, p * LANES:(p + 1) * LANES]
        scores.append(_dot_nt(jnp.where(low if e == 0 else ~low, qp, zero), keys))
    probs, dens = [], []
    for h in range(n_heads):
        s = scores[h]
        if n_loc:
            s = jnp.where(valid, s, NEG_INF)
        sk = sink_ref[h:h + 1, 0:1]
        m = jnp.maximum(jnp.max(s, axis=-1, keepdims=True), sk)
        pr = jnp.exp(s - m)
        dens.append(jnp.sum(pr, axis=-1, keepdims=True) + jnp.exp(sk - m))
        probs.append(pr.astype(BF16))
    outs = [_dot(probs[h], vals[h // 4]) / dens[h] for h in range(n_heads)]
    for p in range(n_heads // 2):
        o_ref[0, :, p * LANES:(p + 1) * LANES] = jnp.where(low, outs[2 * p], outs[2 * p + 1]).astype(BF16)


def _attn_call(q, kd, vd, kc, vc, sinkb, n_loc):
    b, t, nq = q.shape
    nb = t // QBLK
    nctx = kc.shape[1]
    kw = kd.shape[2]
    qspec = pl.BlockSpec((1, QBLK, nq), lambda bb, i: (bb, i, 0))
    loc = []
    for off in (-1, 0, 1)[:n_loc]:
        loc.append(pl.BlockSpec((1, QBLK, kw), functools.partial(
            lambda bb, i, off: (bb, jnp.clip(i + off, 0, nb - 1), 0), off=off)))
    cspec = pl.BlockSpec((1, nctx, kw), lambda bb, i: (bb, 0, 0))
    args = [q] + [kd] * n_loc + [vd] * n_loc + [kc, vc, sinkb]
    return pl.pallas_call(
        functools.partial(_attn_kernel, n_loc=n_loc, seq_len=t),
        grid=(b, nb),
        in_specs=[qspec] + loc + loc + [cspec, cspec, pl.BlockSpec(sinkb.shape, lambda bb, i: (0, 0))],
        out_specs=qspec,
        out_shape=jax.ShapeDtypeStruct((b, t, nq), BF16),
        compiler_params=_cparams(("parallel", "parallel")),
    )(*args)


def _rwkv_prep(z, prow, nrow, is_first, is_last, mu, wa, w0a0, k_k, k_a, bd, tri, keep, d):
    c = z.shape[0]
    hd = k_k.shape[1]
    rowi = lax.broadcasted_iota(jnp.int32, z.shape, 0)
    zp = jnp.where(rowi == 0, jnp.where(is_first, 0.0, prow), pltpu.roll(z, 1, 0))
    zn = jnp.where(rowi == c - 1, jnp.where(is_last, 0.0, nrow), pltpu.roll(z, c - 1, 0))
    zs = z + mu[0:1] * (zp - z) + mu[1:2] * (zn - z)
    r = zs[:, 0:hd]
    k = zs[:, hd:2 * hd]
    v = zs[:, 2 * hd:3 * hd]
    wa_in = zs[:, 3 * hd:3 * hd + LANES]
    gl = zs[:, 3 * hd + LANES:]
    lane = lax.broadcasted_iota(jnp.int32, (c, LANES), 1)
    low = lane < DECAY_LORA
    tw = jnp.where(low, jnp.tanh(wa_in), wa_in)
    tw_hi = tw.astype(BF16)
    tw_lo = (tw - tw_hi.astype(F32)).astype(BF16)
    xwa = _dot(tw_hi, wa[0]) + (_dot(tw_hi, wa[1]) + _dot(tw_lo, wa[0])) + w0a0
    xw = xwa[:, :hd]
    a = jax.nn.sigmoid(xwa[:, hd:])
    w_log = jnp.minimum(xw, 0.0) - jnp.log(1.0 + jnp.exp(-jnp.abs(xw))) - 0.5
    lw = -jnp.exp(w_log)
    kkr = k * k_k
    kk = kkr * lax.rsqrt(_split_dot(kkr * kkr, bd, 2) + 1e-12)
    kd = k * (1.0 + (a - 1.0) * k_a)
    bb = kk * a
    cum = _split_dot_left(tri, lw, 3)
    total = cum[0:1] if d else cum[c - 1:c]
    e_in = jnp.exp(cum)
    e_ex = jnp.exp(cum - lw)
    e_neg = jnp.exp(-cum)
    e_rem = jnp.exp(total - cum)
    pc = jnp.exp(total)
    kt = kk * e_ex
    rt = r * e_in
    bt = bb * e_neg
    kdt = kd * e_neg
    bp = bb * e_rem
    kp = kd * e_rem

    def stack(xp):
        zero = jnp.zeros_like(xp)
        return jnp.concatenate([jnp.where(low, xp, zero), jnp.where(low, zero, xp)], axis=0)

    chains = []
    for p in range(hd // LANES):
        sl = slice(p * LANES, (p + 1) * LANES)
        sb = lambda a: stack(a[:, sl].astype(BF16))
        chains.append(dict(
            d=d, p=p, keep=keep, pc=pc[:, sl], rts=stack(rt[:, sl]),
            kts=sb(kt), rtb=sb(rt), bts=sb(bt), kdts=sb(kdt), vs=sb(v), bps=sb(bp), kps=sb(kp)))
    return chains, r, k, v, gl


def _rwkv_solve(chains, eye, state_ref, c):
    c2 = 2 * c
    bf = lambda a: a.astype(BF16)
    for ch in chains:
        aa = _dot_nt(jnp.concatenate([ch["kts"], ch["rtb"]], axis=0),
                     jnp.concatenate([ch["bts"], ch["kdts"]], axis=0))
        aa = jnp.where(ch["keep"], aa, 0.0)
        ch["auk"] = aa[:c2, c2:]
        ch["arr"] = aa[c2:, :]
        x = -aa[:c2, :c2]
        ch["t"] = eye + x
        ch["xp"] = x
    for ch in chains:
        xb = bf(ch["xp"])
        ch["xp"] = _dot(xb, xb)
    for _ in range(4):
        for ch in chains:
            both = _dot(bf(ch["xp"]), bf(jnp.concatenate([ch["t"], ch["xp"]], axis=1)))
            ch["t"] = ch["t"] + both[:, :c2]
            ch["xp"] = both[:, c2:]
    for ch in chains:
        ch["t"] = ch["t"] + _dot(bf(ch["xp"]), bf(ch["t"]))
    for ch in chains:
        ch["av"] = _dot(bf(ch["auk"]), ch["vs"])
    for ch in chains:
        twm = _dot(bf(ch["t"]), jnp.concatenate([ch["kts"], bf(ch["av"])], axis=1))
        ch["w"] = twm[:, :LANES]
        ch["uloc"] = -twm[:, LANES:]
    for ch in chains:
        vs = ch["vs"]
        wb = bf(ch["w"])
        ch["wb"] = wb
        rhs5 = jnp.concatenate([jnp.concatenate([bf(ch["uloc"]), -wb], axis=1),
                                jnp.concatenate([vs, jnp.zeros_like(vs)], axis=1)], axis=0)
        o5 = _dot(bf(ch["arr"]), rhs5)
        ch["yloc"] = o5[:, :LANES]
        ch["rhat"] = ch["rts"] + o5[:, LANES:]
    for ch in chains:
        ch["s0"] = state_ref[0, ch["d"], ch["p"]]
        ch["ws"] = _dot_nt(jnp.concatenate([ch["wb"], bf(ch["rhat"])], axis=0), bf(ch["s0"]))
    ys = {}
    for ch in chains:
        ws = ch["ws"]
        u = ch["uloc"] - ws[:c2]
        y = ch["yloc"] + ws[c2:]
        snew = ch["s0"] * ch["pc"] + _dot_tn(jnp.concatenate([bf(u), ch["vs"]], axis=0),
                                             jnp.concatenate([ch["bps"], ch["kps"]], axis=0))
        state_ref[0, ch["d"], ch["p"]] = snew
        ys.setdefault(ch["d"], []).append(y[:c] + y[c:])
    return [jnp.concatenate(ys[d], axis=1) for d in sorted(ys)]


def _split_dot_left(m_bf16, x, passes):
    acc = None
    rem = x
    for _ in range(passes):
        piece = rem.astype(BF16)
        term = _dot(m_bf16, piece)
        acc = term if acc is None else acc + term
        rem = rem - piece.astype(F32)
    return acc


def _rwkv_kernel(zf_ref, zfp_ref, zfn_ref, zb_ref, zbp_ref, zbn_ref, s0_ref, mu_ref, wa_ref,
                 w0a0_ref, kk_ref, ka_ref, rk_ref, g2_ref, bd_ref,
                 yf_ref, yb_ref, bonus_ref, gate_ref, state_ref):
    j = pl.program_id(1)
    nc = pl.num_programs(1)
    c = zf_ref.shape[1]

    @pl.when(j == 0)
    def _():
        state_ref[...] = s0_ref[...]

    ri = lax.broadcasted_iota(jnp.int32, (c, c), 0)
    ci = lax.broadcasted_iota(jnp.int32, (c, c), 1)
    r4 = lax.broadcasted_iota(jnp.int32, (4 * c, 4 * c), 0)
    c4 = lax.broadcasted_iota(jnp.int32, (4 * c, 4 * c), 1)
    tt = r4 % c
    ss = c4 % c
    incl = jnp.where(r4 < 2 * c, 0, 1)
    r2 =lax.broadcasted_iota(jnp.int32, (2 * c, 2 * c), 0)
    c2 = lax.broadcasted_iota(jnp.int32, (2 * c, 2 * c), 1)
    eye = jnp.where(r2 == c2, 1.0, 0.0).astype(F32)
    mu = mu_ref[...]
    k_k = kk_ref[...]
    k_a = ka_ref[...]
    bd = bd_ref[...]

    tri_f = jnp.where(ci <= ri, 1.0, 0.0).astype(BF16)
    keep_f = ss < tt + incl
    ch_f, r, k, v, gl = _rwkv_prep(zf_ref[0], zfp_ref[0, 7:8], zfn_ref[0, 0:1], j == 0, j == nc - 1,
                                   mu, wa_ref[0], w0a0_ref[0], k_k, k_a, bd, tri_f, keep_f, 0)
    bonus_ref[0] = _split_dot(r * k * rk_ref[...], bd, 2) * v
    gate_ref[0] = _dot(jax.nn.sigmoid(gl).astype(BF16), g2_ref[...])

    tri_b = jnp.where(ci >= ri, 1.0, 0.0).astype(BF16)
    keep_b = ss > tt - incl
    ch_b, _, _, _, _ = _rwkv_prep(zb_ref[0], zbp_ref[0, 7:8], zbn_ref[0, 0:1], j == nc - 1, j == 0,
                                  mu, wa_ref[1], w0a0_ref[1], k_k, k_a, bd, tri_b, keep_b, 1)
    chains = [ch for pair in zip(ch_f, ch_b) for ch in pair]
    y_f, y_b = _rwkv_solve(chains, eye, state_ref, c)
    yf_ref[0] = y_f
    yb_ref[0] = y_b


def _rwkv_call(z, s0, mu, wa, w0a0, k_k, k_a, r_k, g2, bd):
    b, t, nz = z.shape
    c = CHUNK
    nc = t // c
    hd = k_k.shape[1]
    cb = c // 8
    nb8 = t // 8
    fwd = lambda bb, j: (bb, j, 0)
    bwd = lambda bb, j: (bb, nc - 1 - j, 0)
    const2 = lambda bb, j: (0, 0)
    const3 = lambda bb, j: (0, 0, 0)
    st = pl.BlockSpec((1,) + s0.shape[1:], lambda bb, j: (bb, 0, 0, 0, 0))
    ychunk = pl.BlockSpec((1, c, hd), fwd)
    return pl.pallas_call(
        _rwkv_kernel,
        grid=(b, nc),
        in_specs=[pl.BlockSpec((1, c, nz), fwd),
                  pl.BlockSpec((1, 8, nz), lambda bb, j: (bb, jnp.maximum(j * cb - 1, 0), 0)),
                  pl.BlockSpec((1, 8, nz), lambda bb, j: (bb, jnp.minimum((j + 1) * cb, nb8 - 1), 0)),
                  pl.BlockSpec((1, c, nz), bwd),
                  pl.BlockSpec((1, 8, nz), lambda bb, j: (bb, jnp.maximum((nc - 1 - j) * cb - 1, 0), 0)),
                  pl.BlockSpec((1, 8, nz), lambda bb, j: (bb, jnp.minimum((nc - j) * cb, nb8 - 1), 0)),
                  st,
                  pl.BlockSpec(mu.shape, const2),
                  pl.BlockSpec(wa.shape, lambda bb, j: (0, 0, 0, 0)),
                  pl.BlockSpec(w0a0.shape, const3),
                  pl.BlockSpec(k_k.shape, const2),
                  pl.BlockSpec(k_a.shape, const2),
                  pl.BlockSpec(r_k.shape, const2),
                  pl.BlockSpec(g2.shape, const2),
                  pl.BlockSpec(bd.shape, const2)],
        out_specs=[ychunk, pl.BlockSpec((1, c, hd), bwd), ychunk, ychunk, st],
        out_shape=[jax.ShapeDtypeStruct((b, t, hd), F32)] * 4 + [jax.ShapeDtypeStruct(s0.shape, F32)],
        compiler_params=_cparams(("parallel", "arbitrary")),
    )(z, z, z, z, z, z, s0, mu, wa, w0a0, k_k, k_a, r_k, g2, bd)


def _readout_kernel(att_ref, yf_ref, yb_ref, bonus_ref, gate_ref, x_ref, lg_ref, lb_ref, bd_ref,
                    wo_ref, gt_ref, g2_ref, sh_ref, sc_ref, xo_ref, h_ref):
    bd = bd_ref[...]
    inv_n = 1.0 / RWKV_N
    y = yf_ref[...] + yb_ref[...]
    mean = _split_dot(y, bd, 3) * inv_n
    yc = y - mean
    var = _split_dot(yc * yc, bd, 2) * inv_n
    yn = yc * lax.rsqrt(var + GN_EPS) * lg_ref[...] + lb_ref[...]
    rw = (yn + bonus_ref[...]) * gate_ref[...]
    cat = jnp.concatenate([att_ref[...], rw.astype(BF16)], axis=1)
    xm = x_ref[...] + gt_ref[0] * _dot(cat, wo_ref[...])
    xo_ref[...] = xm
    h_ref[...] = _rms_mod(xm, g2_ref[...], sh_ref[0], sc_ref[0]).astype(BF16)


def _readout_call(att, yf, yb, bonus, gate, x2, lnx_g, lnx_b, bd, w_out, gt1, g2, sh2, sc2,
                  rows_per_group, tm):
    r, d = x2.shape
    hd = yf.shape[1]
    per_group = rows_per_group // tm
    row = lambda i: (i, 0)
    c2 = lambda i: (0, 0)
    grp = lambda i: (i // per_group, 0, 0)
    half = pl.BlockSpec((tm, hd), row)
    full = pl.BlockSpec((tm, d), row)
    vec = pl.BlockSpec((1, 1, d), grp)
    return pl.pallas_call(
        _readout_kernel,
        grid=(r // tm,),
        in_specs=[half, half, half, half, half, full,
                  pl.BlockSpec((1, hd), c2), pl.BlockSpec((1, hd), c2), pl.BlockSpec(bd.shape, c2),
                  pl.BlockSpec(w_out.shape, c2), vec, pl.BlockSpec((1, d), c2), vec, vec],
        out_specs=[full, full],
        out_shape=[jax.ShapeDtypeStruct((r, d), F32), jax.ShapeDtypeStruct((r, d), BF16)],
        compiler_params=_cparams(("parallel",)),
    )(att, yf, yb, bonus, gate, x2, lnx_g, lnx_b, bd, w_out, gt1, g2, sh2, sc2)


def _mlp_kernel(h_ref, x_ref, w1_ref, w2_ref, gt_ref, gn_ref, shn_ref, scn_ref, xo_ref, ho_ref,
                acc_ref, *, final):
    j = pl.program_id(1)

    @pl.when(j == 0)
    def _():
        acc_ref[...] = jnp.zeros_like(acc_ref)

    a = jnp.maximum(_dot(h_ref[...], w1_ref[...]), 0.0)
    acc_ref[...] += _dot((a * a).astype(BF16), w2_ref[...])

    @pl.when(j == pl.num_programs(1) - 1)
    def _():
        xo = x_ref[...] + gt_ref[0] * acc_ref[...]
        if final:
            ms = jnp.mean(xo * xo, axis=-1, keepdims=True)
            xo_ref[...] = xo * lax.rsqrt(ms + NORM_EPS) * gn_ref[...]
            ho_ref[...] = jnp.zeros_like(ho_ref)
        else:
            xo_ref[...] = xo
            ho_ref[...] = _rms_mod(xo, gn_ref[...], shn_ref[0], scn_ref[0]).astype(BF16)


def _mlp_call(h, x2, w1, w2, gt2, g_next, sh_next, sc_next, rows_per_group, tm, tf, final):
    r, d = x2.shape
    ff = w1.shape[1]
    per_group = rows_per_group // tm
    row = lambda i, j: (i, 0)
    grp = lambda i, j: (i // per_group, 0, 0)
    vec = pl.BlockSpec((1, 1, d), grp)
    return pl.pallas_call(
        functools.partial(_mlp_kernel, final=final),
        grid=(r // tm, ff // tf),
        in_specs=[pl.BlockSpec((tm, d), row), pl.BlockSpec((tm, d), row),
                  pl.BlockSpec((d, tf), lambda i, j: (0, j)), pl.BlockSpec((tf, d), lambda i, j: (j, 0)),
                  vec, pl.BlockSpec((1, d), lambda i, j: (0, 0)), vec, vec],
        out_specs=[pl.BlockSpec((tm, d), row), pl.BlockSpec((tm, d), row)],
        out_shape=[jax.ShapeDtypeStruct((r, d), F32), jax.ShapeDtypeStruct((r, d), BF16)],
        scratch_shapes=[pltpu.VMEM((tm, d), F32)],
        compiler_params=_cparams(("parallel", "arbitrary")),
    )(h, x2, w1, w2, gt2, g_next, sh_next, sc_next)


def _fft1_kernel(h_ref, kr_ref, wc_ref, z_ref):
    l1, nb, d = h_ref.shape[1:]
    rows = l1 * nb
    hf = h_ref[0].reshape(rows, d)
    p = _dot(kr_ref[...], hf)
    gw = d // FOURIER_GROUPS
    wc = wc_ref[...]
    zr, zi = [], []
    for g in range(FOURIER_GROUPS):
        ap = jnp.concatenate([p[:rows, g * gw:(g + 1) * gw], p[rows:, g * gw:(g + 1) * gw]],
                             axis=1).astype(BF16)
        zz = _dot(ap, wc)
        zr.append(zz[:, :gw])
        zi.append(zz[:, gw:])
    z = jnp.concatenate(zr + zi, axis=1).astype(BF16)
    z_ref[0] = z.reshape(l1, nb, 2 * d)


def _fft1_call(h4, kr1, wc, nb):
    b, l1, l2, d = h4.shape
    return pl.pallas_call(
        _fft1_kernel,
        grid=(b, l2 // nb),
        in_specs=[pl.BlockSpec((1, l1, nb, d), lambda bb, j: (bb, 0, j, 0)),
                  pl.BlockSpec(kr1.shape, lambda bb, j: (0, 0)),
                  pl.BlockSpec(wc.shape, lambda bb, j: (0, 0))],
        out_specs=pl.BlockSpec((1, l1, nb, 2 * d), lambda bb, j: (bb, 0, j, 0)),
        out_shape=jax.ShapeDtypeStruct((b, l1, l2, 2 * d), BF16),
        compiler_params=_cparams(("parallel", "parallel")),
    )(h4, kr1, wc)


def _fft2_kernel(z_ref, gk_ref, x_ref, wo_ref, gt_ref, g2_ref, sh_ref, sc_ref, xo_ref, h_ref):
    mb, l2, d2 = z_ref.shape[1:]
    m2b = x_ref.shape[1]
    d = d2 // 2
    z = z_ref[0].reshape(mb * l2, d2)
    rhs = jnp.concatenate([z[:, :d], z[:, d:]], axis=0)
    f = _dot(gk_ref[0], rhs)
    y = _dot(f.astype(BF16), wo_ref[...])
    xm = x_ref[0].reshape(m2b * mb, d) + gt_ref[0] * y
    xo_ref[0] = xm.reshape(m2b, mb, d)
    h_ref[0] = _rms_mod(xm, g2_ref[...], sh_ref[0], sc_ref[0]).astype(BF16).reshape(m2b, mb, d)


def _fft2_call(z4, gk, x4, w_out, gt1, g2, sh2, sc2, mb, m2b):
    b, l1, l2, d2 = z4.shape
    d = d2 // 2
    nblk = l1 // mb
    xspec = pl.BlockSpec((1, m2b, mb, d), lambda m, bb, h: (bb, h, m, 0))
    vec = pl.BlockSpec((1, 1, d), lambda m, bb, h: (bb, 0, 0))
    return pl.pallas_call(
        _fft2_kernel,
        grid=(nblk, b, l2 // m2b),
        in_specs=[pl.BlockSpec((1, mb, l2, d2), lambda m, bb, h: (bb, m, 0, 0)),
                  pl.BlockSpec((1, m2b * mb, gk.shape[2]), lambda m, bb, h: (m, h, 0)),
                  xspec,
                  pl.BlockSpec(w_out.shape, lambda m, bb, h: (0, 0)),
                  vec, pl.BlockSpec((1, d), lambda m, bb, h: (0, 0)), vec, vec],
        out_specs=[xspec, xspec],
        out_shape=[jax.ShapeDtypeStruct((b, l2, l1, d), F32), jax.ShapeDtypeStruct((b, l2, l1, d), BF16)],
        compiler_params=_cparams(("parallel", "parallel", "parallel")),
    )(z4, gk, x4, w_out, gt1, g2, sh2, sc2)


def _rope_tables(t):
    axis_dim = HEAD_DIM // 2
    rows = t // GRID_W
    row = jnp.broadcast_to(jnp.arange(rows, dtype=F32)[:, None], (rows, GRID_W)).reshape(t)
    col = jnp.broadcast_to(jnp.arange(GRID_W, dtype=F32)[None, :], (rows, GRID_W)).reshape(t)
    inv = ROPE_BASE ** (-jnp.arange(0, axis_dim, 2, dtype=F32) / axis_dim)
    ang_r, ang_c = row[:, None] * inv, col[:, None] * inv
    cos = jnp.concatenate([jnp.cos(ang_r), jnp.cos(ang_r), jnp.cos(ang_c), jnp.cos(ang_c)], axis=1)
    sin = jnp.concatenate([-jnp.sin(ang_r), jnp.sin(ang_r), -jnp.sin(ang_c), jnp.sin(ang_c)], axis=1)
    return jnp.tile(cos, (1, 2)), jnp.tile(sin, (1, 2))


def _fft_tables(t, gw, nb, mb):
    l1 = t // FFT_L2
    n1 = np.arange(l1)
    ang1 = 2.0 * np.pi * np.outer(n1, n1) / l1
    eye = np.eye(nb)
    kr1 = np.concatenate([np.kron(np.cos(ang1), eye), np.kron(np.sin(ang1), eye)], axis=0)
    ch = np.arange(gw)
    angc = 2.0 * np.pi * np.outer(ch, ch) / gw
    cg, sg = np.cos(angc), np.sin(angc)
    wc = np.block([[cg, -sg], [-sg, -cg]])
    scale = 1.0 / np.sqrt(float(t) * gw)
    m = np.arange(t)
    n2 = np.arange(FFT_L2)
    theta = 2.0 * np.pi * np.outer(m, n2) / t
    cs = np.stack([np.cos(theta), np.sin(theta)], axis=0) * scale
    cs = cs.reshape(2, FFT_L2, l1 // mb, mb, FFT_L2)
    return (jnp.asarray(kr1, F32).astype(BF16), jnp.asarray(wc, F32).astype(BF16),
            jnp.asarray(cs, F32))


def _expand_gk(cs, mb):
    eye = jnp.eye(mb, dtype=F32)
    g = jnp.einsum("rmbpn,pq->bmprqn", cs, eye)
    nblk = cs.shape[2]
    return g.reshape(nblk, FFT_L2 * mb, 2 * mb * FFT_L2).astype(BF16)


def kernel(x, c, ctx, c_ctx, ada_w, ada_b, norm1_g, norm2_g, mix_w_in, mix_w_out, attn_sink,
           shift_mu_prev, shift_mu_next, decay_w0, decay_w2, iclr_a0, iclr_a2, gate_g2, key_kk,
           key_ka, bonus_rk, lnx_g, lnx_b, fourier_w_out, mlp_w1, mlp_w2, final_g):
    b, t, d = x.shape
    nctx = ctx.shape[1]
    hd = key_kk.shape[1]
    q_dim = d - hd
    n_heads = q_dim // HEAD_DIM
    kv_dim = (n_heads // 4) * HEAD_DIM
    att_cols = q_dim + 2 * kv_dim

    cond = jnp.zeros((8, d), F32).at[:b].set(c).at[b].set(c_ctx)
    mods = _ada_call(cond, ada_w, ada_b)
    lat = [mods[i, :b].reshape(b, N_MOD, 1, d) for i in range(2)]
    cmod = [mods[i, b:b + 1].reshape(1, N_MOD, 1, d) for i in range(2)]
    lm = lambda i, k: lat[i][:, k]
    cm = lambda i, k: cmod[i][:, k]
    row1 = lambda a: a.reshape(1, -1)

    w_in = mix_w_in[0]
    wk = w_in[:, q_dim:q_dim + kv_dim].reshape(d, kv_dim // HEAD_DIM, 1, HEAD_DIM)
    wv = w_in[:, q_dim + kv_dim:att_cols].reshape(d, kv_dim // HEAD_DIM, 1, HEAD_DIM)
    dup = lambda w: jnp.broadcast_to(w, (d, kv_dim // HEAD_DIM, 2, HEAD_DIM)).reshape(d, 2 * kv_dim)
    w_att = jnp.concatenate([w_in[:, :q_dim], dup(wk), dup(wv)], axis=1).astype(BF16)
    w_rw = w_in[:, att_cols:].astype(BF16)
    cos_t, sin_t = _rope_tables(t)
    cos_c, sin_c = jnp.ones((nctx, LANES), F32), jnp.zeros((nctx, LANES), F32)
    g1 = row1(norm1_g[0])
    x2 = x.reshape(b * t, d)
    ctx2 = ctx.reshape(b * nctx, d)
    tm_in = min(512, t)
    q, kd, vd, zrw = _inproj_call(x2, g1, lm(0, 0), lm(0, 1), cos_t, sin_t, w_att, w_rw, t, tm_in)
    qc, kc, vc, zrwc = _inproj_call(ctx2, g1, cm(0, 0), cm(0, 1), cos_c, sin_c, w_att, w_rw,
                                    b * nctx, nctx)
    q, kd, vd = (a.reshape(b, t, -1) for a in (q, kd, vd))
    qc, kc, vc = (a.reshape(b, nctx, -1) for a in (qc, kc, vc))
    sinkb = jnp.broadcast_to(attn_sink[0][:, None], (n_heads, LANES)).astype(F32)
    att = _attn_call(q, kd, vd, kc, vc, sinkb, 3)
    att_c = _attn_call(qc, kc, vc, kc, vc, sinkb, 0)

    mu = jnp.stack([shift_mu_prev[0], shift_mu_next[0]])
    zl = jnp.zeros((DECAY_LORA, hd), F32)
    wa = jnp.stack([jnp.concatenate([jnp.concatenate([decay_w2[0, dd], zl], axis=1),
                                     jnp.concatenate([zl, iclr_a2[0, dd]], axis=1)], axis=0)
                    for dd in range(2)])
    wa_hi = wa.astype(BF16)
    wa = jnp.stack([wa_hi, (wa - wa_hi.astype(F32)).astype(BF16)], axis=1)
    w0a0 = jnp.concatenate([decay_w0[0], iclr_a0[0]], axis=1).reshape(2, 1, 2 * hd)
    seg = np.arange(hd) // RWKV_N
    bd = jnp.asarray(seg[:, None] == seg[None, :], F32).astype(BF16)
    k_k, k_a, r_k = row1(key_kk[0]), row1(key_ka[0]), row1(bonus_rk[0])
    g2w = gate_g2[0].astype(BF16)
    s_zero = jnp.zeros((b, 2, hd // LANES, LANES, LANES), F32)
    yfc, ybc, bonus_c, gate_c, s_ctx = _rwkv_call(zrwc.reshape(b, nctx, -1), s_zero, mu, wa, w0a0,
                                                  k_k, k_a, r_k, g2w, bd)
    yf, yb, bonus, gate, _ = _rwkv_call(zrw.reshape(b, t, -1), s_ctx, mu, wa, w0a0,
                                        k_k, k_a, r_k, g2w, bd)

    w_out = mix_w_out[0].astype(BF16)
    n2g = row1(norm2_g[0])
    flat = lambda a: a.reshape(-1, a.shape[-1])
    xm, h2 = _readout_call(flat(att), flat(yf), flat(yb), flat(bonus), flat(gate), x2,
                           row1(lnx_g[0]), row1(lnx_b[0]), bd, w_out, lm(0, 2), n2g,
                           lm(0, 3), lm(0, 4), t, tm_in)
    xmc, h2c = _readout_call(flat(att_c), flat(yfc), flat(ybc), flat(bonus_c), flat(gate_c), ctx2,
                             row1(lnx_g[0]), row1(lnx_b[0]), bd, w_out, cm(0, 2), n2g,
                             cm(0, 3), cm(0, 4), b * nctx, nctx)
    w1 = mlp_w1[0].astype(BF16)
    w2 = mlp_w2[0].astype(BF16)
    g1n = row1(norm1_g[1])
    tm_mlp = min(1024, t)
    x1, h1 = _mlp_call(h2, xm, w1, w2, lm(0, 5), g1n, lm(1, 0), lm(1, 1), t, tm_mlp, 512, False)
    ctx1, _ = _mlp_call(h2c, xmc, w1, w2, cm(0, 5), g1n, cm(1, 0), cm(1, 1), b * nctx, nctx, 512,
                        False)
    del ctx1

    l1 = t // FFT_L2
    nb = 16
    mb = min(8, l1)
    gw = d // FOURIER_GROUPS
    kr1, wc, cs = _fft_tables(t, gw, nb, mb)
    gk = _expand_gk(cs, mb)
    z4 = _fft1_call(h1.reshape(b, l1, FFT_L2, d), kr1, wc, nb)
    xm4, h24 = _fft2_call(z4, gk, x1.reshape(b, FFT_L2, l1, d), fourier_w_out[0].astype(BF16),
                          lm(1, 2), row1(norm2_g[1]), lm(1, 3), lm(1, 4), mb, 64)
    out, _ = _mlp_call(h24.reshape(b * t, d), xm4.reshape(b * t, d), mlp_w1[1].astype(BF16),
                       mlp_w2[1].astype(BF16), lm(1, 5), row1(final_g), lm(1, 0), lm(1, 1),
                       t, tm_mlp, 512, True)
    return out.reshape(b, t, d)
```

```python
import functools

import numpy as np
import jax
import jax.numpy as jnp
from jax import lax
from jax.experimental import pallas as pl
from jax.experimental.pallas import tpu as pltpu

F32 = jnp.float32
BF16 = jnp.bfloat16
HIGHEST = lax.Precision.HIGHEST

HEAD_DIM = 64
WINDOW = 128
QBLK = 128
GRID_W = 64
ROPE_BASE = 10000.0
RWKV_N = 64
DECAY_LORA = 64
ICLR_LORA = 64
GATE_LORA = 128
FOURIER_GROUPS = 4
N_MOD = 6
NORM_EPS = 1e-6
GN_EPS = 64e-5
NEG_INF = -1e30

CHUNK = 64
RWKV_BLOCK_CHUNKS = 2
LANES = 128
FFT_L2 = 128
VMEM_LIMIT = 48 * 1024 * 1024
LOG2E = 1.4426950408889634
Q_SCALE = HEAD_DIM ** -0.5 * LOG2E


def _cparams(sem):
    return pltpu.CompilerParams(dimension_semantics=sem, vmem_limit_bytes=VMEM_LIMIT)


def _dot(a, b, **kw):
    return jnp.dot(a, b, preferred_element_type=F32, **kw)


def _dot_nt(a, b):
    return lax.dot_general(a, b, (((1,), (1,)), ((), ())), preferred_element_type=F32)


def _dot_tn(a, b):
    return lax.dot_general(a, b, (((0,), (0,)), ((), ())), preferred_element_type=F32)


def _split_dot(x, m_bf16, passes):
    acc = None
    rem = x
    for _ in range(passes):
        piece = rem.astype(BF16)
        term = _dot(piece, m_bf16)
        acc = term if acc is None else acc + term
        rem = rem - piece.astype(F32)
    return acc


def _rms_mod(x, g, sh, sc):
    ms = jnp.mean(x * x, axis=-1, keepdims=True)
    return (x * lax.rsqrt(ms + NORM_EPS)) * g * (1.0 + sc) + sh


def _ada_kernel(cond_ref, w_ref, b_ref, o_ref):
    s = cond_ref[...]
    s = s * jax.nn.sigmoid(s)
    o_ref[0] = _dot(s, w_ref[0], precision=HIGHEST) + b_ref[0]


def _ada_call(cond, ada_w, ada_b):
    depth, d, n = ada_w.shape
    tn = 1536
    return pl.pallas_call(
        _ada_kernel,
        grid=(depth, n // tn),
        in_specs=[pl.BlockSpec((8, d), lambda i, j: (0, 0)),
                  pl.BlockSpec((1, d, tn), lambda i, j: (i, 0, j)),
                  pl.BlockSpec((1, 1, tn), lambda i, j: (i, 0, j))],
        out_specs=pl.BlockSpec((1, 8, tn), lambda i, j: (i, 0, j)),
        out_shape=jax.ShapeDtypeStruct((depth, 8, n), F32),
        compiler_params=_cparams(("parallel", "parallel")),
    )(cond, ada_w, ada_b.reshape(depth, 1, n))


def _inproj_kernel(x_ref, g_ref, sh_ref, sc_ref, cos_ref, sin_ref, wa_ref, wr_ref,
                   q_ref, k_ref, v_ref, z_ref):
    h = _rms_mod(x_ref[...], g_ref[...], sh_ref[0], sc_ref[0]).astype(BF16)
    z_ref[...] = _dot(h, wr_ref[...])
    za = _dot(h, wa_ref[...])
    cos = cos_ref[...]
    sin = sin_ref[...]
    lane = lax.broadcasted_iota(jnp.int32, cos.shape, 1)
    first = (lane % 32) < 16
    nq = q_ref.shape[1] // LANES
    nk = k_ref.shape[1] // LANES
    for c in range(nq + nk):
        s = za[:, c * LANES:(c + 1) * LANES]
        partner = jnp.where(first, pltpu.roll(s, LANES - 16, 1), pltpu.roll(s, 16, 1))
        ro = s * cos + partner * sin
        if c < nq:
            q_ref[:, c * LANES:(c + 1) * LANES] = (ro * Q_SCALE).astype(BF16)
        else:
            k_ref[:, (c - nq) * LANES:(c - nq + 1) * LANES] = ro.astype(BF16)
    v_ref[...] = za[:, (nq + nk) * LANES:].astype(BF16)


def _inproj_call(x2, g1, sh, sc, cos, sin, w_att, w_rw, rows_per_group, tm):
    r, d = x2.shape
    period = cos.shape[0]
    n_per = period // tm
    per_group = rows_per_group // tm
    na = w_att.shape[1]
    nr = w_rw.shape[1]
    nq, nkd = 512, 256
    row = lambda i: (i, 0)
    grp = lambda i: (i // per_group, 0, 0)
    return pl.pallas_call(
        _inproj_kernel,
        grid=(r // tm,),
        in_specs=[pl.BlockSpec((tm, d), row),
                  pl.BlockSpec((1, d), lambda i: (0, 0)),
                  pl.BlockSpec((1, 1, d), grp),
                  pl.BlockSpec((1, 1, d), grp),
                  pl.BlockSpec((tm, LANES), lambda i: (i % n_per, 0)),
                  pl.BlockSpec((tm, LANES), lambda i: (i % n_per, 0)),
                  pl.BlockSpec((d, na), lambda i: (0, 0)),
                  pl.BlockSpec((d, nr), lambda i: (0, 0))],
        out_specs=[pl.BlockSpec((tm, nq), row), pl.BlockSpec((tm, nkd), row),
                   pl.BlockSpec((tm, nkd), row), pl.BlockSpec((tm, nr), row)],
        out_shape=[jax.ShapeDtypeStruct((r, nq), BF16), jax.ShapeDtypeStruct((r, nkd), BF16),
                   jax.ShapeDtypeStruct((r, nkd), BF16), jax.ShapeDtypeStruct((r, nr), F32)],
        compiler_params=_cparams(("parallel",)),
    )(x2, g1, sh, sc, cos, sin, w_att, w_rw)


def _attn_kernel(*refs, n_loc, seq_len):
    q_ref = refs[0]
    k_refs = refs[1:1 + n_loc]
    v_refs = refs[1 + n_loc:1 + 2 * n_loc]
    kc_ref, vc_ref, sink_ref, o_ref = refs[1 + 2 * n_loc:]
    i = pl.program_id(1)
    nb = pl.num_programs(1)
    nctx = kc_ref.shape[1]
    nkeys = n_loc * QBLK + nctx
    gq = 4
    rows = gq * QBLK
    low = lax.broadcasted_iota(jnp.int32, (QBLK, LANES), 1) < HEAD_DIM
    if n_loc:
        rq = lax.broadcasted_iota(jnp.int32, (rows, QBLK), 0) % QBLK
        ck = lax.broadcasted_iota(jnp.int32, (rows, QBLK), 1)
        mask_prev = ck >= rq + jnp.where(i >= 1, 0, QBLK)
        mask_next = ck <= rq - jnp.where(i <= nb - 2, 0, QBLK)
    zero = jnp.zeros((QBLK, LANES), BF16)
    ones = jnp.ones((nkeys, LANES), BF16)
    n_groups = q_ref.shape[2] // (gq * HEAD_DIM)
    scores, vals, sinks = [], [], []
    for g in range(n_groups):
        ksl = slice(g * LANES, (g + 1) * LANES)
        keys = jnp.concatenate([kr[0, :, ksl] for kr in k_refs] + [kc_ref[0, :, ksl]], axis=0)
        vals.append(jnp.concatenate(
            [jnp.concatenate([vr[0, :, ksl] for vr in v_refs] + [vc_ref[0, :, ksl]], axis=0), ones], axis=1))
        qs = []
        for pp in range(2):
            qp = q_ref[0, :, (2 * g + pp) * LANES:(2 * g + pp + 1) * LANES]
            qs += [jnp.where(low, qp, zero), jnp.where(low, zero, qp)]
        scores.append(_dot_nt(jnp.concatenate(qs, axis=0), keys))
        sinks.append(jnp.concatenate(
            [jnp.broadcast_to(sink_ref[gq * g + h:gq * g + h + 1, 0:1], (QBLK, 1)) for h in range(gq)], axis=0))
    probs, ms = [], []
    for g in range(n_groups):
        s = scores[g]
        if n_loc:
            s = jnp.concatenate([jnp.where(mask_prev, s[:, :QBLK], NEG_INF), s[:, QBLK:2 * QBLK],
                                 jnp.where(mask_next, s[:, 2 * QBLK:3 * QBLK], NEG_INF), s[:, 3 * QBLK:]],
                                axis=1)
        m = jnp.maximum(jnp.max(s, axis=-1, keepdims=True), sinks[g])
        ms.append(m)
        probs.append(jnp.exp2(s - m).astype(BF16))
    for g in range(n_groups):
        o = _dot(probs[g], vals[g])
        out = o[:, :LANES] / (o[:, LANES:] + jnp.exp2(sinks[g] - ms[g]))
        for pp in range(2):
            even = out[2 * pp * QBLK:(2 * pp + 1) * QBLK]
            odd = out[(2 * pp + 1) * QBLK:(2 * pp + 2) * QBLK]
            p = 2 * g + pp
            o_ref[0, :, p * LANES:(p + 1) * LANES] = jnp.where(low, even, odd).astype(BF16)


def _attn_call(q, kd, vd, kc, vc, sinkb, n_loc):
    b, t, nq = q.shape
    nb = t // QBLK
    nctx = kc.shape[1]
    kw = kd.shape[2]
    qspec = pl.BlockSpec((1, QBLK, nq), lambda bb, i: (bb, i, 0))
    loc = []
    for off in (-1, 0, 1)[:n_loc]:
        loc.append(pl.BlockSpec((1, QBLK, kw), functools.partial(
            lambda bb, i, off: (bb, jnp.clip(i + off, 0, nb - 1), 0), off=off)))
    cspec = pl.BlockSpec((1, nctx, kw), lambda bb, i: (bb, 0, 0))
    args = [q] + [kd] * n_loc + [vd] * n_loc + [kc, vc, sinkb]
    return pl.pallas_call(
        functools.partial(_attn_kernel, n_loc=n_loc, seq_len=t),
        grid=(b, nb),
        in_specs=[qspec] + loc + loc + [cspec, cspec, pl.BlockSpec(sinkb.shape, lambda bb, i: (0, 0))],
        out_specs=qspec,
        out_shape=jax.ShapeDtypeStruct((b, t, nq), BF16),
        compiler_params=_cparams(("parallel", "parallel")),
    )(*args)


def _rwkv_prep(z, prow, nrow, is_first, is_last, mu, wa, w0a0, k_k, k_a, bd, tri, keep, d):
    nrows = z.shape[0]
    c = CHUNK
    hd = k_k.shape[1]
    row8 = lax.broadcasted_iota(jnp.int32, (8, z.shape[1]), 0)
    zp = pltpu.roll(z, 1, 0)
    zp = jnp.concatenate([jnp.where(row8 == 0, jnp.where(is_first, 0.0, prow), zp[:8]), zp[8:]], axis=0)
    zn = pltpu.roll(z, nrows - 1, 0)
    zn = jnp.concatenate([zn[:nrows - 8],
                          jnp.where(row8 == 7, jnp.where(is_last, 0.0, nrow), zn[nrows - 8:])], axis=0)
    zs = mu[2:3] * z + mu[0:1] * zp + mu[1:2] * zn
    r = zs[:, 0:hd]
    k = zs[:, hd:2 * hd]
    v = zs[:, 2 * hd:3 * hd]
    wa_in = zs[:, 3 * hd:3 * hd + LANES]
    gl = zs[:, 3 * hd + LANES:]
    low_r = lax.broadcasted_iota(jnp.int32, (nrows, LANES), 1) < DECAY_LORA
    low = lax.broadcasted_iota(jnp.int32, (c, LANES), 1) < RWKV_N
    tw = jnp.where(low_r, jnp.tanh(wa_in), wa_in)
    tw_hi = tw.astype(BF16)
    tw_lo = (tw - tw_hi.astype(F32)).astype(BF16)
    xwa = _dot(tw_hi, wa[0]) + (_dot(tw_hi, wa[1]) + _dot(tw_lo, wa[0])) + w0a0
    xw = xwa[:, :hd]
    a = jax.nn.sigmoid(xwa[:, hd:])
    w_log = jnp.minimum(xw, 0.0) - jnp.log(1.0 + jnp.exp(-jnp.abs(xw))) - 0.5
    lw = -jnp.exp(w_log)
    kkr = k * k_k
    kk = kkr * lax.rsqrt(_split_dot(kkr * kkr, bd, 2) + 1e-12)
    kd = k * (1.0 + (a - 1.0) * k_a)
    bb = kk * a
    cum = _split_dot_left(tri, lw, 3)
    nsub = nrows // c
    totals = [cum[s * c:s * c + 1] if d else cum[s * c + c - 1:s * c + c] for s in range(nsub)]
    total = jnp.concatenate([jnp.broadcast_to(tl, (c, hd)) for tl in totals], axis=0)
    e_in = jnp.exp(cum)
    e_ex = jnp.exp(cum - lw)
    e_neg = jnp.exp(-cum)
    e_rem = jnp.exp(total - cum)
    kt = kk * e_ex
    rt = r * e_in
    bt = bb * e_neg
    kdt = kd * e_neg
    bp = bb * e_rem
    kp = kd * e_rem

    def stack(xp):
        zero = jnp.zeros_like(xp)
        return jnp.concatenate([jnp.where(low, xp, zero), jnp.where(low, zero, xp)], axis=0)

    chains = []
    for s in range(nsub):
        pc = jnp.exp(totals[s])
        for p in range(hd // LANES):
            sl = slice(p * LANES, (p + 1) * LANES)
            sb = lambda a: stack(a[s * c:(s + 1) * c, sl].astype(BF16))
            chains.append(dict(
                d=d, p=p, s=s, keep=keep, pc=pc[:, sl],
                kts=sb(kt), rtb=sb(rt), bts=sb(bt), kdts=sb(kdt), vs=sb(v), bps=sb(bp), kps=sb(kp)))
    return chains, r, k, v, gl


def _rwkv_solve(chains, ioff, state_ref, c):
    c2 = 2 * c
    bf = lambda a: a.astype(BF16)
    rr = lax.broadcasted_iota(jnp.int32, (c2, 2 * c2), 0)
    cc = lax.broadcasted_iota(jnp.int32, (c2, 2 * c2), 1)
    diag = (rr // c) == ((cc // c) % 2)
    zb = jnp.zeros((c2, c2), BF16)

    def blockdiag(ab):
        n = ab.shape[1] // 2
        z = jnp.zeros((ab.shape[0], n), ab.dtype)
        return jnp.concatenate([jnp.concatenate([ab[:, :n], z], axis=1),
                                jnp.concatenate([z, ab[:, n:]], axis=1)], axis=0)

    def swap_halves(g):
        return jnp.concatenate([g[c:], g[:c]], axis=0)

    by_key = {(ch["d"], ch["s"], ch["p"]): ch for ch in chains}
    supers = [(by_key[(d, s, p)], by_key[(d, s, p + 1)])
              for (d, s, p) in sorted(by_key) if p % 2 == 0]
    sup = [dict(c0=a, c1=b, d=a["d"], s=a["s"], q=a["p"] // 2) for a, b in supers]
    for ch in chains:
        ch["kr"] = jnp.concatenate([ch["kts"], ch["rtb"]], axis=0)
        aa = _dot_nt(ch["kr"], jnp.concatenate([ch["bts"], ch["kdts"]], axis=0))
        aa = aa * ch["keep"]
        ch["auk"] = aa[:c2, c2:]
        ch["arr"] = bf(aa[c2:, :])
        ch["b0"] = ioff - aa[:c2, :c2]
    for sc in sup:
        sc["b"] = jnp.concatenate([sc["c0"]["b0"], sc["c1"]["b0"]], axis=1)
        sc["kr"] = jnp.concatenate([sc["c0"]["kr"], sc["c1"]["kr"]], axis=1)
    for _ in range(6):
        for sc in sup:
            b = sc["b"]
            xbd = jnp.where(diag, b, 0.0)
            sc["b"] = _dot(bf(xbd), blockdiag(bf(b))) + (b - xbd)
    for sc in sup:
        auk = jnp.concatenate([sc["c0"]["auk"], sc["c1"]["auk"]], axis=1)
        vs2 = jnp.concatenate([sc["c0"]["vs"], sc["c1"]["vs"]], axis=1)
        sc["av"] = _dot(bf(auk), blockdiag(vs2))
    for sc in sup:
        sc["tsw"] = bf(jnp.where(diag, 0.0, sc["b"]))
    groups = {}
    for sc in sup:
        groups.setdefault((sc["d"], sc["q"]), []).append(sc)
    state = {(d, p): state_ref[0, d, p] for (d, q) in groups for p in (2 * q, 2 * q + 1)}
    nsub = len(next(iter(groups.values())))
    ys = {}
    for step in range(nsub):
        cur = {key: sorted(g, key=lambda sc: sc["s"], reverse=bool(key[0]))[step] for key, g in groups.items()}
        for (d, q), sc in cur.items():
            s2 = jnp.concatenate([bf(state[(d, 2 * q)]), bf(state[(d, 2 * q + 1)])], axis=1)
            sc["ksrs"] = _dot_nt(sc["kr"], blockdiag(s2))
        for (d, q), sc in cur.items():
            g = sc["av"] + sc["ksrs"][:c2]
            sc["ub"] = bf(-_dot(sc["tsw"], blockdiag(bf(swap_halves(g)))))
        for (d, q), sc in cur.items():
            for i, ch in enumerate((sc["c0"], sc["c1"])):
                key = (d, 2 * q + i)
                uv = jnp.concatenate([sc["ub"][:, i * c2:(i + 1) * c2], ch["vs"]], axis=0)
                y = sc["ksrs"][c2:, i * c2:(i + 1) * c2] + _dot(ch["arr"], uv)
                state[key] = state[key] * ch["pc"] + _dot_tn(uv, jnp.concatenate([ch["bps"], ch["kps"]], axis=0))
                ys[(d, sc["s"], key[1])] = y[:c] + y[c:]
    for key, s_new in state.items():
        state_ref[0, key[0], key[1]] = s_new
    n_pairs = len(state) // 2
    return [jnp.concatenate([jnp.concatenate([ys[(d, s, p)] for p in range(n_pairs)], axis=1)
                             for s in range(nsub)], axis=0) for d in range(2)]


def _split_dot_left(m_bf16, x, passes):
    acc = None
    rem = x
    for _ in range(passes):
        piece = rem.astype(BF16)
        term = _dot(m_bf16, piece)
        acc = term if acc is None else acc + term
        rem = rem - piece.astype(F32)
    return acc


def _rwkv_kernel(zf_ref, zfp_ref, zfn_ref, zb_ref, zbp_ref, zbn_ref, s0_ref, mu_ref, wa_ref,
                 w0a0_ref, kk_ref, ka_ref, rk_ref, g2_ref, bd_ref, tri_ref, keep_ref, ioff_ref,
                 yf_ref, yb_ref, bonus_ref, gate_ref, state_ref):
    j = pl.program_id(1)
    nc = pl.num_programs(1)
    c = CHUNK

    @pl.when(j == 0)
    def _():
        state_ref[...] = s0_ref[...]

    mu = mu_ref[...]
    k_k = kk_ref[...]
    k_a = ka_ref[...]
    bd = bd_ref[...]
    ch_f, r, k, v, gl = _rwkv_prep(zf_ref[0], zfp_ref[0, 7:8], zfn_ref[0, 0:1], j == 0, j == nc - 1,
                                   mu, wa_ref[0], w0a0_ref[0], k_k, k_a, bd, tri_ref[0], keep_ref[0], 0)
    bonus_ref[0] = _split_dot(r * k * rk_ref[...], bd, 2) * v
    gate_ref[0] = _dot(jax.nn.sigmoid(gl).astype(BF16), g2_ref[...])
    ch_b, _, _, _, _ = _rwkv_prep(zb_ref[0], zbp_ref[0, 7:8], zbn_ref[0, 0:1], j == nc - 1, j == 0,
                                  mu, wa_ref[1], w0a0_ref[1], k_k, k_a, bd, tri_ref[1], keep_ref[1], 1)
    chains = [ch for pair in zip(ch_f, ch_b) for ch in pair]
    y_f, y_b = _rwkv_solve(chains, ioff_ref[...], state_ref, c)
    yf_ref[0] = y_f
    yb_ref[0] = y_b


def _rwkv_masks(c, nsub):
    ti = np.arange(c)
    tri = np.stack([np.kron(np.eye(nsub), ti[None, :] <= ti[:, None]),
                    np.kron(np.eye(nsub), ti[None, :] >= ti[:, None])]).astype(np.float32)
    tt = (np.arange(4 * c) % c)[:, None]
    ss = (np.arange(4 * c) % c)[None, :]
    incl = (np.arange(4 * c) >= 2 * c)[:, None]
    keep = np.stack([np.where(incl, ss <= tt, ss < tt), np.where(incl, ss >= tt, ss > tt)])
    ioff = np.kron(np.array([[0.0, 1.0], [1.0, 0.0]]), np.eye(c))
    return (jnp.asarray(tri, F32).astype(BF16), jnp.asarray(keep.astype(np.float32)),
            jnp.asarray(ioff, F32))


def _rwkv_call(z, s0, mu, wa, w0a0, k_k, k_a, r_k, g2, bd):
    b, t, nz = z.shape
    c = RWKV_BLOCK_CHUNKS * CHUNK
    tri, keep, ioff = _rwkv_masks(CHUNK, RWKV_BLOCK_CHUNKS)
    nc = t // c
    hd = k_k.shape[1]
    cb = c // 8
    nb8 = t // 8
    fwd = lambda bb, j: (bb, j, 0)
    bwd = lambda bb, j: (bb, nc - 1 - j, 0)
    const2 = lambda bb, j: (0, 0)
    const3 = lambda bb, j: (0, 0, 0)
    st = pl.BlockSpec((1,) + s0.shape[1:], lambda bb, j: (bb, 0, 0, 0, 0))
    ychunk = pl.BlockSpec((1, c, hd), fwd)
    return pl.pallas_call(
        _rwkv_kernel,
        grid=(b, nc),
        in_specs=[pl.BlockSpec((1, c, nz), fwd),
                  pl.BlockSpec((1, 8, nz), lambda bb, j: (bb, jnp.maximum(j * cb - 1, 0), 0)),
                  pl.BlockSpec((1, 8, nz), lambda bb, j: (bb, jnp.minimum((j + 1) * cb, nb8 - 1), 0)),
                  pl.BlockSpec((1, c, nz), bwd),
                  pl.BlockSpec((1, 8, nz), lambda bb, j: (bb, jnp.maximum((nc - 1 - j) * cb - 1, 0), 0)),
                  pl.BlockSpec((1, 8, nz), lambda bb, j: (bb, jnp.minimum((nc - j) * cb, nb8 - 1), 0)),
                  st,
                  pl.BlockSpec(mu.shape, const2),
                  pl.BlockSpec(wa.shape, lambda bb, j: (0, 0, 0, 0)),
                  pl.BlockSpec(w0a0.shape, const3),
                  pl.BlockSpec(k_k.shape, const2),
                  pl.BlockSpec(k_a.shape, const2),
                  pl.BlockSpec(r_k.shape, const2),
                  pl.BlockSpec(g2.shape, const2),
                  pl.BlockSpec(bd.shape, const2),
                  pl.BlockSpec(tri.shape, const3),
                  pl.BlockSpec(keep.shape, const3),
                  pl.BlockSpec(ioff.shape, const2)],
        out_specs=[ychunk, pl.BlockSpec((1, c, hd), bwd), ychunk, ychunk, st],
        out_shape=[jax.ShapeDtypeStruct((b, t, hd), F32)] * 4 + [jax.ShapeDtypeStruct(s0.shape, F32)],
        compiler_params=_cparams(("parallel", "arbitrary")),
    )(z, z, z, z, z, z, s0, mu, wa, w0a0, k_k, k_a, r_k, g2, bd, tri, keep, ioff)


def _readout_kernel(att_ref, yf_ref, yb_ref, bonus_ref, gate_ref, x_ref, lg_ref, lb_ref, bd_ref,
                    wo_ref, gt_ref, g2_ref, sh_ref, sc_ref, xo_ref, h_ref):
    bd = bd_ref[...]
    inv_n = 1.0 / RWKV_N
    y = yf_ref[...] + yb_ref[...]
    mean = _split_dot(y, bd, 3) * inv_n
    yc = y - mean
    var = _split_dot(yc * yc, bd, 2) * inv_n
    yn = yc * lax.rsqrt(var + GN_EPS) * lg_ref[...] + lb_ref[...]
    rw = (yn + bonus_ref[...]) * gate_ref[...]
    cat = jnp.concatenate([att_ref[...], rw.astype(BF16)], axis=1)
    xm = x_ref[...] + gt_ref[0] * _dot(cat, wo_ref[...])
    xo_ref[...] = xm
    h_ref[...] = _rms_mod(xm, g2_ref[...], sh_ref[0], sc_ref[0]).astype(BF16)


def _readout_call(att, yf, yb, bonus, gate, x2, lnx_g, lnx_b, bd, w_out, gt1, g2, sh2, sc2,
                  rows_per_group, tm):
    r, d = x2.shape
    hd = yf.shape[1]
    per_group = rows_per_group // tm
    row = lambda i: (i, 0)
    c2 = lambda i: (0, 0)
    grp = lambda i: (i // per_group, 0, 0)
    half = pl.BlockSpec((tm, hd), row)
    full = pl.BlockSpec((tm, d), row)
    vec = pl.BlockSpec((1, 1, d), grp)
    return pl.pallas_call(
        _readout_kernel,
        grid=(r // tm,),
        in_specs=[half, half, half, half, half, full,
                  pl.BlockSpec((1, hd), c2), pl.BlockSpec((1, hd), c2), pl.BlockSpec(bd.shape, c2),
                  pl.BlockSpec(w_out.shape, c2), vec, pl.BlockSpec((1, d), c2), vec, vec],
        out_specs=[full, full],
        out_shape=[jax.ShapeDtypeStruct((r, d), F32), jax.ShapeDtypeStruct((r, d), BF16)],
        compiler_params=_cparams(("parallel",)),
    )(att, yf, yb, bonus, gate, x2, lnx_g, lnx_b, bd, w_out, gt1, g2, sh2, sc2)


def _mlp_kernel(h_ref, x_ref, w1_ref, w2_ref, gt_ref, gn_ref, shn_ref, scn_ref, xo_ref, ho_ref,
                acc_ref, *, final):
    j = pl.program_id(1)

    @pl.when(j == 0)
    def _():
        acc_ref[...] = jnp.zeros_like(acc_ref)

    a = jnp.maximum(_dot(h_ref[...], w1_ref[...]), 0.0)
    acc_ref[...] += _dot((a * a).astype(BF16), w2_ref[...])

    @pl.when(j == pl.num_programs(1) - 1)
    def _():
        xo = x_ref[...] + gt_ref[0] * acc_ref[...]
        if final:
            ms = jnp.mean(xo * xo, axis=-1, keepdims=True)
            xo_ref[...] = xo * lax.rsqrt(ms + NORM_EPS) * gn_ref[...]
            ho_ref[...] = jnp.zeros_like(ho_ref)
        else:
            xo_ref[...] = xo
            ho_ref[...] = _rms_mod(xo, gn_ref[...], shn_ref[0], scn_ref[0]).astype(BF16)


def _mlp_call(h, x2, w1, w2, gt2, g_next, sh_next, sc_next, rows_per_group, tm, tf, final):
    r, d = x2.shape
    ff = w1.shape[1]
    per_group = rows_per_group // tm
    row = lambda i, j: (i, 0)
    grp = lambda i, j: (i // per_group, 0, 0)
    vec = pl.BlockSpec((1, 1, d), grp)
    return pl.pallas_call(
        functools.partial(_mlp_kernel, final=final),
        grid=(r // tm, ff // tf),
        in_specs=[pl.BlockSpec((tm, d), row), pl.BlockSpec((tm, d), row),
                  pl.BlockSpec((d, tf), lambda i, j: (0, j)), pl.BlockSpec((tf, d), lambda i, j: (j, 0)),
                  vec, pl.BlockSpec((1, d), lambda i, j: (0, 0)), vec, vec],
        out_specs=[pl.BlockSpec((tm, d), row), pl.BlockSpec((tm, d), row)],
        out_shape=[jax.ShapeDtypeStruct((r, d), F32), jax.ShapeDtypeStruct((r, d), BF16)],
        scratch_shapes=[pltpu.VMEM((tm, d), F32)],
        compiler_params=_cparams(("parallel", "arbitrary")),
    )(h, x2, w1, w2, gt2, g_next, sh_next, sc_next)


def _fft1_kernel(h_ref, kr_ref, wc_ref, z_ref):
    l1, nb, d = h_ref.shape[1:]
    rows = l1 * nb
    hf = h_ref[0].reshape(rows, d)
    p = _dot(kr_ref[...], hf)
    gw = d // FOURIER_GROUPS
    wc = wc_ref[...]
    zr, zi = [], []
    for g in range(FOURIER_GROUPS):
        ap = jnp.concatenate([p[:rows, g * gw:(g + 1) * gw], p[rows:, g * gw:(g + 1) * gw]],
                             axis=1).astype(BF16)
        zz = _dot(ap, wc)
        zr.append(zz[:, :gw])
        zi.append(zz[:, gw:])
    z = jnp.concatenate(zr + zi, axis=1).astype(BF16)
    z_ref[0] = z.reshape(l1, nb, 2 * d)


def _fft1_call(h4, kr1, wc, nb):
    b, l1, l2, d = h4.shape
    return pl.pallas_call(
        _fft1_kernel,
        grid=(b, l2 // nb),
        in_specs=[pl.BlockSpec((1, l1, nb, d), lambda bb, j: (bb, 0, j, 0)),
                  pl.BlockSpec(kr1.shape, lambda bb, j: (0, 0)),
                  pl.BlockSpec(wc.shape, lambda bb, j: (0, 0))],
        out_specs=pl.BlockSpec((1, l1, nb, 2 * d), lambda bb, j: (bb, 0, j, 0)),
        out_shape=jax.ShapeDtypeStruct((b, l1, l2, 2 * d), BF16),
        compiler_params=_cparams(("parallel", "parallel")),
    )(h4, kr1, wc)


def _fft2_kernel(z_ref, gk_ref, x_ref, wo_ref, gt_ref, g2_ref, sh_ref, sc_ref, xo_ref, h_ref):
    mb, l2, d2 = z_ref.shape[1:]
    m2b = x_ref.shape[1]
    d = d2 // 2
    z = z_ref[0].reshape(mb * l2, d2)
    rhs = jnp.concatenate([z[:, :d], z[:, d:]], axis=0)
    f = _dot(gk_ref[0], rhs)
    y = _dot(f.astype(BF16), wo_ref[...])
    xm = x_ref[0].reshape(m2b * mb, d) + gt_ref[0] * y
    xo_ref[0] = xm.reshape(m2b, mb, d)
    h_ref[0] = _rms_mod(xm, g2_ref[...], sh_ref[0], sc_ref[0]).astype(BF16).reshape(m2b, mb, d)


def _fft2_call(z4, gk, x4, w_out, gt1, g2, sh2, sc2, mb, m2b):
    b, l1, l2, d2 = z4.shape
    d = d2 // 2
    nblk = l1 // mb
    xspec = pl.BlockSpec((1, m2b, mb, d), lambda m, bb, h: (bb, h, m, 0))
    vec = pl.BlockSpec((1, 1, d), lambda m, bb, h: (bb, 0, 0))
    return pl.pallas_call(
        _fft2_kernel,
        grid=(nblk, b, l2 // m2b),
        in_specs=[pl.BlockSpec((1, mb, l2, d2), lambda m, bb, h: (bb, m, 0, 0)),
                  pl.BlockSpec((1, m2b * mb, gk.shape[2]), lambda m, bb, h: (m, h, 0)),
                  xspec,
                  pl.BlockSpec(w_out.shape, lambda m, bb, h: (0, 0)),
                  vec, pl.BlockSpec((1, d), lambda m, bb, h: (0, 0)), vec, vec],
        out_specs=[xspec, xspec],
        out_shape=[jax.ShapeDtypeStruct((b, l2, l1, d), F32), jax.ShapeDtypeStruct((b, l2, l1, d), BF16)],
        compiler_params=_cparams(("parallel", "parallel", "parallel")),
    )(z4, gk, x4, w_out, gt1, g2, sh2, sc2)


def _rope_tables(t):
    axis_dim = HEAD_DIM // 2
    rows = t // GRID_W
    row = jnp.broadcast_to(jnp.arange(rows, dtype=F32)[:, None], (rows, GRID_W)).reshape(t)
    col = jnp.broadcast_to(jnp.arange(GRID_W, dtype=F32)[None, :], (rows, GRID_W)).reshape(t)
    inv = ROPE_BASE ** (-jnp.arange(0, axis_dim, 2, dtype=F32) / axis_dim)
    ang_r, ang_c = row[:, None] * inv, col[:, None] * inv
    cos = jnp.concatenate([jnp.cos(ang_r), jnp.cos(ang_r), jnp.cos(ang_c), jnp.cos(ang_c)], axis=1)
    sin = jnp.concatenate([-jnp.sin(ang_r), jnp.sin(ang_r), -jnp.sin(ang_c), jnp.sin(ang_c)], axis=1)
    return jnp.tile(cos, (1, 2)), jnp.tile(sin, (1, 2))


def _fft_tables(t, gw, nb, mb):
    l1 = t // FFT_L2
    n1 = np.arange(l1)
    ang1 = 2.0 * np.pi * np.outer(n1, n1) / l1
    eye = np.eye(nb)
    kr1 = np.concatenate([np.kron(np.cos(ang1), eye), np.kron(np.sin(ang1), eye)], axis=0)
    ch = np.arange(gw)
    angc = 2.0 * np.pi * np.outer(ch, ch) / gw
    cg, sg = np.cos(angc), np.sin(angc)
    wc = np.block([[cg, -sg], [-sg, -cg]])
    scale = 1.0 / np.sqrt(float(t) * gw)
    m = np.arange(t)
    n2 = np.arange(FFT_L2)
    theta = 2.0 * np.pi * np.outer(m, n2) / t
    cs = np.stack([np.cos(theta), np.sin(theta)], axis=0) * scale
    cs = cs.reshape(2, FFT_L2, l1 // mb, mb, FFT_L2)
    return (jnp.asarray(kr1, F32).astype(BF16), jnp.asarray(wc, F32).astype(BF16),
            jnp.asarray(cs, F32))


def _expand_gk(cs, mb):
    eye = jnp.eye(mb, dtype=F32)
    g = jnp.einsum("rmbpn,pq->bmprqn", cs, eye)
    nblk = cs.shape[2]
    return g.reshape(nblk, FFT_L2 * mb, 2 * mb * FFT_L2).astype(BF16)


def kernel(x, c, ctx, c_ctx, ada_w, ada_b, norm1_g, norm2_g, mix_w_in, mix_w_out, attn_sink,
           shift_mu_prev, shift_mu_next, decay_w0, decay_w2, iclr_a0, iclr_a2, gate_g2, key_kk,
           key_ka, bonus_rk, lnx_g, lnx_b, fourier_w_out, mlp_w1, mlp_w2, final_g):
    b, t, d = x.shape
    nctx = ctx.shape[1]
    hd = key_kk.shape[1]
    q_dim = d - hd
    n_heads = q_dim // HEAD_DIM
    kv_dim = (n_heads // 4) * HEAD_DIM
    att_cols = q_dim + 2 * kv_dim

    cond = jnp.zeros((8, d), F32).at[:b].set(c).at[b].set(c_ctx)
    mods = _ada_call(cond, ada_w, ada_b)
    lat = [mods[i, :b].reshape(b, N_MOD, 1, d) for i in range(2)]
    cmod = [mods[i, b:b + 1].reshape(1, N_MOD, 1, d) for i in range(2)]
    lm = lambda i, k: lat[i][:, k]
    cm = lambda i, k: cmod[i][:, k]
    row1 = lambda a: a.reshape(1, -1)

    w_in = mix_w_in[0]
    wk = w_in[:, q_dim:q_dim + kv_dim].reshape(d, kv_dim // HEAD_DIM, 1, HEAD_DIM)
    wv = w_in[:, q_dim + kv_dim:att_cols].reshape(d, kv_dim // HEAD_DIM, 1, HEAD_DIM)
    dup = lambda w: jnp.broadcast_to(w, (d, kv_dim // HEAD_DIM, 2, HEAD_DIM)).reshape(d, 2 * kv_dim)
    w_att = jnp.concatenate([w_in[:, :q_dim], dup(wk), dup(wv)], axis=1).astype(BF16)
    w_rw = w_in[:, att_cols:].astype(BF16)
    cos_t, sin_t = _rope_tables(t)
    cos_c, sin_c = jnp.ones((nctx, LANES), F32), jnp.zeros((nctx, LANES), F32)
    g1 = row1(norm1_g[0])
    x2 = x.reshape(b * t, d)
    ctx2 = ctx.reshape(b * nctx, d)
    tm_in = min(512, t)
    q, kd, vd, zrw = _inproj_call(x2, g1, lm(0, 0), lm(0, 1), cos_t, sin_t, w_att, w_rw, t, tm_in)
    qc, kc, vc, zrwc = _inproj_call(ctx2, g1, cm(0, 0), cm(0, 1), cos_c, sin_c, w_att, w_rw,
                                    b * nctx, nctx)
    q, kd, vd = (a.reshape(b, t, -1) for a in (q, kd, vd))
    qc, kc, vc = (a.reshape(b, nctx, -1) for a in (qc, kc, vc))
    sinkb = jnp.broadcast_to(attn_sink[0][:, None] * LOG2E, (n_heads, LANES)).astype(F32)
    att = _attn_call(q, kd, vd, kc, vc, sinkb, 3)
    att_c = _attn_call(qc, kc, vc, kc, vc, sinkb, 0)

    mu = jnp.stack([shift_mu_prev[0], shift_mu_next[0], 1.0 - shift_mu_prev[0] - shift_mu_next[0]])
    zl = jnp.zeros((DECAY_LORA, hd), F32)
    wa = jnp.stack([jnp.concatenate([jnp.concatenate([decay_w2[0, dd], zl], axis=1),
                                     jnp.concatenate([zl, iclr_a2[0, dd]], axis=1)], axis=0)
                    for dd in range(2)])
    wa_hi = wa.astype(BF16)
    wa = jnp.stack([wa_hi, (wa - wa_hi.astype(F32)).astype(BF16)], axis=1)
    w0a0 = jnp.concatenate([decay_w0[0], iclr_a0[0]], axis=1).reshape(2, 1, 2 * hd)
    seg = np.arange(hd) // RWKV_N
    bd = jnp.asarray(seg[:, None] == seg[None, :], F32).astype(BF16)
    k_k, k_a, r_k = row1(key_kk[0]), row1(key_ka[0]), row1(bonus_rk[0])
    g2w = gate_g2[0].astype(BF16)
    s_zero = jnp.zeros((b, 2, hd // LANES, LANES, LANES), F32)
    yfc, ybc, bonus_c, gate_c, s_ctx = _rwkv_call(zrwc.reshape(b, nctx, -1), s_zero, mu, wa, w0a0,
                                                  k_k, k_a, r_k, g2w, bd)
    yf, yb, bonus, gate, _ = _rwkv_call(zrw.reshape(b, t, -1), s_ctx, mu, wa, w0a0,
                                        k_k, k_a, r_k, g2w, bd)

    w_out = mix_w_out[0].astype(BF16)
    n2g = row1(norm2_g[0])
    flat = lambda a: a.reshape(-1, a.shape[-1])
    xm, h2 = _readout_call(flat(att), flat(yf), flat(yb), flat(bonus), flat(gate), x2,
                           row1(lnx_g[0]), row1(lnx_b[0]), bd, w_out, lm(0, 2), n2g,
                           lm(0, 3), lm(0, 4), t, tm_in)
    xmc, h2c = _readout_call(flat(att_c), flat(yfc), flat(ybc), flat(bonus_c), flat(gate_c), ctx2,
                             row1(lnx_g[0]), row1(lnx_b[0]), bd, w_out, cm(0, 2), n2g,
                             cm(0, 3), cm(0, 4), b * nctx, nctx)
    w1 = mlp_w1[0].astype(BF16)
    w2 = mlp_w2[0].astype(BF16)
    g1n = row1(norm1_g[1])
    tm_mlp = min(1024, t)
    x1, h1 = _mlp_call(h2, xm, w1, w2, lm(0, 5), g1n, lm(1, 0), lm(1, 1), t, tm_mlp, 512, False)
    ctx1, _ = _mlp_call(h2c, xmc, w1, w2, cm(0, 5), g1n, cm(1, 0), cm(1, 1), b * nctx, nctx, 512,
                        False)
    del ctx1

    l1 = t // FFT_L2
    nb = 16
    mb = min(8, l1)
    gw = d // FOURIER_GROUPS
    kr1, wc, cs = _fft_tables(t, gw, nb, mb)
    gk = _expand_gk(cs, mb)
    z4 = _fft1_call(h1.reshape(b, l1, FFT_L2, d), kr1, wc, nb)
    xm4, h24 = _fft2_call(z4, gk, x1.reshape(b, FFT_L2, l1, d), fourier_w_out[0].astype(BF16),
                          lm(1, 2), row1(norm2_g[1]), lm(1, 3), lm(1, 4), mb, 64)
    out, _ = _mlp_call(h24.reshape(b * t, d), xm4.reshape(b * t, d), mlp_w1[1].astype(BF16),
                       mlp_w2[1].astype(BF16), lm(1, 5), row1(final_g), lm(1, 0), lm(1, 1),
                       t, tm_mlp, 512, True)
    return out.reshape(b, t, d)
```

```python
import functools

import numpy as np
import jax
import jax.numpy as jnp
from jax import lax
from jax.experimental import pallas as pl
from jax.experimental.pallas import tpu as pltpu

F32 = jnp.float32
BF16 = jnp.bfloat16
HIGHEST = lax.Precision.HIGHEST

HEAD_DIM = 64
WINDOW = 128
QBLK = 128
GRID_W = 64
ROPE_BASE = 10000.0
RWKV_N = 64
DECAY_LORA = 64
ICLR_LORA = 64
GATE_LORA = 128
FOURIER_GROUPS = 4
N_MOD = 6
NORM_EPS = 1e-6
GN_EPS = 64e-5
NEG_INF = -1e30

CHUNK = 64
RWKV_BLOCK_CHUNKS = 2
LANES = 128
FFT_L2 = 128
MLP_TF = 1024
VMEM_LIMIT = 48 * 1024 * 1024
LOG2E = 1.4426950408889634
Q_SCALE = HEAD_DIM ** -0.5 * LOG2E


def _cparams(sem):
    return pltpu.CompilerParams(dimension_semantics=sem, vmem_limit_bytes=VMEM_LIMIT)


def _dot(a, b, **kw):
    return jnp.dot(a, b, preferred_element_type=F32, **kw)


def _dot_nt(a, b):
    return lax.dot_general(a, b, (((1,), (1,)), ((), ())), preferred_element_type=F32)


def _dot_tn(a, b):
    return lax.dot_general(a, b, (((0,), (0,)), ((), ())), preferred_element_type=F32)


def _split_dot(x, m_bf16, passes):
    acc = None
    rem = x
    for _ in range(passes):
        piece = rem.astype(BF16)
        term = _dot(piece, m_bf16)
        acc = term if acc is None else acc + term
        rem = rem - piece.astype(F32)
    return acc


def _rms_mod(x, g, sh, sc):
    ms = jnp.mean(x * x, axis=-1, keepdims=True)
    return (x * lax.rsqrt(ms + NORM_EPS)) * g * (1.0 + sc) + sh


def _ada_kernel(cond_ref, w_ref, b_ref, o_ref):
    s = cond_ref[...]
    s = s * jax.nn.sigmoid(s)
    o_ref[0] = _dot(s, w_ref[0], precision=HIGHEST) + b_ref[0]


def _ada_call(cond, ada_w, ada_b):
    depth, d, n = ada_w.shape
    tn = 1536
    return pl.pallas_call(
        _ada_kernel,
        grid=(depth, n // tn),
        in_specs=[pl.BlockSpec((8, d), lambda i, j: (0, 0)),
                  pl.BlockSpec((1, d, tn), lambda i, j: (i, 0, j)),
                  pl.BlockSpec((1, 1, tn), lambda i, j: (i, 0, j))],
        out_specs=pl.BlockSpec((1, 8, tn), lambda i, j: (i, 0, j)),
        out_shape=jax.ShapeDtypeStruct((depth, 8, n), F32),
        compiler_params=_cparams(("parallel", "parallel")),
    )(cond, ada_w, ada_b.reshape(depth, 1, n))


def _inproj_kernel(x_ref, g_ref, sh_ref, sc_ref, cos_ref, sin_ref, wa_ref, wr_ref,
                   q_ref, k_ref, v_ref, z_ref):
    h = _rms_mod(x_ref[...], g_ref[...], sh_ref[0], sc_ref[0]).astype(BF16)
    z_ref[...] = _dot(h, wr_ref[...])
    za = _dot(h, wa_ref[...])
    cos = cos_ref[...]
    sin = sin_ref[...]
    lane = lax.broadcasted_iota(jnp.int32, cos.shape, 1)
    first = (lane % 32) < 16
    nq = q_ref.shape[1] // LANES
    nk = k_ref.shape[1] // LANES
    for c in range(nq + nk):
        s = za[:, c * LANES:(c + 1) * LANES]
        partner = jnp.where(first, pltpu.roll(s, LANES - 16, 1), pltpu.roll(s, 16, 1))
        ro = s * cos + partner * sin
        if c < nq:
            q_ref[:, c * LANES:(c + 1) * LANES] = (ro * Q_SCALE).astype(BF16)
        else:
            k_ref[:, (c - nq) * LANES:(c - nq + 1) * LANES] = ro.astype(BF16)
    v_ref[...] = za[:, (nq + nk) * LANES:].astype(BF16)


def _inproj_call(x2, g1, sh, sc, cos, sin, w_att, w_rw, rows_per_group, tm):
    r, d = x2.shape
    period = cos.shape[0]
    n_per = period // tm
    per_group = rows_per_group // tm
    na = w_att.shape[1]
    nr = w_rw.shape[1]
    nq, nkd = 512, 256
    row = lambda i: (i, 0)
    grp = lambda i: (i // per_group, 0, 0)
    return pl.pallas_call(
        _inproj_kernel,
        grid=(r // tm,),
        in_specs=[pl.BlockSpec((tm, d), row),
                  pl.BlockSpec((1, d), lambda i: (0, 0)),
                  pl.BlockSpec((1, 1, d), grp),
                  pl.BlockSpec((1, 1, d), grp),
                  pl.BlockSpec((tm, LANES), lambda i: (i % n_per, 0)),
                  pl.BlockSpec((tm, LANES), lambda i: (i % n_per, 0)),
                  pl.BlockSpec((d, na), lambda i: (0, 0)),
                  pl.BlockSpec((d, nr), lambda i: (0, 0))],
        out_specs=[pl.BlockSpec((tm, nq), row), pl.BlockSpec((tm, nkd), row),
                   pl.BlockSpec((tm, nkd), row), pl.BlockSpec((tm, nr), row)],
        out_shape=[jax.ShapeDtypeStruct((r, nq), BF16), jax.ShapeDtypeStruct((r, nkd), BF16),
                   jax.ShapeDtypeStruct((r, nkd), BF16), jax.ShapeDtypeStruct((r, nr), F32)],
        compiler_params=_cparams(("parallel",)),
    )(x2, g1, sh, sc, cos, sin, w_att, w_rw)


def _attn_kernel(*refs, n_loc, seq_len):
    q_ref = refs[0]
    k_refs = refs[1:1 + n_loc]
    v_refs = refs[1 + n_loc:1 + 2 * n_loc]
    kc_ref, vc_ref, sink_ref, o_ref = refs[1 + 2 * n_loc:]
    i = pl.program_id(1)
    nb = pl.num_programs(1)
    nctx = kc_ref.shape[1]
    nkeys = n_loc * QBLK + nctx
    gq = 4
    rows = gq * QBLK
    low = lax.broadcasted_iota(jnp.int32, (QBLK, LANES), 1) < HEAD_DIM
    if n_loc:
        rq = lax.broadcasted_iota(jnp.int32, (rows, QBLK), 0) % QBLK
        ck = lax.broadcasted_iota(jnp.int32, (rows, QBLK), 1)
        mask_prev = ck >= rq + jnp.where(i >= 1, 0, QBLK)
        mask_next = ck <= rq - jnp.where(i <= nb - 2, 0, QBLK)
    zero = jnp.zeros((QBLK, LANES), BF16)
    ones = jnp.ones((nkeys, LANES), BF16)
    n_groups = q_ref.shape[2] // (gq * HEAD_DIM)
    scores, vals, sinks = [], [], []
    for g in range(n_groups):
        ksl = slice(g * LANES, (g + 1) * LANES)
        keys = jnp.concatenate([kr[0, :, ksl] for kr in k_refs] + [kc_ref[0, :, ksl]], axis=0)
        vals.append(jnp.concatenate(
            [jnp.concatenate([vr[0, :, ksl] for vr in v_refs] + [vc_ref[0, :, ksl]], axis=0), ones], axis=1))
        qs = []
        for pp in range(2):
            qp = q_ref[0, :, (2 * g + pp) * LANES:(2 * g + pp + 1) * LANES]
            qs += [jnp.where(low, qp, zero), jnp.where(low, zero, qp)]
        scores.append(_dot_nt(jnp.concatenate(qs, axis=0), keys))
        sinks.append(jnp.concatenate(
            [jnp.broadcast_to(sink_ref[gq * g + h:gq * g + h + 1, 0:1], (QBLK, 1)) for h in range(gq)], axis=0))
    probs, ms = [], []
    for g in range(n_groups):
        s = scores[g]
        if n_loc:
            s = jnp.concatenate([jnp.where(mask_prev, s[:, :QBLK], NEG_INF), s[:, QBLK:2 * QBLK],
                                 jnp.where(mask_next, s[:, 2 * QBLK:3 * QBLK], NEG_INF), s[:, 3 * QBLK:]],
                                axis=1)
        m = jnp.maximum(jnp.max(s, axis=-1, keepdims=True), sinks[g])
        ms.append(m)
        probs.append(jnp.exp2(s - m).astype(BF16))
    for g in range(n_groups):
        o = _dot(probs[g], vals[g])
        out = o[:, :LANES] / (o[:, LANES:] + jnp.exp2(sinks[g] - ms[g]))
        for pp in range(2):
            even = out[2 * pp * QBLK:(2 * pp + 1) * QBLK]
            odd = out[(2 * pp + 1) * QBLK:(2 * pp + 2) * QBLK]
            p = 2 * g + pp
            o_ref[0, :, p * LANES:(p + 1) * LANES] = jnp.where(low, even, odd).astype(BF16)


def _attn_call(q, kd, vd, kc, vc, sinkb, n_loc):
    b, t, nq = q.shape
    nb = t // QBLK
    nctx = kc.shape[1]
    kw = kd.shape[2]
    qspec = pl.BlockSpec((1, QBLK, nq), lambda bb, i: (bb, i, 0))
    loc = []
    for off in (-1, 0, 1)[:n_loc]:
        loc.append(pl.BlockSpec((1, QBLK, kw), functools.partial(
            lambda bb, i, off: (bb, jnp.clip(i + off, 0, nb - 1), 0), off=off)))
    cspec = pl.BlockSpec((1, nctx, kw), lambda bb, i: (bb, 0, 0))
    args = [q] + [kd] * n_loc + [vd] * n_loc + [kc, vc, sinkb]
    return pl.pallas_call(
        functools.partial(_attn_kernel, n_loc=n_loc, seq_len=t),
        grid=(b, nb),
        in_specs=[qspec] + loc + loc + [cspec, cspec, pl.BlockSpec(sinkb.shape, lambda bb, i: (0, 0))],
        out_specs=qspec,
        out_shape=jax.ShapeDtypeStruct((b, t, nq), BF16),
        compiler_params=_cparams(("parallel", "parallel")),
    )(*args)


def _rwkv_prep(z, prow, nrow, is_first, is_last, mu, wa, w0a0, k_k, k_a, bd, tri, keep, d):
    nrows = z.shape[0]
    c = CHUNK
    hd = k_k.shape[1]
    row8 = lax.broadcasted_iota(jnp.int32, (8, z.shape[1]), 0)
    zp = pltpu.roll(z, 1, 0)
    zp = jnp.concatenate([jnp.where(row8 == 0, jnp.where(is_first, 0.0, prow), zp[:8]), zp[8:]], axis=0)
    zn = pltpu.roll(z, nrows - 1, 0)
    zn = jnp.concatenate([zn[:nrows - 8],
                          jnp.where(row8 == 7, jnp.where(is_last, 0.0, nrow), zn[nrows - 8:])], axis=0)
    zs = mu[2:3] * z + mu[0:1] * zp + mu[1:2] * zn
    r = zs[:, 0:hd]
    k = zs[:, hd:2 * hd]
    v = zs[:, 2 * hd:3 * hd]
    wa_in = zs[:, 3 * hd:3 * hd + LANES]
    gl = zs[:, 3 * hd + LANES:]
    low_r = lax.broadcasted_iota(jnp.int32, (nrows, LANES), 1) < DECAY_LORA
    low = lax.broadcasted_iota(jnp.int32, (c, LANES), 1) < RWKV_N
    tw = jnp.where(low_r, jnp.tanh(wa_in), wa_in)
    xwa = _split_dot(tw, wa, 2) + w0a0
    xw = xwa[:, :hd]
    a = jax.nn.sigmoid(xwa[:, hd:])
    w_log = jnp.minimum(xw, 0.0) - jnp.log(1.0 + jnp.exp(-jnp.abs(xw))) - 0.5
    lw = -jnp.exp(w_log)
    kkr = k * k_k
    kk = kkr * lax.rsqrt(_split_dot(kkr * kkr, bd, 1) + 1e-12)
    kd = k * (1.0 + (a - 1.0) * k_a)
    bb = kk * a
    cum = _split_dot_left(tri, lw, 2)
    nsub = nrows // c
    totals = [cum[s * c:s * c + 1] if d else cum[s * c + c - 1:s * c + c] for s in range(nsub)]
    total = jnp.concatenate([jnp.broadcast_to(tl, (c, hd)) for tl in totals], axis=0)
    e_in = jnp.exp(cum)
    e_ex = jnp.exp(cum - lw)
    e_neg = jnp.exp(-cum)
    e_rem = jnp.exp(total - cum)
    kt = kk * e_ex
    rt = r * e_in
    bt = bb * e_neg
    kdt = kd * e_neg
    bp = bb * e_rem
    kp = kd * e_rem

    def stack(xp):
        zero = jnp.zeros_like(xp)
        return jnp.concatenate([jnp.where(low, xp, zero), jnp.where(low, zero, xp)], axis=0)

    chains = []
    for s in range(nsub):
        pc = jnp.exp(totals[s])
        for p in range(hd // LANES):
            sl = slice(p * LANES, (p + 1) * LANES)
            sb = lambda a: stack(a[s * c:(s + 1) * c, sl].astype(BF16))
            chains.append(dict(
                d=d, p=p, s=s, keep=keep, pc=pc[:, sl],
                kts=sb(kt), rtb=sb(rt), bts=sb(bt), kdts=sb(kdt), vs=sb(v), bps=sb(bp), kps=sb(kp)))
    return chains, r, k, v, gl


def _rwkv_solve(chains, ioff, state_ref, c):
    c2 = 2 * c
    bf = lambda a: a.astype(BF16)
    rr = lax.broadcasted_iota(jnp.int32, (c2, 2 * c2), 0)
    cc = lax.broadcasted_iota(jnp.int32, (c2, 2 * c2), 1)
    diag = (rr // c) == ((cc // c) % 2)

    def blockdiag(ab):
        n = ab.shape[1] // 2
        z = jnp.zeros((ab.shape[0], n), ab.dtype)
        return jnp.concatenate([jnp.concatenate([ab[:, :n], z], axis=1),
                                jnp.concatenate([z, ab[:, n:]], axis=1)], axis=0)

    def swap_halves(g):
        return jnp.concatenate([g[c:], g[:c]], axis=0)

    by_key = {(ch["d"], ch["s"], ch["p"]): ch for ch in chains}
    supers = [(by_key[(d, s, p)], by_key[(d, s, p + 1)])
              for (d, s, p) in sorted(by_key) if p % 2 == 0]
    sup = [dict(c0=a, c1=b, d=a["d"], s=a["s"], q=a["p"] // 2) for a, b in supers]
    for ch in chains:
        ch["kr"] = jnp.concatenate([ch["kts"], ch["rtb"]], axis=0)
        aa = _dot_nt(ch["kr"], jnp.concatenate([ch["bts"], ch["kdts"]], axis=0))
        aa = aa * ch["keep"]
        ch["auk"] = aa[:c2, c2:]
        ch["arr"] = bf(aa[c2:, :])
        ch["b0"] = ioff - aa[:c2, :c2]
    for sc in sup:
        sc["b"] = jnp.concatenate([sc["c0"]["b0"], sc["c1"]["b0"]], axis=1)
        sc["kr"] = jnp.concatenate([sc["c0"]["kr"], sc["c1"]["kr"]], axis=1)
    for _ in range(6):
        for sc in sup:
            b = sc["b"]
            xbd = jnp.where(diag, b, 0.0)
            sc["b"] = _dot(bf(xbd), blockdiag(bf(b))) + (b - xbd)
    for sc in sup:
        auk = jnp.concatenate([sc["c0"]["auk"], sc["c1"]["auk"]], axis=1)
        vs2 = jnp.concatenate([sc["c0"]["vs"], sc["c1"]["vs"]], axis=1)
        sc["av"] = _dot(bf(auk), blockdiag(vs2))
    for sc in sup:
        sc["tsw"] = bf(jnp.where(diag, 0.0, sc["b"]))
    groups = {}
    for sc in sup:
        groups.setdefault((sc["d"], sc["q"]), []).append(sc)
    state = {(d, p): state_ref[0, d, p] for (d, q) in groups for p in (2 * q, 2 * q + 1)}
    nsub = len(next(iter(groups.values())))
    ys = {}
    for step in range(nsub):
        cur = {key: sorted(g, key=lambda sc: sc["s"], reverse=bool(key[0]))[step] for key, g in groups.items()}
        for (d, q), sc in cur.items():
            s2 = jnp.concatenate([bf(state[(d, 2 * q)]), bf(state[(d, 2 * q + 1)])], axis=1)
            sc["ksrs"] = _dot_nt(sc["kr"], blockdiag(s2))
        for (d, q), sc in cur.items():
            g = sc["av"] + sc["ksrs"][:c2]
            sc["ub"] = bf(-_dot(sc["tsw"], blockdiag(bf(swap_halves(g)))))
        for (d, q), sc in cur.items():
            for i, ch in enumerate((sc["c0"], sc["c1"])):
                key = (d, 2 * q + i)
                uv = jnp.concatenate([sc["ub"][:, i * c2:(i + 1) * c2], ch["vs"]], axis=0)
                y = sc["ksrs"][c2:, i * c2:(i + 1) * c2] + _dot(ch["arr"], uv)
                state[key] = state[key] * ch["pc"] + _dot_tn(uv, jnp.concatenate([ch["bps"], ch["kps"]], axis=0))
                ys[(d, sc["s"], key[1])] = y[:c] + y[c:]
    for key, s_new in state.items():
        state_ref[0, key[0], key[1]] = s_new
    dirs = sorted({d for d, _ in state})
    n_pairs = len(state) // len(dirs)
    return [jnp.concatenate([jnp.concatenate([ys[(d, s, p)] for p in range(n_pairs)], axis=1)
                             for s in range(nsub)], axis=0) for d in dirs]


def _split_dot_left(m_bf16, x, passes):
    acc = None
    rem = x
    for _ in range(passes):
        piece = rem.astype(BF16)
        term = _dot(m_bf16, piece)
        acc = term if acc is None else acc + term
        rem = rem - piece.astype(F32)
    return acc


def _rwkv_kernel(zf_ref, zfp_ref, zfn_ref, zb_ref, zbp_ref, zbn_ref, s0_ref, mu_ref, wa_ref,
                 w0a0_ref, kk_ref, ka_ref, rk_ref, g2_ref, bd_ref, tri_ref, keep_ref, ioff_ref,
                 yf_ref, yb_ref, bonus_ref, gate_ref, state_ref):
    j = pl.program_id(1)
    nc = pl.num_programs(1)
    c = CHUNK

    @pl.when(j == 0)
    def _():
        state_ref[...] = s0_ref[...]

    mu = mu_ref[...]
    k_k = kk_ref[...]
    k_a = ka_ref[...]
    bd = bd_ref[...]
    ch_f, r, k, v, gl = _rwkv_prep(zf_ref[0], zfp_ref[0, 7:8], zfn_ref[0, 0:1], j == 0, j == nc - 1,
                                   mu, wa_ref[0], w0a0_ref[0], k_k, k_a, bd, tri_ref[0], keep_ref[0], 0)
    bonus_ref[0] = _split_dot(r * k * rk_ref[...], bd, 1) * v
    gate_ref[0] = _dot(jax.nn.sigmoid(gl).astype(BF16), g2_ref[...])
    ch_b, _, _, _, _ = _rwkv_prep(zb_ref[0], zbp_ref[0, 7:8], zbn_ref[0, 0:1], j == nc - 1, j == 0,
                                  mu, wa_ref[1], w0a0_ref[1], k_k, k_a, bd, tri_ref[1], keep_ref[1], 1)
    yf_ref[0], yb_ref[0] = _rwkv_solve(ch_f + ch_b, ioff_ref[...], state_ref, c)


def _rwkv_masks(c, nsub):
    ti = np.arange(c)
    tri = np.stack([np.kron(np.eye(nsub), ti[None, :] <= ti[:, None]),
                    np.kron(np.eye(nsub), ti[None, :] >= ti[:, None])]).astype(np.float32)
    tt = (np.arange(4 * c) % c)[:, None]
    ss = (np.arange(4 * c) % c)[None, :]
    incl = (np.arange(4 * c) >= 2 * c)[:, None]
    keep = np.stack([np.where(incl, ss <= tt, ss < tt), np.where(incl, ss >= tt, ss > tt)])
    ioff = np.kron(np.array([[0.0, 1.0], [1.0, 0.0]]), np.eye(c))
    return (jnp.asarray(tri, F32).astype(BF16), jnp.asarray(keep.astype(np.float32)),
            jnp.asarray(ioff, F32))


def _rwkv_call(z, s0, mu, wa, w0a0, k_k, k_a, r_k, g2, bd):
    b, t, nz = z.shape
    c = RWKV_BLOCK_CHUNKS * CHUNK
    tri, keep, ioff = _rwkv_masks(CHUNK, RWKV_BLOCK_CHUNKS)
    nc = t // c
    hd = k_k.shape[1]
    cb = c // 8
    nb8 = t // 8
    fwd = lambda bb, j: (bb, j, 0)
    bwd = lambda bb, j: (bb, nc - 1 - j, 0)
    const2 = lambda bb, j: (0, 0)
    const3 = lambda bb, j: (0, 0, 0)
    st = pl.BlockSpec((1,) + s0.shape[1:], lambda bb, j: (bb, 0, 0, 0, 0))
    ychunk = pl.BlockSpec((1, c, hd), fwd)
    return pl.pallas_call(
        _rwkv_kernel,
        grid=(b, nc),
        in_specs=[pl.BlockSpec((1, c, nz), fwd),
                  pl.BlockSpec((1, 8, nz), lambda bb, j: (bb, jnp.maximum(j * cb - 1, 0), 0)),
                  pl.BlockSpec((1, 8, nz), lambda bb, j: (bb, jnp.minimum((j + 1) * cb, nb8 - 1), 0)),
                  pl.BlockSpec((1, c, nz), bwd),
                  pl.BlockSpec((1, 8, nz), lambda bb, j: (bb, jnp.maximum((nc - 1 - j) * cb - 1, 0), 0)),
                  pl.BlockSpec((1, 8, nz), lambda bb, j: (bb, jnp.minimum((nc - j) * cb, nb8 - 1), 0)),
                  st,
                  pl.BlockSpec(mu.shape, const2),
                  pl.BlockSpec(wa.shape, const3),
                  pl.BlockSpec(w0a0.shape, const3),
                  pl.BlockSpec(k_k.shape, const2),
                  pl.BlockSpec(k_a.shape, const2),
                  pl.BlockSpec(r_k.shape, const2),
                  pl.BlockSpec(g2.shape, const2),
                  pl.BlockSpec(bd.shape, const2),
                  pl.BlockSpec(tri.shape, const3),
                  pl.BlockSpec(keep.shape, const3),
                  pl.BlockSpec(ioff.shape, const2)],
        out_specs=[ychunk, pl.BlockSpec((1, c, hd), bwd), ychunk, ychunk, st],
        out_shape=[jax.ShapeDtypeStruct((b, t, hd), F32)] * 4 + [jax.ShapeDtypeStruct(s0.shape, F32)],
        compiler_params=_cparams(("parallel", "arbitrary")),
    )(z, z, z, z, z, z, s0, mu, wa, w0a0, k_k, k_a, r_k, g2, bd, tri, keep, ioff)


def _readout_kernel(att_ref, yf_ref, yb_ref, bonus_ref, gate_ref, x_ref, lg_ref, lb_ref, bd_ref,
                    wo_ref, gt_ref, g2_ref, sh_ref, sc_ref, xo_ref, h_ref):
    bd = bd_ref[...]
    inv_n = 1.0 / RWKV_N
    y = yf_ref[...] + yb_ref[...]
    tm = y.shape[0]
    ysq = y * y
    y_hi = y.astype(BF16)
    q_hi = ysq.astype(BF16)
    parts = jnp.concatenate([y_hi, (y - y_hi.astype(F32)).astype(BF16),
                             q_hi, (ysq - q_hi.astype(F32)).astype(BF16)], axis=0)
    st = _dot(parts, bd) * inv_n
    mean = st[:tm] + st[tm:2 * tm]
    var = st[2 * tm:3 * tm] + st[3 * tm:] - mean * mean
    yn = (y - mean) * lax.rsqrt(var + GN_EPS) * lg_ref[...] + lb_ref[...]
    rw = (yn + bonus_ref[...]) * gate_ref[...]
    cat = jnp.concatenate([att_ref[...], rw.astype(BF16)], axis=1)
    xm = x_ref[...] + gt_ref[0] * _dot(cat, wo_ref[...])
    xo_ref[...] = xm
    h_ref[...] = _rms_mod(xm, g2_ref[...], sh_ref[0], sc_ref[0]).astype(BF16)


def _readout_call(att, yf, yb, bonus, gate, x2, lnx_g, lnx_b, bd, w_out, gt1, g2, sh2, sc2,
                  rows_per_group, tm):
    r, d = x2.shape
    hd = yf.shape[1]
    per_group = rows_per_group // tm
    row = lambda i: (i, 0)
    c2 = lambda i: (0, 0)
    grp = lambda i: (i // per_group, 0, 0)
    half = pl.BlockSpec((tm, hd), row)
    full = pl.BlockSpec((tm, d), row)
    vec = pl.BlockSpec((1, 1, d), grp)
    return pl.pallas_call(
        _readout_kernel,
        grid=(r // tm,),
        in_specs=[half, half, half, half, half, full,
                  pl.BlockSpec((1, hd), c2), pl.BlockSpec((1, hd), c2), pl.BlockSpec(bd.shape, c2),
                  pl.BlockSpec(w_out.shape, c2), vec, pl.BlockSpec((1, d), c2), vec, vec],
        out_specs=[full, full],
        out_shape=[jax.ShapeDtypeStruct((r, d), F32), jax.ShapeDtypeStruct((r, d), BF16)],
        compiler_params=_cparams(("parallel",)),
    )(att, yf, yb, bonus, gate, x2, lnx_g, lnx_b, bd, w_out, gt1, g2, sh2, sc2)


def _mlp_kernel(h_ref, x_ref, w1_ref, w2_ref, gt_ref, gn_ref, shn_ref, scn_ref, xo_ref, ho_ref,
                acc_ref, *, final):
    j = pl.program_id(1)

    @pl.when(j == 0)
    def _():
        acc_ref[...] = jnp.zeros_like(acc_ref)

    a = jnp.maximum(_dot(h_ref[...], w1_ref[...]), 0.0)
    acc_ref[...] += _dot((a * a).astype(BF16), w2_ref[...])

    @pl.when(j == pl.num_programs(1) - 1)
    def _():
        xo = x_ref[...] + gt_ref[0] * acc_ref[...]
        if final:
            ms = jnp.mean(xo * xo, axis=-1, keepdims=True)
            xo_ref[...] = xo * lax.rsqrt(ms + NORM_EPS) * gn_ref[...]
            ho_ref[...] = jnp.zeros_like(ho_ref)
        else:
            xo_ref[...] = xo
            ho_ref[...] = _rms_mod(xo, gn_ref[...], shn_ref[0], scn_ref[0]).astype(BF16)


def _mlp_call(h, x2, w1, w2, gt2, g_next, sh_next, sc_next, rows_per_group, tm, tf, final):
    r, d = x2.shape
    ff = w1.shape[1]
    per_group = rows_per_group // tm
    row = lambda i, j: (i, 0)
    grp = lambda i, j: (i // per_group, 0, 0)
    vec = pl.BlockSpec((1, 1, d), grp)
    return pl.pallas_call(
        functools.partial(_mlp_kernel, final=final),
        grid=(r // tm, ff // tf),
        in_specs=[pl.BlockSpec((tm, d), row), pl.BlockSpec((tm, d), row),
                  pl.BlockSpec((d, tf), lambda i, j: (0, j)), pl.BlockSpec((tf, d), lambda i, j: (j, 0)),
                  vec, pl.BlockSpec((1, d), lambda i, j: (0, 0)), vec, vec],
        out_specs=[pl.BlockSpec((tm, d), row), pl.BlockSpec((tm, d), row)],
        out_shape=[jax.ShapeDtypeStruct((r, d), F32), jax.ShapeDtypeStruct((r, d), BF16)],
        scratch_shapes=[pltpu.VMEM((tm, d), F32)],
        compiler_params=_cparams(("parallel", "arbitrary")),
    )(h, x2, w1, w2, gt2, g_next, sh_next, sc_next)


def _fft1_kernel(h_ref, kr_ref, wc_ref, z_ref):
    l1, nb, d = h_ref.shape[1:]
    rows = l1 * nb
    hf = h_ref[0].reshape(rows, d)
    p = _dot(kr_ref[...], hf)
    gw = d // FOURIER_GROUPS
    wc = wc_ref[...]
    zr, zi = [], []
    for g in range(FOURIER_GROUPS):
        ap = jnp.concatenate([p[:rows, g * gw:(g + 1) * gw], p[rows:, g * gw:(g + 1) * gw]],
                             axis=1).astype(BF16)
        zz = _dot(ap, wc)
        zr.append(zz[:, :gw])
        zi.append(zz[:, gw:])
    z = jnp.concatenate(zr + zi, axis=1).astype(BF16)
    z_ref[0] = z.reshape(l1, nb, 2 * d)


def _fft1_call(h4, kr1, wc, nb):
    b, l1, l2, d = h4.shape
    return pl.pallas_call(
        _fft1_kernel,
        grid=(b, l2 // nb),
        in_specs=[pl.BlockSpec((1, l1, nb, d), lambda bb, j: (bb, 0, j, 0)),
                  pl.BlockSpec(kr1.shape, lambda bb, j: (0, 0)),
                  pl.BlockSpec(wc.shape, lambda bb, j: (0, 0))],
        out_specs=pl.BlockSpec((1, l1, nb, 2 * d), lambda bb, j: (bb, 0, j, 0)),
        out_shape=jax.ShapeDtypeStruct((b, l1, l2, 2 * d), BF16),
        compiler_params=_cparams(("parallel", "parallel")),
    )(h4, kr1, wc)


def _fft2_kernel(z_ref, gk_ref, x_ref, wo_ref, gt_ref, g2_ref, sh_ref, sc_ref, xo_ref, h_ref):
    mb, l2, d2 = z_ref.shape[1:]
    m2b = x_ref.shape[1]
    d = d2 // 2
    z = z_ref[0].reshape(mb * l2, d2)
    rhs = jnp.concatenate([z[:, :d], z[:, d:]], axis=0)
    f = _dot(gk_ref[0], rhs)
    y = _dot(f.astype(BF16), wo_ref[...])
    xm = x_ref[0].reshape(m2b * mb, d) + gt_ref[0] * y
    xo_ref[0] = xm.reshape(m2b, mb, d)
    h_ref[0] = _rms_mod(xm, g2_ref[...], sh_ref[0], sc_ref[0]).astype(BF16).reshape(m2b, mb, d)


def _fft2_call(z4, gk, x4, w_out, gt1, g2, sh2, sc2, mb, m2b):
    b, l1, l2, d2 = z4.shape
    d = d2 // 2
    nblk = l1 // mb
    xspec = pl.BlockSpec((1, m2b, mb, d), lambda m, bb, h: (bb, h, m, 0))
    vec = pl.BlockSpec((1, 1, d), lambda m, bb, h: (bb, 0, 0))
    return pl.pallas_call(
        _fft2_kernel,
        grid=(nblk, b, l2 // m2b),
        in_specs=[pl.BlockSpec((1, mb, l2, d2), lambda m, bb, h: (bb, m, 0, 0)),
                  pl.BlockSpec((1, m2b * mb, gk.shape[2]), lambda m, bb, h: (m, h, 0)),
                  xspec,
                  pl.BlockSpec(w_out.shape, lambda m, bb, h: (0, 0)),
                  vec, pl.BlockSpec((1, d), lambda m, bb, h: (0, 0)), vec, vec],
        out_specs=[xspec, xspec],
        out_shape=[jax.ShapeDtypeStruct((b, l2, l1, d), F32), jax.ShapeDtypeStruct((b, l2, l1, d), BF16)],
        compiler_params=_cparams(("parallel", "parallel", "parallel")),
    )(z4, gk, x4, w_out, gt1, g2, sh2, sc2)


def _rope_tables(t):
    axis_dim = HEAD_DIM // 2
    rows = t // GRID_W
    row = jnp.broadcast_to(jnp.arange(rows, dtype=F32)[:, None], (rows, GRID_W)).reshape(t)
    col = jnp.broadcast_to(jnp.arange(GRID_W, dtype=F32)[None, :], (rows, GRID_W)).reshape(t)
    inv = ROPE_BASE ** (-jnp.arange(0, axis_dim, 2, dtype=F32) / axis_dim)
    ang_r, ang_c = row[:, None] * inv, col[:, None] * inv
    cos = jnp.concatenate([jnp.cos(ang_r), jnp.cos(ang_r), jnp.cos(ang_c), jnp.cos(ang_c)], axis=1)
    sin = jnp.concatenate([-jnp.sin(ang_r), jnp.sin(ang_r), -jnp.sin(ang_c), jnp.sin(ang_c)], axis=1)
    return jnp.tile(cos, (1, 2)), jnp.tile(sin, (1, 2))


def _fft_tables(t, gw, nb, mb):
    l1 = t // FFT_L2
    n1 = np.arange(l1)
    ang1 = 2.0 * np.pi * np.outer(n1, n1) / l1
    eye = np.eye(nb)
    kr1 = np.concatenate([np.kron(np.cos(ang1), eye), np.kron(np.sin(ang1), eye)], axis=0)
    ch = np.arange(gw)
    angc = 2.0 * np.pi * np.outer(ch, ch) / gw
    cg, sg = np.cos(angc), np.sin(angc)
    wc = np.block([[cg, -sg], [-sg, -cg]])
    scale = 1.0 / np.sqrt(float(t) * gw)
    m = np.arange(t)
    n2 = np.arange(FFT_L2)
    theta = 2.0 * np.pi * np.outer(m, n2) / t
    cs = np.stack([np.cos(theta), np.sin(theta)], axis=0) * scale
    cs = cs.reshape(2, FFT_L2, l1 // mb, mb, FFT_L2)
    return (jnp.asarray(kr1, F32).astype(BF16), jnp.asarray(wc, F32).astype(BF16),
            jnp.asarray(cs, F32))


def _expand_gk(cs, mb):
    eye = jnp.eye(mb, dtype=F32)
    g = jnp.einsum("rmbpn,pq->bmprqn", cs, eye)
    nblk = cs.shape[2]
    return g.reshape(nblk, FFT_L2 * mb, 2 * mb * FFT_L2).astype(BF16)


def kernel(x, c, ctx, c_ctx, ada_w, ada_b, norm1_g, norm2_g, mix_w_in, mix_w_out, attn_sink,
           shift_mu_prev, shift_mu_next, decay_w0, decay_w2, iclr_a0, iclr_a2, gate_g2, key_kk,
           key_ka, bonus_rk, lnx_g, lnx_b, fourier_w_out, mlp_w1, mlp_w2, final_g):
    b, t, d = x.shape
    nctx = ctx.shape[1]
    hd = key_kk.shape[1]
    q_dim = d - hd
    n_heads = q_dim // HEAD_DIM
    kv_dim = (n_heads // 4) * HEAD_DIM
    att_cols = q_dim + 2 * kv_dim

    cond = jnp.zeros((8, d), F32).at[:b].set(c).at[b].set(c_ctx)
    mods = _ada_call(cond, ada_w, ada_b)
    lat = [mods[i, :b].reshape(b, N_MOD, 1, d) for i in range(2)]
    cmod = [mods[i, b:b + 1].reshape(1, N_MOD, 1, d) for i in range(2)]
    lm = lambda i, k: lat[i][:, k]
    cm = lambda i, k: cmod[i][:, k]
    row1 = lambda a: a.reshape(1, -1)

    w_in = mix_w_in[0]
    wk = w_in[:, q_dim:q_dim + kv_dim].reshape(d, kv_dim // HEAD_DIM, 1, HEAD_DIM)
    wv = w_in[:, q_dim + kv_dim:att_cols].reshape(d, kv_dim // HEAD_DIM, 1, HEAD_DIM)
    dup = lambda w: jnp.broadcast_to(w, (d, kv_dim // HEAD_DIM, 2, HEAD_DIM)).reshape(d, 2 * kv_dim)
    w_att = jnp.concatenate([w_in[:, :q_dim], dup(wk), dup(wv)], axis=1).astype(BF16)
    w_rw = w_in[:, att_cols:].astype(BF16)
    cos_t, sin_t = _rope_tables(t)
    cos_c, sin_c = jnp.ones((nctx, LANES), F32), jnp.zeros((nctx, LANES), F32)
    g1 = row1(norm1_g[0])
    x2 = x.reshape(b * t, d)
    ctx2 = ctx.reshape(b * nctx, d)
    tm_in = min(512, t)
    q, kd, vd, zrw = _inproj_call(x2, g1, lm(0, 0), lm(0, 1), cos_t, sin_t, w_att, w_rw, t, tm_in)
    qc, kc, vc, zrwc = _inproj_call(ctx2, g1, cm(0, 0), cm(0, 1), cos_c, sin_c, w_att, w_rw,
                                    b * nctx, nctx)
    q, kd, vd = (a.reshape(b, t, -1) for a in (q, kd, vd))
    qc, kc, vc = (a.reshape(b, nctx, -1) for a in (qc, kc, vc))
    sinkb = jnp.broadcast_to(attn_sink[0][:, None] * LOG2E, (n_heads, LANES)).astype(F32)
    att = _attn_call(q, kd, vd, kc, vc, sinkb, 3)
    att_c = _attn_call(qc, kc, vc, kc, vc, sinkb, 0)

    mu = jnp.stack([shift_mu_prev[0], shift_mu_next[0], 1.0 - shift_mu_prev[0] - shift_mu_next[0]])
    zl = jnp.zeros((DECAY_LORA, hd), F32)
    wa = jnp.stack([jnp.concatenate([jnp.concatenate([decay_w2[0, dd], zl], axis=1),
                                     jnp.concatenate([zl, iclr_a2[0, dd]], axis=1)], axis=0)
                    for dd in range(2)])
    wa = wa.astype(BF16)
    w0a0 = jnp.concatenate([decay_w0[0], iclr_a0[0]], axis=1).reshape(2, 1, 2 * hd)
    seg = np.arange(hd) // RWKV_N
    bd = jnp.asarray(seg[:, None] == seg[None, :], F32).astype(BF16)
    k_k, k_a, r_k = row1(key_kk[0]), row1(key_ka[0]), row1(bonus_rk[0])
    g2w = gate_g2[0].astype(BF16)
    s_zero = jnp.zeros((b, 2, hd // LANES, LANES, LANES), F32)
    yfc, ybc, bonus_c, gate_c, s_ctx = _rwkv_call(zrwc.reshape(b, nctx, -1), s_zero, mu, wa, w0a0,
                                                  k_k, k_a, r_k, g2w, bd)
    yf, yb, bonus, gate, _ = _rwkv_call(zrw.reshape(b, t, -1), s_ctx, mu, wa, w0a0,
                                        k_k, k_a, r_k, g2w, bd)

    w_out = mix_w_out[0].astype(BF16)
    n2g = row1(norm2_g[0])
    flat = lambda a: a.reshape(-1, a.shape[-1])
    xm, h2 = _readout_call(flat(att), flat(yf), flat(yb), flat(bonus), flat(gate), x2,
                           row1(lnx_g[0]), row1(lnx_b[0]), bd, w_out, lm(0, 2), n2g,
                           lm(0, 3), lm(0, 4), t, tm_in)
    xmc, h2c = _readout_call(flat(att_c), flat(yfc), flat(ybc), flat(bonus_c), flat(gate_c), ctx2,
                             row1(lnx_g[0]), row1(lnx_b[0]), bd, w_out, cm(0, 2), n2g,
                             cm(0, 3), cm(0, 4), b * nctx, nctx)
    w1 = mlp_w1[0].astype(BF16)
    w2 = mlp_w2[0].astype(BF16)
    g1n = row1(norm1_g[1])
    tm_mlp = min(1024, t)
    x1, h1 = _mlp_call(h2, xm, w1, w2, lm(0, 5), g1n, lm(1, 0), lm(1, 1), t, tm_mlp, MLP_TF, False)
    ctx1, _ = _mlp_call(h2c, xmc, w1, w2, cm(0, 5), g1n, cm(1, 0), cm(1, 1), b * nctx, nctx, MLP_TF,
                        False)
    del ctx1

    l1 = t // FFT_L2
    nb = 8
    mb = min(8, l1)
    gw = d // FOURIER_GROUPS
    kr1, wc, cs = _fft_tables(t, gw, nb, mb)
    gk = _expand_gk(cs, mb)
    z4 = _fft1_call(h1.reshape(b, l1, FFT_L2, d), kr1, wc, nb)
    xm4, h24 = _fft2_call(z4, gk, x1.reshape(b, FFT_L2, l1, d), fourier_w_out[0].astype(BF16),
                          lm(1, 2), row1(norm2_g[1]), lm(1, 3), lm(1, 4), mb, 64)
    out, _ = _mlp_call(h24.reshape(b * t, d), xm4.reshape(b * t, d), mlp_w1[1].astype(BF16),
                       mlp_w2[1].astype(BF16), lm(1, 5), row1(final_g), lm(1, 0), lm(1, 1),
                       t, tm_mlp, MLP_TF, True)
    return out.reshape(b, t, d)
```

```python
import functools

import numpy as np
import jax
import jax.numpy as jnp
from jax import lax
from jax.experimental import pallas as pl
from jax.experimental.pallas import tpu as pltpu

F32 = jnp.float32
BF16 = jnp.bfloat16
HIGHEST = lax.Precision.HIGHEST

HEAD_DIM = 64
WINDOW = 128
QBLK = 128
GRID_W = 64
ROPE_BASE = 10000.0
RWKV_N = 64
DECAY_LORA = 64
ICLR_LORA = 64
GATE_LORA = 128
FOURIER_GROUPS = 4
N_MOD = 6
NORM_EPS = 1e-6
GN_EPS = 64e-5
NEG_INF = -1e30

CHUNK = 64
RWKV_PREP_ROWS = 256
RWKV_BLOCK_CHUNKS = 4
LANES = 128
FFT_L2 = 128
MLP_TF = 1024
VMEM_LIMIT = 48 * 1024 * 1024
LOG2E = 1.4426950408889634
Q_SCALE = HEAD_DIM ** -0.5 * LOG2E


def _cparams(sem):
    return pltpu.CompilerParams(dimension_semantics=sem, vmem_limit_bytes=VMEM_LIMIT)


def _dot(a, b, **kw):
    return jnp.dot(a, b, preferred_element_type=F32, **kw)


def _dot_nt(a, b):
    return lax.dot_general(a, b, (((1,), (1,)), ((), ())), preferred_element_type=F32)


def _dot_tn(a, b):
    return lax.dot_general(a, b, (((0,), (0,)), ((), ())), preferred_element_type=F32)


def _split_dot(x, m_bf16, passes):
    acc = None
    rem = x
    for _ in range(passes):
        piece = rem.astype(BF16)
        term = _dot(piece, m_bf16)
        acc = term if acc is None else acc + term
        rem = rem - piece.astype(F32)
    return acc


def _rms_mod(x, g, sh, sc):
    ms = jnp.mean(x * x, axis=-1, keepdims=True)
    return (x * lax.rsqrt(ms + NORM_EPS)) * (g * (1.0 + sc)) + sh


def _ada_kernel(cond_ref, w_ref, b_ref, o_ref):
    s = cond_ref[...]
    s = s * jax.nn.sigmoid(s)
    o_ref[0] = _dot(s, w_ref[0], precision=HIGHEST) + b_ref[0]


def _ada_call(cond, ada_w, ada_b):
    depth, d, n = ada_w.shape
    tn = 1536
    return pl.pallas_call(
        _ada_kernel,
        grid=(depth, n // tn),
        in_specs=[pl.BlockSpec((8, d), lambda i, j: (0, 0)),
                  pl.BlockSpec((1, d, tn), lambda i, j: (i, 0, j)),
                  pl.BlockSpec((1, 1, tn), lambda i, j: (i, 0, j))],
        out_specs=pl.BlockSpec((1, 8, tn), lambda i, j: (i, 0, j)),
        out_shape=jax.ShapeDtypeStruct((depth, 8, n), F32),
        compiler_params=_cparams(("parallel", "parallel")),
    )(cond, ada_w, ada_b.reshape(depth, 1, n))


def _inproj_kernel(x_ref, g_ref, sh_ref, sc_ref, cos_ref, sin_ref, wa_ref, wr_ref,
                   q_ref, k_ref, v_ref, z_ref):
    h = _rms_mod(x_ref[...], g_ref[...], sh_ref[0], sc_ref[0]).astype(BF16)
    z_ref[...] = _dot(h, wr_ref[...])
    za = _dot(h, wa_ref[...])
    cos = cos_ref[...]
    sin = sin_ref[...]
    lane = lax.broadcasted_iota(jnp.int32, cos.shape, 1)
    first = (lane % 32) < 16
    nq = q_ref.shape[1] // LANES
    nk = k_ref.shape[1] // LANES
    for c in range(nq + nk):
        s = za[:, c * LANES:(c + 1) * LANES]
        partner = jnp.where(first, pltpu.roll(s, LANES - 16, 1), pltpu.roll(s, 16, 1))
        ro = s * cos + partner * sin
        if c < nq:
            q_ref[:, c * LANES:(c + 1) * LANES] = (ro * Q_SCALE).astype(BF16)
        else:
            k_ref[:, (c - nq) * LANES:(c - nq + 1) * LANES] = ro.astype(BF16)
    v_ref[...] = za[:, (nq + nk) * LANES:].astype(BF16)


def _inproj_call(x2, g1, sh, sc, cos, sin, w_att, w_rw, rows_per_group, tm):
    r, d = x2.shape
    period = cos.shape[0]
    n_per = period // tm
    per_group = rows_per_group // tm
    na = w_att.shape[1]
    nr = w_rw.shape[1]
    nq, nkd = 512, 256
    row = lambda i: (i, 0)
    grp = lambda i: (i // per_group, 0, 0)
    return pl.pallas_call(
        _inproj_kernel,
        grid=(r // tm,),
        in_specs=[pl.BlockSpec((tm, d), row),
                  pl.BlockSpec((1, d), lambda i: (0, 0)),
                  pl.BlockSpec((1, 1, d), grp),
                  pl.BlockSpec((1, 1, d), grp),
                  pl.BlockSpec((tm, LANES), lambda i: (i % n_per, 0)),
                  pl.BlockSpec((tm, LANES), lambda i: (i % n_per, 0)),
                  pl.BlockSpec((d, na), lambda i: (0, 0)),
                  pl.BlockSpec((d, nr), lambda i: (0, 0))],
        out_specs=[pl.BlockSpec((tm, nq), row), pl.BlockSpec((tm, nkd), row),
                   pl.BlockSpec((tm, nkd), row), pl.BlockSpec((tm, nr), row)],
        out_shape=[jax.ShapeDtypeStruct((r, nq), BF16), jax.ShapeDtypeStruct((r, nkd), BF16),
                   jax.ShapeDtypeStruct((r, nkd), BF16), jax.ShapeDtypeStruct((r, nr), F32)],
        compiler_params=_cparams(("parallel",)),
    )(x2, g1, sh, sc, cos, sin, w_att, w_rw)


def _attn_kernel(*refs, n_loc, seq_len):
    q_ref = refs[0]
    k_refs = refs[1:1 + n_loc]
    v_refs = refs[1 + n_loc:1 + 2 * n_loc]
    kc_ref, vc_ref, sink_ref, o_ref = refs[1 + 2 * n_loc:]
    i = pl.program_id(1)
    nb = pl.num_programs(1)
    nctx = kc_ref.shape[1]
    nkeys = n_loc * QBLK + nctx
    gq = 4
    rows = gq * QBLK
    low = lax.broadcasted_iota(jnp.int32, (QBLK, LANES), 1) < HEAD_DIM
    if n_loc:
        rq = lax.broadcasted_iota(jnp.int32, (rows, QBLK), 0) % QBLK
        ck = lax.broadcasted_iota(jnp.int32, (rows, QBLK), 1)
        mask_prev = ck >= rq + jnp.where(i >= 1, 0, QBLK)
        mask_next = ck <= rq - jnp.where(i <= nb - 2, 0, QBLK)
    zero = jnp.zeros((QBLK, LANES), BF16)
    ones = jnp.ones((nkeys, LANES), BF16)
    n_groups = q_ref.shape[2] // (gq * HEAD_DIM)
    scores, vals, sinks = [], [], []
    for g in range(n_groups):
        ksl = slice(g * LANES, (g + 1) * LANES)
        keys = jnp.concatenate([kr[0, :, ksl] for kr in k_refs] + [kc_ref[0, :, ksl]], axis=0)
        vals.append(jnp.concatenate(
            [jnp.concatenate([vr[0, :, ksl] for vr in v_refs] + [vc_ref[0, :, ksl]], axis=0), ones], axis=1))
        qs = []
        for pp in range(2):
            qp = q_ref[0, :, (2 * g + pp) * LANES:(2 * g + pp + 1) * LANES]
            qs += [jnp.where(low, qp, zero), jnp.where(low, zero, qp)]
        scores.append(_dot_nt(jnp.concatenate(qs, axis=0), keys))
        sinks.append(jnp.concatenate(
            [jnp.broadcast_to(sink_ref[gq * g + h:gq * g + h + 1, 0:1], (QBLK, 1)) for h in range(gq)], axis=0))
    probs, ms = [], []
    for g in range(n_groups):
        s = scores[g]
        if n_loc:
            s = jnp.concatenate([jnp.where(mask_prev, s[:, :QBLK], NEG_INF), s[:, QBLK:2 * QBLK],
                                 jnp.where(mask_next, s[:, 2 * QBLK:3 * QBLK], NEG_INF), s[:, 3 * QBLK:]],
                                axis=1)
        m = jnp.maximum(jnp.max(s, axis=-1, keepdims=True), sinks[g])
        ms.append(m)
        probs.append(jnp.exp2(s - m).astype(BF16))
    for g in range(n_groups):
        o = _dot(probs[g], vals[g])
        out = o[:, :LANES] / (o[:, LANES:] + jnp.exp2(sinks[g] - ms[g]))
        for pp in range(2):
            even = out[2 * pp * QBLK:(2 * pp + 1) * QBLK]
            odd = out[(2 * pp + 1) * QBLK:(2 * pp + 2) * QBLK]
            p = 2 * g + pp
            o_ref[0, :, p * LANES:(p + 1) * LANES] = jnp.where(low, even, odd).astype(BF16)


def _attn_call(q, kd, vd, kc, vc, sinkb, n_loc):
    b, t, nq = q.shape
    nb = t // QBLK
    nctx = kc.shape[1]
    kw = kd.shape[2]
    qspec = pl.BlockSpec((1, QBLK, nq), lambda bb, i: (bb, i, 0))
    loc = []
    for off in (-1, 0, 1)[:n_loc]:
        loc.append(pl.BlockSpec((1, QBLK, kw), functools.partial(
            lambda bb, i, off: (bb, jnp.clip(i + off, 0, nb - 1), 0), off=off)))
    cspec = pl.BlockSpec((1, nctx, kw), lambda bb, i: (bb, 0, 0))
    args = [q] + [kd] * n_loc + [vd] * n_loc + [kc, vc, sinkb]
    return pl.pallas_call(
        functools.partial(_attn_kernel, n_loc=n_loc, seq_len=t),
        grid=(b, nb),
        in_specs=[qspec] + loc + loc + [cspec, cspec, pl.BlockSpec(sinkb.shape, lambda bb, i: (0, 0))],
        out_specs=qspec,
        out_shape=jax.ShapeDtypeStruct((b, t, nq), BF16),
        compiler_params=_cparams(("parallel", "parallel")),
    )(*args)


def _rwkv_prep_kernel(z_ref, zp_ref, zn_ref, mu_ref, wa_ref, w0a0_ref, kk_ref, ka_ref, rk_ref,
                      g2_ref, bd_ref, tri_ref, opf_ref, opb_ref, v_ref, pc_ref, bonus_ref, gate_ref):
    j = pl.program_id(1)
    nblk = pl.num_programs(1)
    z = z_ref[0]
    nrows = z.shape[0]
    c = CHUNK
    hd = kk_ref.shape[1]
    mu = mu_ref[...]
    row8 = lax.broadcasted_iota(jnp.int32, (8, z.shape[1]), 0)
    zp = pltpu.roll(z, 1, 0)
    zp = jnp.concatenate([jnp.where(row8 == 0, jnp.where(j == 0, 0.0, zp_ref[0, 7:8]), zp[:8]), zp[8:]],
                         axis=0)
    zn = pltpu.roll(z, nrows - 1, 0)
    zn = jnp.concatenate([zn[:nrows - 8],
                          jnp.where(row8 == 7, jnp.where(j == nblk - 1, 0.0, zn_ref[0, 0:1]), zn[nrows - 8:])],
                         axis=0)
    zs = mu[2:3] * z + mu[0:1] * zp + mu[1:2] * zn
    r = zs[:, 0:hd]
    k = zs[:, hd:2 * hd]
    v = zs[:, 2 * hd:3 * hd]
    wa_in = zs[:, 3 * hd:3 * hd + LANES]
    gl = zs[:, 3 * hd + LANES:]
    bd = bd_ref[...]
    v_ref[0] = v.astype(BF16)
    bonus_ref[0] = _split_dot(r * k * rk_ref[...], bd, 1) * v
    gate_ref[0] = _dot(jax.nn.sigmoid(gl).astype(BF16), g2_ref[...])
    low_r = lax.broadcasted_iota(jnp.int32, (nrows, LANES), 1) < DECAY_LORA
    tw = jnp.where(low_r, jnp.tanh(wa_in), wa_in)
    xwa2 = _split_dot(tw, wa_ref[...], 2) + w0a0_ref[...]
    kkr = k * kk_ref[...]
    kk = kkr * lax.rsqrt(_split_dot(kkr * kkr, bd, 1) + 1e-12)
    nsub = nrows // c
    for d, op_ref in enumerate((opf_ref, opb_ref)):
        xw = xwa2[:, 2 * d * hd:(2 * d + 1) * hd]
        a = jax.nn.sigmoid(xwa2[:, (2 * d + 1) * hd:(2 * d + 2) * hd])
        w_log = jnp.minimum(xw, 0.0) - jnp.log(1.0 + jnp.exp(-jnp.abs(xw))) - 0.5
        lw = -jnp.exp(w_log)
        kd = k * (1.0 + (a - 1.0) * ka_ref[...])
        bb = kk * a
        cum = _split_dot_left(tri_ref[d], lw, 2)
        e_in = jnp.exp(cum)
        e_ex = jnp.exp(cum - lw)
        e_neg = jnp.exp(-cum)
        op_ref[0] = jnp.concatenate([kk * e_ex, r * e_in, bb * e_neg, kd * e_neg], axis=1).astype(BF16)
        for s in range(nsub):
            total = cum[s * c:s * c + 1] if d else cum[s * c + c - 1:s * c + c]
            pc_ref[0, 8 * s:8 * s + 8, d * hd:(d + 1) * hd] = jnp.broadcast_to(jnp.exp(total), (8, hd))


def _rwkv_prep_call(z, mu, wa2, w0a02, k_k, k_a, r_k, g2, bd):
    b, t, nz = z.shape
    rows = min(RWKV_PREP_ROWS, t)
    nsub = rows // CHUNK
    ti = np.arange(CHUNK)
    tri = np.stack([np.kron(np.eye(nsub), ti[None, :] <= ti[:, None]),
                    np.kron(np.eye(nsub), ti[None, :] >= ti[:, None])]).astype(np.float32)
    tri = jnp.asarray(tri, F32).astype(BF16)
    hd = k_k.shape[1]
    nblk = t // rows
    cb = rows // 8
    nb8 = t // 8
    blk = lambda bb, j: (bb, j, 0)
    const2 = lambda bb, j: (0, 0)
    const3 = lambda bb, j: (0, 0, 0)
    full = lambda a: pl.BlockSpec(a.shape, const2 if a.ndim == 2 else const3)
    return pl.pallas_call(
        _rwkv_prep_kernel,
        grid=(b, nblk),
        in_specs=[pl.BlockSpec((1, rows, nz), blk),
                  pl.BlockSpec((1, 8, nz), lambda bb, j: (bb, jnp.maximum(j * cb - 1, 0), 0)),
                  pl.BlockSpec((1, 8, nz), lambda bb, j: (bb, jnp.minimum((j + 1) * cb, nb8 - 1), 0)),
                  full(mu), full(wa2), full(w0a02), full(k_k), full(k_a), full(r_k), full(g2), full(bd),
                  full(tri)],
        out_specs=[pl.BlockSpec((1, rows, 4 * hd), blk), pl.BlockSpec((1, rows, 4 * hd), blk),
                   pl.BlockSpec((1, rows, hd), blk), pl.BlockSpec((1, 8 * nsub, 2 * hd), blk),
                   pl.BlockSpec((1, rows, hd), blk), pl.BlockSpec((1, rows, hd), blk)],
        out_shape=[jax.ShapeDtypeStruct((b, t, 4 * hd), BF16), jax.ShapeDtypeStruct((b, t, 4 * hd), BF16),
                   jax.ShapeDtypeStruct((b, t, hd), BF16),
                   jax.ShapeDtypeStruct((b, 8 * (t // CHUNK), 2 * hd), F32),
                   jax.ShapeDtypeStruct((b, t, hd), F32), jax.ShapeDtypeStruct((b, t, hd), F32)],
        compiler_params=_cparams(("parallel", "parallel")),
    )(z, z, z, mu, wa2, w0a02, k_k, k_a, r_k, g2, bd, tri)


def _rwkv_chains(op_ref, v_ref, pc_ref, keep, d):
    c = CHUNK
    hd = v_ref.shape[2]
    nsub = op_ref.shape[1] // c
    low = lax.broadcasted_iota(jnp.int32, (c, LANES), 1) < RWKV_N

    def stack(xp):
        zero = jnp.zeros_like(xp)
        return jnp.concatenate([jnp.where(low, xp, zero), jnp.where(low, zero, xp)], axis=0)

    chains = []
    for s in range(nsub):
        rows = slice(s * c, (s + 1) * c)
        for p in range(hd // LANES):
            sb = lambda i: stack(op_ref[0, rows, i * hd + p * LANES:i * hd + (p + 1) * LANES])
            chains.append(dict(
                d=d, p=p, s=s, keep=keep,
                pc=pc_ref[0, 8 * s:8 * s + 1, d * hd + p * LANES:d * hd + (p + 1) * LANES],
                kts=sb(0), rtb=sb(1), bts=sb(2), kdts=sb(3),
                vs=stack(v_ref[0, rows, p * LANES:(p + 1) * LANES])))
    return chains


def _rwkv_solve(chains, ioff, state_ref, c):
    c2 = 2 * c
    bf = lambda a: a.astype(BF16)
    rr = lax.broadcasted_iota(jnp.int32, (c2, 2 * c2), 0)
    cc = lax.broadcasted_iota(jnp.int32, (c2, 2 * c2), 1)
    diag = (rr // c) == ((cc // c) % 2)

    def blockdiag(ab):
        n = ab.shape[1] // 2
        z = jnp.zeros((ab.shape[0], n), ab.dtype)
        return jnp.concatenate([jnp.concatenate([ab[:, :n], z], axis=1),
                                jnp.concatenate([z, ab[:, n:]], axis=1)], axis=0)

    def swap_halves(g):
        return jnp.concatenate([g[c:], g[:c]], axis=0)

    by_key = {(ch["d"], ch["s"], ch["p"]): ch for ch in chains}
    supers = [(by_key[(d, s, p)], by_key[(d, s, p + 1)])
              for (d, s, p) in sorted(by_key) if p % 2 == 0]
    sup = [dict(c0=a, c1=b, d=a["d"], s=a["s"], q=a["p"] // 2) for a, b in supers]
    for ch in chains:
        ch["kr"] = jnp.concatenate([ch["kts"], ch["rtb"]], axis=0)
        ch["bk"] = jnp.concatenate([ch["bts"], ch["kdts"]], axis=0)
        aa = _dot_nt(ch["kr"], ch["bk"])
        aa = aa * ch["keep"]
        ch["auk"] = aa[:c2, c2:]
        ch["arr"] = bf(aa[c2:, :])
        ch["b0"] = ioff - aa[:c2, :c2]
    for sc in sup:
        sc["b"] = jnp.concatenate([sc["c0"]["b0"], sc["c1"]["b0"]], axis=1)
        sc["kr"] = jnp.concatenate([sc["c0"]["kr"], sc["c1"]["kr"]], axis=1)
    for _ in range(6):
        for sc in sup:
            b = sc["b"]
            xbd = jnp.where(diag, b, 0.0)
            sc["b"] = _dot(bf(xbd), blockdiag(bf(b))) + (b - xbd)
    for sc in sup:
        auk = jnp.concatenate([sc["c0"]["auk"], sc["c1"]["auk"]], axis=1)
        vs2 = jnp.concatenate([sc["c0"]["vs"], sc["c1"]["vs"]], axis=1)
        sc["av"] = _dot(bf(auk), blockdiag(vs2))
    for sc in sup:
        sc["tsw"] = bf(jnp.where(diag, 0.0, sc["b"]))
    groups = {}
    for sc in sup:
        groups.setdefault((sc["d"], sc["q"]), []).append(sc)
    state = {(d, p): state_ref[0, d, p] for (d, q) in groups for p in (2 * q, 2 * q + 1)}
    nsub = len(next(iter(groups.values())))
    ys = {}
    for step in range(nsub):
        cur = {key: sorted(g, key=lambda sc: sc["s"], reverse=bool(key[0]))[step] for key, g in groups.items()}
        for (d, q), sc in cur.items():
            s2 = jnp.concatenate([bf(state[(d, 2 * q)]), bf(state[(d, 2 * q + 1)])], axis=1)
            sc["ksrs"] = _dot_nt(sc["kr"], blockdiag(s2))
        for (d, q), sc in cur.items():
            g = sc["av"] + sc["ksrs"][:c2]
            sc["ub"] = bf(-_dot(sc["tsw"], blockdiag(bf(swap_halves(g)))))
        for (d, q), sc in cur.items():
            for i, ch in enumerate((sc["c0"], sc["c1"])):
                key = (d, 2 * q + i)
                uv = jnp.concatenate([sc["ub"][:, i * c2:(i + 1) * c2], ch["vs"]], axis=0)
                y = sc["ksrs"][c2:, i * c2:(i + 1) * c2] + _dot(ch["arr"], uv)
                state[key] = (state[key] + _dot_tn(uv, ch["bk"])) * ch["pc"]
                ys[(d, sc["s"], key[1])] = y[:c] + y[c:]
    for key, s_new in state.items():
        state_ref[0, key[0], key[1]] = s_new
    dirs = sorted({d for d, _ in state})
    n_pairs = len(state) // len(dirs)
    return [jnp.concatenate([jnp.concatenate([ys[(d, s, p)] for p in range(n_pairs)], axis=1)
                             for s in range(nsub)], axis=0) for d in dirs]


def _split_dot_left(m_bf16, x, passes):
    acc = None
    rem = x
    for _ in range(passes):
        piece = rem.astype(BF16)
        term = _dot(m_bf16, piece)
        acc = term if acc is None else acc + term
        rem = rem - piece.astype(F32)
    return acc


def _rwkv_solve_kernel(opf_ref, opb_ref, vf_ref, vb_ref, pcf_ref, pcb_ref, s0_ref, keep_ref, ioff_ref,
                       yf_ref, yb_ref, state_ref):
    @pl.when(pl.program_id(1) == 0)
    def _():
        state_ref[...] = s0_ref[...]

    chains = (_rwkv_chains(opf_ref, vf_ref, pcf_ref, keep_ref[0], 0)
              + _rwkv_chains(opb_ref, vb_ref, pcb_ref, keep_ref[1], 1))
    yf_ref[0], yb_ref[0] = _rwkv_solve(chains, ioff_ref[...], state_ref, CHUNK)


def _rwkv_masks(c):
    tt = (np.arange(4 * c) % c)[:, None]
    ss = (np.arange(4 * c) % c)[None, :]
    incl = (np.arange(4 * c) >= 2 * c)[:, None]
    keep = np.stack([np.where(incl, ss <= tt, ss < tt), np.where(incl, ss >= tt, ss > tt)])
    ioff = np.kron(np.array([[0.0, 1.0], [1.0, 0.0]]), np.eye(c))
    return jnp.asarray(keep.astype(np.float32)), jnp.asarray(ioff, F32)


def _rwkv_solve_call(opf, opb, vb, pcs, s0):
    b, t, hd = vb.shape
    rows = RWKV_BLOCK_CHUNKS * CHUNK
    keep, ioff = _rwkv_masks(CHUNK)
    nblk = t // rows
    fwd = lambda bb, j: (bb, j, 0)
    bwd = lambda bb, j: (bb, nblk - 1 - j, 0)
    st = pl.BlockSpec((1,) + s0.shape[1:], lambda bb, j: (bb, 0, 0, 0, 0))
    spec = lambda width, idx: pl.BlockSpec((1, rows, width), idx)
    pcspec = lambda idx: pl.BlockSpec((1, 8 * RWKV_BLOCK_CHUNKS, 2 * hd), idx)
    return pl.pallas_call(
        _rwkv_solve_kernel,
        grid=(b, nblk),
        in_specs=[spec(4 * hd, fwd), spec(4 * hd, bwd), spec(hd, fwd), spec(hd, bwd),
                  pcspec(fwd), pcspec(bwd), st,
                  pl.BlockSpec(keep.shape, lambda bb, j: (0, 0, 0)),
                  pl.BlockSpec(ioff.shape, lambda bb, j: (0, 0))],
        out_specs=[spec(hd, fwd), spec(hd, bwd), st],
        out_shape=[jax.ShapeDtypeStruct((b, t, hd), F32)] * 2 + [jax.ShapeDtypeStruct(s0.shape, F32)],
        compiler_params=_cparams(("parallel", "arbitrary")),
    )(opf, opb, vb, vb, pcs, pcs, s0, keep, ioff)


def _readout_kernel(att_ref, yf_ref, yb_ref, bonus_ref, gate_ref, x_ref, lg_ref, lb_ref, bd_ref,
                    wo_ref, gt_ref, g2_ref, sh_ref, sc_ref, xo_ref, h_ref):
    bd = bd_ref[...]
    inv_n = 1.0 / RWKV_N
    y = yf_ref[...] + yb_ref[...]
    tm = y.shape[0]
    ysq = y * y
    y_hi = y.astype(BF16)
    q_hi = ysq.astype(BF16)
    parts = jnp.concatenate([y_hi, (y - y_hi.astype(F32)).astype(BF16),
                             q_hi, (ysq - q_hi.astype(F32)).astype(BF16)], axis=0)
    st = _dot(parts, bd) * inv_n
    mean = st[:tm] + st[tm:2 * tm]
    var = st[2 * tm:3 * tm] + st[3 * tm:] - mean * mean
    yn = (y - mean) * lax.rsqrt(var + GN_EPS) * lg_ref[...] + lb_ref[...]
    rw = (yn + bonus_ref[...]) * gate_ref[...]
    cat = jnp.concatenate([att_ref[...], rw.astype(BF16)], axis=1)
    xm = x_ref[...] + gt_ref[0] * _dot(cat, wo_ref[...])
    xo_ref[...] = xm
    h_ref[...] = _rms_mod(xm, g2_ref[...], sh_ref[0], sc_ref[0]).astype(BF16)


def _readout_call(att, yf, yb, bonus, gate, x2, lnx_g, lnx_b, bd, w_out, gt1, g2, sh2, sc2,
                  rows_per_group, tm):
    r, d = x2.shape
    hd = yf.shape[1]
    per_group = rows_per_group // tm
    row = lambda i: (i, 0)
    c2 = lambda i: (0, 0)
    grp = lambda i: (i // per_group, 0, 0)
    half = pl.BlockSpec((tm, hd), row)
    full = pl.BlockSpec((tm, d), row)
    vec = pl.BlockSpec((1, 1, d), grp)
    return pl.pallas_call(
        _readout_kernel,
        grid=(r // tm,),
        in_specs=[half, half, half, half, half, full,
                  pl.BlockSpec((1, hd), c2), pl.BlockSpec((1, hd), c2), pl.BlockSpec(bd.shape, c2),
                  pl.BlockSpec(w_out.shape, c2), vec, pl.BlockSpec((1, d), c2), vec, vec],
        out_specs=[full, full],
        out_shape=[jax.ShapeDtypeStruct((r, d), F32), jax.ShapeDtypeStruct((r, d), BF16)],
        compiler_params=_cparams(("parallel",)),
    )(att, yf, yb, bonus, gate, x2, lnx_g, lnx_b, bd, w_out, gt1, g2, sh2, sc2)


def _mlp_kernel(h_ref, x_ref, w1_ref, w2_ref, gt_ref, gn_ref, shn_ref, scn_ref, xo_ref, ho_ref,
                acc_ref, *, final):
    j = pl.program_id(1)

    @pl.when(j == 0)
    def _():
        acc_ref[...] = jnp.zeros_like(acc_ref)

    a = jnp.maximum(_dot(h_ref[...], w1_ref[...]), 0.0)
    acc_ref[...] += _dot((a * a).astype(BF16), w2_ref[...])

    @pl.when(j == pl.num_programs(1) - 1)
    def _():
        xo = x_ref[...] + gt_ref[0] * acc_ref[...]
        if final:
            ms = jnp.mean(xo * xo, axis=-1, keepdims=True)
            xo_ref[...] = xo * lax.rsqrt(ms + NORM_EPS) * gn_ref[...]
            ho_ref[...] = jnp.zeros_like(ho_ref)
        else:
            xo_ref[...] = xo
            ho_ref[...] = _rms_mod(xo, gn_ref[...], shn_ref[0], scn_ref[0]).astype(BF16)


def _mlp_call(h, x2, w1, w2, gt2, g_next, sh_next, sc_next, rows_per_group, tm, tf, final):
    r, d = x2.shape
    ff = w1.shape[1]
    per_group = rows_per_group // tm
    row = lambda i, j: (i, 0)
    grp = lambda i, j: (i // per_group, 0, 0)
    vec = pl.BlockSpec((1, 1, d), grp)
    return pl.pallas_call(
        functools.partial(_mlp_kernel, final=final),
        grid=(r // tm, ff // tf),
        in_specs=[pl.BlockSpec((tm, d), row), pl.BlockSpec((tm, d), row),
                  pl.BlockSpec((d, tf), lambda i, j: (0, j)), pl.BlockSpec((tf, d), lambda i, j: (j, 0)),
                  vec, pl.BlockSpec((1, d), lambda i, j: (0, 0)), vec, vec],
        out_specs=[pl.BlockSpec((tm, d), row), pl.BlockSpec((tm, d), row)],
        out_shape=[jax.ShapeDtypeStruct((r, d), F32), jax.ShapeDtypeStruct((r, d), BF16)],
        scratch_shapes=[pltpu.VMEM((tm, d), F32)],
        compiler_params=_cparams(("parallel", "arbitrary")),
    )(h, x2, w1, w2, gt2, g_next, sh_next, sc_next)


def _fft1_kernel(h_ref, kr_ref, wc_ref, z_ref):
    l1, nb, d = h_ref.shape[1:]
    rows = l1 * nb
    hf = h_ref[0].reshape(rows, d)
    p = _dot(kr_ref[...], hf)
    gw = d // FOURIER_GROUPS
    wc = wc_ref[...]
    zr, zi = [], []
    for g in range(FOURIER_GROUPS):
        ap = jnp.concatenate([p[:rows, g * gw:(g + 1) * gw], p[rows:, g * gw:(g + 1) * gw]],
                             axis=1).astype(BF16)
        zz = _dot(ap, wc)
        zr.append(zz[:, :gw])
        zi.append(zz[:, gw:])
    z = jnp.concatenate(zr + zi, axis=1).astype(BF16)
    z_ref[0] = z.reshape(l1, nb, 2 * d)


def _fft1_call(h4, kr1, wc, nb):
    b, l1, l2, d = h4.shape
    return pl.pallas_call(
        _fft1_kernel,
        grid=(b, l2 // nb),
        in_specs=[pl.BlockSpec((1, l1, nb, d), lambda bb, j: (bb, 0, j, 0)),
                  pl.BlockSpec(kr1.shape, lambda bb, j: (0, 0)),
                  pl.BlockSpec(wc.shape, lambda bb, j: (0, 0))],
        out_specs=pl.BlockSpec((1, l1, nb, 2 * d), lambda bb, j: (bb, 0, j, 0)),
        out_shape=jax.ShapeDtypeStruct((b, l1, l2, 2 * d), BF16),
        compiler_params=_cparams(("parallel", "parallel")),
    )(h4, kr1, wc)


def _fft2_kernel(z_ref, gk_ref, x_ref, wo_ref, gt_ref, g2_ref, sh_ref, sc_ref, xo_ref, h_ref):
    mb, l2, d2 = z_ref.shape[1:]
    m2b = x_ref.shape[1]
    d = d2 // 2
    z = z_ref[0].reshape(mb * l2, d2)
    rhs = jnp.concatenate([z[:, :d], z[:, d:]], axis=0)
    f = _dot(gk_ref[0], rhs)
    y = _dot(f.astype(BF16), wo_ref[...])
    xm = x_ref[0].reshape(m2b * mb, d) + gt_ref[0] * y
    xo_ref[0] = xm.reshape(m2b, mb, d)
    h_ref[0] = _rms_mod(xm, g2_ref[...], sh_ref[0], sc_ref[0]).astype(BF16).reshape(m2b, mb, d)


def _fft2_call(z4, gk, x4, w_out, gt1, g2, sh2, sc2, mb, m2b):
    b, l1, l2, d2 = z4.shape
    d = d2 // 2
    nblk = l1 // mb
    xspec = pl.BlockSpec((1, m2b, mb, d), lambda m, bb, h: (bb, h, m, 0))
    vec = pl.BlockSpec((1, 1, d), lambda m, bb, h: (bb, 0, 0))
    return pl.pallas_call(
        _fft2_kernel,
        grid=(nblk, b, l2 // m2b),
        in_specs=[pl.BlockSpec((1, mb, l2, d2), lambda m, bb, h: (bb, m, 0, 0)),
                  pl.BlockSpec((1, m2b * mb, gk.shape[2]), lambda m, bb, h: (m, h, 0)),
                  xspec,
                  pl.BlockSpec(w_out.shape, lambda m, bb, h: (0, 0)),
                  vec, pl.BlockSpec((1, d), lambda m, bb, h: (0, 0)), vec, vec],
        out_specs=[xspec, xspec],
        out_shape=[jax.ShapeDtypeStruct((b, l2, l1, d), F32), jax.ShapeDtypeStruct((b, l2, l1, d), BF16)],
        compiler_params=_cparams(("parallel", "parallel", "parallel")),
    )(z4, gk, x4, w_out, gt1, g2, sh2, sc2)


def _rope_tables(t):
    axis_dim = HEAD_DIM // 2
    rows = t // GRID_W
    row = jnp.broadcast_to(jnp.arange(rows, dtype=F32)[:, None], (rows, GRID_W)).reshape(t)
    col = jnp.broadcast_to(jnp.arange(GRID_W, dtype=F32)[None, :], (rows, GRID_W)).reshape(t)
    inv = ROPE_BASE ** (-jnp.arange(0, axis_dim, 2, dtype=F32) / axis_dim)
    ang_r, ang_c = row[:, None] * inv, col[:, None] * inv
    cos = jnp.concatenate([jnp.cos(ang_r), jnp.cos(ang_r), jnp.cos(ang_c), jnp.cos(ang_c)], axis=1)
    sin = jnp.concatenate([-jnp.sin(ang_r), jnp.sin(ang_r), -jnp.sin(ang_c), jnp.sin(ang_c)], axis=1)
    return jnp.tile(cos, (1, 2)), jnp.tile(sin, (1, 2))


def _fft_tables(t, gw, nb, mb):
    l1 = t // FFT_L2
    n1 = np.arange(l1)
    ang1 = 2.0 * np.pi * np.outer(n1, n1) / l1
    eye = np.eye(nb)
    kr1 = np.concatenate([np.kron(np.cos(ang1), eye), np.kron(np.sin(ang1), eye)], axis=0)
    ch = np.arange(gw)
    angc = 2.0 * np.pi * np.outer(ch, ch) / gw
    cg, sg = np.cos(angc), np.sin(angc)
    wc = np.block([[cg, -sg], [-sg, -cg]])
    scale = 1.0 / np.sqrt(float(t) * gw)
    m = np.arange(t)
    n2 = np.arange(FFT_L2)
    theta = 2.0 * np.pi * np.outer(m, n2) / t
    cs = np.stack([np.cos(theta), np.sin(theta)], axis=0) * scale
    cs = cs.reshape(2, FFT_L2, l1 // mb, mb, FFT_L2)
    return (jnp.asarray(kr1, F32).astype(BF16), jnp.asarray(wc, F32).astype(BF16),
            jnp.asarray(cs, F32))


def _expand_gk(cs, mb):
    eye = jnp.eye(mb, dtype=F32)
    g = jnp.einsum("rmbpn,pq->bmprqn", cs, eye)
    nblk = cs.shape[2]
    return g.reshape(nblk, FFT_L2 * mb, 2 * mb * FFT_L2).astype(BF16)


def kernel(x, c, ctx, c_ctx, ada_w, ada_b, norm1_g, norm2_g, mix_w_in, mix_w_out, attn_sink,
           shift_mu_prev, shift_mu_next, decay_w0, decay_w2, iclr_a0, iclr_a2, gate_g2, key_kk,
           key_ka, bonus_rk, lnx_g, lnx_b, fourier_w_out, mlp_w1, mlp_w2, final_g):
    b, t, d = x.shape
    nctx = ctx.shape[1]
    hd = key_kk.shape[1]
    q_dim = d - hd
    n_heads = q_dim // HEAD_DIM
    kv_dim = (n_heads // 4) * HEAD_DIM
    att_cols = q_dim + 2 * kv_dim

    cond = jnp.zeros((8, d), F32).at[:b].set(c).at[b].set(c_ctx)
    mods = _ada_call(cond, ada_w, ada_b)
    lat = [mods[i, :b].reshape(b, N_MOD, 1, d) for i in range(2)]
    cmod = [mods[i, b:b + 1].reshape(1, N_MOD, 1, d) for i in range(2)]
    lm = lambda i, k: lat[i][:, k]
    cm = lambda i, k: cmod[i][:, k]
    row1 = lambda a: a.reshape(1, -1)

    w_in = mix_w_in[0]
    wk = w_in[:, q_dim:q_dim + kv_dim].reshape(d, kv_dim // HEAD_DIM, 1, HEAD_DIM)
    wv = w_in[:, q_dim + kv_dim:att_cols].reshape(d, kv_dim // HEAD_DIM, 1, HEAD_DIM)
    dup = lambda w: jnp.broadcast_to(w, (d, kv_dim // HEAD_DIM, 2, HEAD_DIM)).reshape(d, 2 * kv_dim)
    w_att = jnp.concatenate([w_in[:, :q_dim], dup(wk), dup(wv)], axis=1).astype(BF16)
    w_rw = w_in[:, att_cols:].astype(BF16)
    cos_t, sin_t = _rope_tables(t)
    cos_c, sin_c = jnp.ones((nctx, LANES), F32), jnp.zeros((nctx, LANES), F32)
    g1 = row1(norm1_g[0])
    x2 = x.reshape(b * t, d)
    ctx2 = ctx.reshape(b * nctx, d)
    tm_in = min(512, t)
    q, kd, vd, zrw = _inproj_call(x2, g1, lm(0, 0), lm(0, 1), cos_t, sin_t, w_att, w_rw, t, tm_in)
    qc, kc, vc, zrwc = _inproj_call(ctx2, g1, cm(0, 0), cm(0, 1), cos_c, sin_c, w_att, w_rw,
                                    b * nctx, nctx)
    q, kd, vd = (a.reshape(b, t, -1) for a in (q, kd, vd))
    qc, kc, vc = (a.reshape(b, nctx, -1) for a in (qc, kc, vc))
    sinkb = jnp.broadcast_to(attn_sink[0][:, None] * LOG2E, (n_heads, LANES)).astype(F32)
    att = _attn_call(q, kd, vd, kc, vc, sinkb, 3)
    att_c = _attn_call(qc, kc, vc, kc, vc, sinkb, 0)

    mu = jnp.stack([shift_mu_prev[0], shift_mu_next[0], 1.0 - shift_mu_prev[0] - shift_mu_next[0]])
    zl = jnp.zeros((DECAY_LORA, hd), F32)
    wa = jnp.stack([jnp.concatenate([jnp.concatenate([decay_w2[0, dd], zl], axis=1),
                                     jnp.concatenate([zl, iclr_a2[0, dd]], axis=1)], axis=0)
                    for dd in range(2)])
    wa = jnp.concatenate([wa[0], wa[1]], axis=1).astype(BF16)
    w0a0 = jnp.concatenate([decay_w0[0], iclr_a0[0]], axis=1).reshape(1, 4 * hd)
    seg = np.arange(hd) // RWKV_N
    bd = jnp.asarray(seg[:, None] == seg[None, :], F32).astype(BF16)
    k_k, k_a, r_k = row1(key_kk[0]), row1(key_ka[0]), row1(bonus_rk[0])
    g2w = gate_g2[0].astype(BF16)
    s_zero = jnp.zeros((b, 2, hd // LANES, LANES, LANES), F32)
    opf_c, opb_c, vb_c, pcs_c, bonus_c, gate_c = _rwkv_prep_call(zrwc.reshape(b, nctx, -1), mu, wa, w0a0,
                                                                 k_k, k_a, r_k, g2w, bd)
    opf, opb, vb, pcs, bonus, gate = _rwkv_prep_call(zrw.reshape(b, t, -1), mu, wa, w0a0,
                                                     k_k, k_a, r_k, g2w, bd)
    yfc, ybc, s_ctx = _rwkv_solve_call(opf_c, opb_c, vb_c, pcs_c, s_zero)
    yf, yb, _ = _rwkv_solve_call(opf, opb, vb, pcs, s_ctx)

    w_out = mix_w_out[0].astype(BF16)
    n2g = row1(norm2_g[0])
    flat = lambda a: a.reshape(-1, a.shape[-1])
    xm, h2 = _readout_call(flat(att), flat(yf), flat(yb), flat(bonus), flat(gate), x2,
                           row1(lnx_g[0]), row1(lnx_b[0]), bd, w_out, lm(0, 2), n2g,
                           lm(0, 3), lm(0, 4), t, tm_in)
    xmc, h2c = _readout_call(flat(att_c), flat(yfc), flat(ybc), flat(bonus_c), flat(gate_c), ctx2,
                             row1(lnx_g[0]), row1(lnx_b[0]), bd, w_out, cm(0, 2), n2g,
                             cm(0, 3), cm(0, 4), b * nctx, nctx)
    w1 = mlp_w1[0].astype(BF16)
    w2 = mlp_w2[0].astype(BF16)
    g1n = row1(norm1_g[1])
    tm_mlp = min(1024, t)
    x1, h1 = _mlp_call(h2, xm, w1, w2, lm(0, 5), g1n, lm(1, 0), lm(1, 1), t, tm_mlp, MLP_TF, False)
    ctx1, _ = _mlp_call(h2c, xmc, w1, w2, cm(0, 5), g1n, cm(1, 0), cm(1, 1), b * nctx, nctx, MLP_TF,
                        False)
    del ctx1

    l1 = t // FFT_L2
    nb = 8
    mb = min(8, l1)
    gw = d // FOURIER_GROUPS
    kr1, wc, cs = _fft_tables(t, gw, nb, mb)
    gk = _expand_gk(cs, mb)
    z4 = _fft1_call(h1.reshape(b, l1, FFT_L2, d), kr1, wc, nb)
    xm4, h24 = _fft2_call(z4, gk, x1.reshape(b, FFT_L2, l1, d), fourier_w_out[0].astype(BF16),
                          lm(1, 2), row1(norm2_g[1]), lm(1, 3), lm(1, 4), mb, 64)
    out, _ = _mlp_call(h24.reshape(b * t, d), xm4.reshape(b * t, d), mlp_w1[1].astype(BF16),
                       mlp_w2[1].astype(BF16), lm(1, 5), row1(final_g), lm(1, 0), lm(1, 1),
                       t, tm_mlp, MLP_TF, True)
    return out.reshape(b, t, d)
```

```python
import functools

import numpy as np
import jax
import jax.numpy as jnp
from jax import lax
from jax.experimental import pallas as pl
from jax.experimental.pallas import tpu as pltpu

F32 = jnp.float32
BF16 = jnp.bfloat16
HIGHEST = lax.Precision.HIGHEST

HEAD_DIM = 64
WINDOW = 128
QBLK = 128
ATTN_QBLOCKS = 4
GRID_W = 64
ROPE_BASE = 10000.0
RWKV_N = 64
DECAY_LORA = 64
ICLR_LORA = 64
GATE_LORA = 128
FOURIER_GROUPS = 4
N_MOD = 6
NORM_EPS = 1e-6
GN_EPS = 64e-5
NEG_INF = -1e30

CHUNK = 64
RWKV_PREP_ROWS = 256
RWKV_BLOCK_CHUNKS = 4
RWKV_BATCH_ROWS = 1
LANES = 128
FFT_L2 = 128
MLP_TF = 1024
VMEM_LIMIT = 48 * 1024 * 1024
LOG2E = 1.4426950408889634
Q_SCALE = HEAD_DIM ** -0.5 * LOG2E
W_LOG2_OFFSET = -0.5 * LOG2E + float(np.log2(LOG2E))


def _cparams(sem):
    return pltpu.CompilerParams(dimension_semantics=sem, vmem_limit_bytes=VMEM_LIMIT)


def _dot(a, b, **kw):
    return jnp.dot(a, b, preferred_element_type=F32, **kw)


def _dot_nt(a, b):
    return lax.dot_general(a, b, (((1,), (1,)), ((), ())), preferred_element_type=F32)


def _dot_tn(a, b):
    return lax.dot_general(a, b, (((0,), (0,)), ((), ())), preferred_element_type=F32)


def _split_dot(x, m_bf16, passes):
    acc = None
    rem = x
    for _ in range(passes):
        piece = rem.astype(BF16)
        term = _dot(piece, m_bf16)
        acc = term if acc is None else acc + term
        rem = rem - piece.astype(F32)
    return acc


def _rms_mod(x, g, sh, sc):
    ms = jnp.mean(x * x, axis=-1, keepdims=True)
    return (x * lax.rsqrt(ms + NORM_EPS)) * (g * (1.0 + sc)) + sh


def _ada_kernel(cond_ref, w_ref, b_ref, o_ref):
    s = cond_ref[...]
    s = s * jax.nn.sigmoid(s)
    o_ref[0] = _dot(s, w_ref[0], precision=HIGHEST) + b_ref[0]


def _ada_call(cond, ada_w, ada_b):
    depth, d, n = ada_w.shape
    tn = 1536
    return pl.pallas_call(
        _ada_kernel,
        grid=(depth, n // tn),
        in_specs=[pl.BlockSpec((8, d), lambda i, j: (0, 0)),
                  pl.BlockSpec((1, d, tn), lambda i, j: (i, 0, j)),
                  pl.BlockSpec((1, 1, tn), lambda i, j: (i, 0, j))],
        out_specs=pl.BlockSpec((1, 8, tn), lambda i, j: (i, 0, j)),
        out_shape=jax.ShapeDtypeStruct((depth, 8, n), F32),
        compiler_params=_cparams(("parallel", "parallel")),
    )(cond, ada_w, ada_b.reshape(depth, 1, n))


def _inproj_kernel(x_ref, g_ref, sh_ref, sc_ref, cos_ref, sin_ref, wa_ref, wr_ref,
                   q_ref, k_ref, v_ref, z_ref):
    h = _rms_mod(x_ref[...], g_ref[...], sh_ref[0], sc_ref[0]).astype(BF16)
    z_ref[...] = _dot(h, wr_ref[...])
    za = _dot(h, wa_ref[...])
    cos = cos_ref[...]
    sin = sin_ref[...]
    lane = lax.broadcasted_iota(jnp.int32, cos.shape, 1)
    first = (lane % 32) < 16
    nq = q_ref.shape[1] // LANES
    nk = k_ref.shape[1] // LANES
    for c in range(nq + nk):
        s = za[:, c * LANES:(c + 1) * LANES]
        partner = jnp.where(first, pltpu.roll(s, LANES - 16, 1), pltpu.roll(s, 16, 1))
        ro = s * cos + partner * sin
        if c < nq:
            q_ref[:, c * LANES:(c + 1) * LANES] = (ro * Q_SCALE).astype(BF16)
        else:
            k_ref[:, (c - nq) * LANES:(c - nq + 1) * LANES] = ro.astype(BF16)
    v_ref[...] = za[:, (nq + nk) * LANES:].astype(BF16)


def _inproj_call(x2, g1, sh, sc, cos, sin, w_att, w_rw, rows_per_group, tm):
    r, d = x2.shape
    period = cos.shape[0]
    n_per = period // tm
    per_group = rows_per_group // tm
    na = w_att.shape[1]
    nr = w_rw.shape[1]
    nq, nkd = 512, 256
    row = lambda i: (i, 0)
    grp = lambda i: (i // per_group, 0, 0)
    return pl.pallas_call(
        _inproj_kernel,
        grid=(r // tm,),
        in_specs=[pl.BlockSpec((tm, d), row),
                  pl.BlockSpec((1, d), lambda i: (0, 0)),
                  pl.BlockSpec((1, 1, d), grp),
                  pl.BlockSpec((1, 1, d), grp),
                  pl.BlockSpec((tm, LANES), lambda i: (i % n_per, 0)),
                  pl.BlockSpec((tm, LANES), lambda i: (i % n_per, 0)),
                  pl.BlockSpec((d, na), lambda i: (0, 0)),
                  pl.BlockSpec((d, nr), lambda i: (0, 0))],
        out_specs=[pl.BlockSpec((tm, nq), row), pl.BlockSpec((tm, nkd), row),
                   pl.BlockSpec((tm, nkd), row), pl.BlockSpec((tm, nr), row)],
        out_shape=[jax.ShapeDtypeStruct((r, nq), BF16), jax.ShapeDtypeStruct((r, nkd), BF16),
                   jax.ShapeDtypeStruct((r, nkd), BF16), jax.ShapeDtypeStruct((r, nr), F32)],
        compiler_params=_cparams(("parallel",)),
    )(x2, g1, sh, sc, cos, sin, w_att, w_rw)


def _attn_kernel(*refs, local, qpb):
    q_ref = refs[0]
    nk = qpb + 2 if local else 0
    k_refs = refs[1:1 + nk]
    v_refs = refs[1 + nk:1 + 2 * nk]
    kc_ref, vc_ref, sink_ref, o_ref = refs[1 + 2 * nk:]
    i = pl.program_id(1)
    nb = pl.num_programs(1) * qpb
    nctx = kc_ref.shape[1]
    nkeys = (3 * QBLK if local else 0) + nctx
    gq = 4
    rows = gq * QBLK
    low = lax.broadcasted_iota(jnp.int32, (QBLK, LANES), 1) < HEAD_DIM
    if local:
        rq = lax.broadcasted_iota(jnp.int32, (rows, QBLK), 0) % QBLK
        ck = lax.broadcasted_iota(jnp.int32, (rows, QBLK), 1)
    zero = jnp.zeros((QBLK, LANES), BF16)
    ones = jnp.ones((nkeys, LANES), BF16)
    n_groups = q_ref.shape[2] // (gq * HEAD_DIM)
    units = [(qi, g) for qi in range(qpb) for g in range(n_groups)]
    scores, vals, sinks = {}, {}, {}
    for qi, g in units:
        ksl = slice(g * LANES, (g + 1) * LANES)
        keys = jnp.concatenate([kr[0, :, ksl] for kr in k_refs[qi:qi + 3]] + [kc_ref[0, :, ksl]], axis=0)
        vals[qi, g] = jnp.concatenate(
            [jnp.concatenate([vr[0, :, ksl] for vr in v_refs[qi:qi + 3]] + [vc_ref[0, :, ksl]], axis=0), ones],
            axis=1)
        qs = []
        for pp in range(2):
            qp = q_ref[0, qi * QBLK:(qi + 1) * QBLK, (2 * g + pp) * LANES:(2 * g + pp + 1) * LANES]
            qs += [jnp.where(low, qp, zero), jnp.where(low, zero, qp)]
        scores[qi, g] = _dot_nt(jnp.concatenate(qs, axis=0), keys)
        sinks[g] = jnp.concatenate(
            [jnp.broadcast_to(sink_ref[gq * g + h:gq * g + h + 1, 0:1], (QBLK, 1)) for h in range(gq)], axis=0)
    probs, ms = {}, {}
    for qi, g in units:
        s = scores[qi, g]
        if local:
            blk = i * qpb + qi
            mask_prev = ck >= rq + jnp.where(blk >= 1, 0, QBLK)
            mask_next = ck <= rq - jnp.where(blk <= nb - 2, 0, QBLK)
            s = jnp.concatenate([jnp.where(mask_prev, s[:, :QBLK], NEG_INF), s[:, QBLK:2 * QBLK],
                                 jnp.where(mask_next, s[:, 2 * QBLK:3 * QBLK], NEG_INF), s[:, 3 * QBLK:]],
                                axis=1)
        m = jnp.maximum(jnp.max(s, axis=-1, keepdims=True), sinks[g])
        ms[qi, g] = m
        probs[qi, g] = jnp.exp2(s - m).astype(BF16)
    for qi, g in units:
        o = _dot(probs[qi, g], vals[qi, g])
        out = o[:, :LANES] / (o[:, LANES:] + jnp.exp2(sinks[g] - ms[qi, g]))
        for pp in range(2):
            even = out[2 * pp * QBLK:(2 * pp + 1) * QBLK]
            odd = out[(2 * pp + 1) * QBLK:(2 * pp + 2) * QBLK]
            p = 2 * g + pp
            o_ref[0, qi * QBLK:(qi + 1) * QBLK, p * LANES:(p + 1) * LANES] = (
                jnp.where(low, even, odd).astype(BF16))


def _attn_call(q, kd, vd, kc, vc, sinkb, n_loc):
    b, t, nq = q.shape
    nb = t // QBLK
    qpb = min(ATTN_QBLOCKS, nb)
    nctx = kc.shape[1]
    kw = kd.shape[2]
    qspec = pl.BlockSpec((1, qpb * QBLK, nq), lambda bb, i: (bb, i, 0))
    loc = []
    for off in range(-1, qpb + 1) if n_loc else ():
        loc.append(pl.BlockSpec((1, QBLK, kw), functools.partial(
            lambda bb, i, off: (bb, jnp.clip(i * qpb + off, 0, nb - 1), 0), off=off)))
    cspec = pl.BlockSpec((1, nctx, kw), lambda bb, i: (bb, 0, 0))
    args = [q] + [kd] * len(loc) + [vd] * len(loc) + [kc, vc, sinkb]
    return pl.pallas_call(
        functools.partial(_attn_kernel, local=bool(n_loc), qpb=qpb),
        grid=(b, nb // qpb),
        in_specs=[qspec] + loc + loc + [cspec, cspec, pl.BlockSpec(sinkb.shape, lambda bb, i: (0, 0))],
        out_specs=qspec,
        out_shape=jax.ShapeDtypeStruct((b, t, nq), BF16),
        compiler_params=_cparams(("parallel", "parallel")),
    )(*args)


def _rwkv_prep_kernel(z_ref, zp_ref, zn_ref, mu_ref, wa_ref, w0a0_ref, kk_ref, ka_ref, rk_ref,
                      g2_ref, bd_ref, tri_ref, opf_ref, opb_ref, v_ref, pc_ref, bonus_ref, gate_ref):
    j = pl.program_id(1)
    nblk = pl.num_programs(1)
    z = z_ref[0]
    nrows = z.shape[0]
    c = CHUNK
    hd = kk_ref.shape[1]
    mu = mu_ref[...]
    row8 = lax.broadcasted_iota(jnp.int32, (8, z.shape[1]), 0)
    zp = pltpu.roll(z, 1, 0)
    zp = jnp.concatenate([jnp.where(row8 == 0, jnp.where(j == 0, 0.0, zp_ref[0, 7:8]), zp[:8]), zp[8:]],
                         axis=0)
    zn = pltpu.roll(z, nrows - 1, 0)
    zn = jnp.concatenate([zn[:nrows - 8],
                          jnp.where(row8 == 7, jnp.where(j == nblk - 1, 0.0, zn_ref[0, 0:1]), zn[nrows - 8:])],
                         axis=0)
    zs = mu[2:3] * z + mu[0:1] * zp + mu[1:2] * zn
    r = zs[:, 0:hd]
    k = zs[:, hd:2 * hd]
    v = zs[:, 2 * hd:3 * hd]
    wa_in = zs[:, 3 * hd:3 * hd + LANES]
    gl = zs[:, 3 * hd + LANES:]
    bd = bd_ref[...]
    v_ref[0] = v.astype(BF16)
    bonus_ref[0] = _split_dot(r * k * rk_ref[...], bd, 1) * v
    gate_ref[0] = _dot(jax.nn.sigmoid(gl).astype(BF16), g2_ref[...])
    low_r = lax.broadcasted_iota(jnp.int32, (nrows, LANES), 1) < DECAY_LORA
    tw = jnp.where(low_r, jnp.tanh(wa_in), wa_in)
    xwa2 = _split_dot(tw, wa_ref[...], 2) + w0a0_ref[...]
    kkr = k * kk_ref[...]
    kk = kkr * lax.rsqrt(_split_dot(kkr * kkr, bd, 1) + 1e-12)
    nsub = nrows // c
    for d, op_ref in enumerate((opf_ref, opb_ref)):
        xs = xwa2[:, 2 * d * hd:(2 * d + 1) * hd]
        a = 1.0 / (1.0 + jnp.exp2(-xwa2[:, (2 * d + 1) * hd:(2 * d + 2) * hd]))
        w_l2 = jnp.minimum(xs, 0.0) - jnp.log(1.0 + jnp.exp2(-jnp.abs(xs))) * LOG2E + W_LOG2_OFFSET
        lw = -jnp.exp2(w_l2)
        kd = k * (ka_ref[0:1] + a * ka_ref[1:2])
        bb = kk * a
        cum = _split_dot_left(tri_ref[d], lw, 2)
        e_in = jnp.exp2(cum)
        e_ex = jnp.exp2(cum - lw)
        e_neg = jnp.exp2(-cum)
        op_ref[0] = jnp.concatenate([kk * e_ex, r * e_in, bb * e_neg, kd * e_neg], axis=1).astype(BF16)
        for s in range(nsub):
            total = cum[s * c:s * c + 1] if d else cum[s * c + c - 1:s * c + c]
            pc_ref[0, 8 * s:8 * s + 8, d * hd:(d + 1) * hd] = jnp.broadcast_to(jnp.exp2(total), (8, hd))


def _rwkv_prep_call(z, mu, wa2, w0a02, k_k, k_a, r_k, g2, bd):
    b, t, nz = z.shape
    rows = min(RWKV_PREP_ROWS, t)
    nsub = rows // CHUNK
    ti = np.arange(CHUNK)
    tri = np.stack([np.kron(np.eye(nsub), ti[None, :] <= ti[:, None]),
                    np.kron(np.eye(nsub), ti[None, :] >= ti[:, None])]).astype(np.float32)
    tri = jnp.asarray(tri, F32).astype(BF16)
    hd = k_k.shape[1]
    nblk = t // rows
    cb = rows // 8
    nb8 = t // 8
    blk = lambda bb, j: (bb, j, 0)
    const2 = lambda bb, j: (0, 0)
    const3 = lambda bb, j: (0, 0, 0)
    full = lambda a: pl.BlockSpec(a.shape, const2 if a.ndim == 2 else const3)
    return pl.pallas_call(
        _rwkv_prep_kernel,
        grid=(b, nblk),
        in_specs=[pl.BlockSpec((1, rows, nz), blk),
                  pl.BlockSpec((1, 8, nz), lambda bb, j: (bb, jnp.maximum(j * cb - 1, 0), 0)),
                  pl.BlockSpec((1, 8, nz), lambda bb, j: (bb, jnp.minimum((j + 1) * cb, nb8 - 1), 0)),
                  full(mu), full(wa2), full(w0a02), full(k_k), full(k_a), full(r_k), full(g2), full(bd),
                  full(tri)],
        out_specs=[pl.BlockSpec((1, rows, 4 * hd), blk), pl.BlockSpec((1, rows, 4 * hd), blk),
                   pl.BlockSpec((1, rows, hd), blk), pl.BlockSpec((1, 8 * nsub, 2 * hd), blk),
                   pl.BlockSpec((1, rows, hd), blk), pl.BlockSpec((1, rows, hd), blk)],
        out_shape=[jax.ShapeDtypeStruct((b, t, 4 * hd), BF16), jax.ShapeDtypeStruct((b, t, 4 * hd), BF16),
                   jax.ShapeDtypeStruct((b, t, hd), BF16),
                   jax.ShapeDtypeStruct((b, 8 * (t // CHUNK), 2 * hd), F32),
                   jax.ShapeDtypeStruct((b, t, hd), F32), jax.ShapeDtypeStruct((b, t, hd), F32)],
        compiler_params=_cparams(("parallel", "parallel")),
    )(z, z, z, mu, wa2, w0a02, k_k, k_a, r_k, g2, bd, tri)


def _rwkv_chains(op_ref, v_ref, pc_ref, keep, bi, d):
    c = CHUNK
    hd = v_ref.shape[2]
    nsub = op_ref.shape[1] // c
    low = lax.broadcasted_iota(jnp.int32, (c, LANES), 1) < RWKV_N

    def stack(xp):
        zero = jnp.zeros_like(xp)
        return jnp.concatenate([jnp.where(low, xp, zero), jnp.where(low, zero, xp)], axis=0)

    chains = []
    for s in range(nsub):
        rows = slice(s * c, (s + 1) * c)
        for p in range(hd // LANES):
            sb = lambda i: stack(op_ref[bi, rows, i * hd + p * LANES:i * hd + (p + 1) * LANES])
            chains.append(dict(
                d=2 * bi + d, p=p, s=s, keep=keep,
                pc=pc_ref[bi, 8 * s:8 * s + 1, d * hd + p * LANES:d * hd + (p + 1) * LANES],
                kts=sb(0), rtb=sb(1), bts=sb(2), kdts=sb(3),
                vs=stack(v_ref[bi, rows, p * LANES:(p + 1) * LANES])))
    return chains


def _rwkv_solve(chains, ioff, state_ref, c):
    c2 = 2 * c
    bf = lambda a: a.astype(BF16)
    rr = lax.broadcasted_iota(jnp.int32, (c2, 2 * c2), 0)
    cc = lax.broadcasted_iota(jnp.int32, (c2, 2 * c2), 1)
    diag = (rr // c) == ((cc // c) % 2)

    def blockdiag(ab):
        n = ab.shape[1] // 2
        z = jnp.zeros((ab.shape[0], n), ab.dtype)
        return jnp.concatenate([jnp.concatenate([ab[:, :n], z], axis=1),
                                jnp.concatenate([z, ab[:, n:]], axis=1)], axis=0)

    def swap_halves(g):
        return jnp.concatenate([g[c:], g[:c]], axis=0)

    by_key = {(ch["d"], ch["s"], ch["p"]): ch for ch in chains}
    supers = [(by_key[(d, s, p)], by_key[(d, s, p + 1)])
              for (d, s, p) in sorted(by_key) if p % 2 == 0]
    sup = [dict(c0=a, c1=b, d=a["d"], s=a["s"], q=a["p"] // 2) for a, b in supers]
    for ch in chains:
        ch["kr"] = jnp.concatenate([ch["kts"], ch["rtb"]], axis=0)
        ch["bk"] = jnp.concatenate([ch["bts"], ch["kdts"]], axis=0)
        aa = _dot_nt(ch["kr"], ch["bk"])
        aa = aa * ch["keep"]
        ch["auk"] = aa[:c2, c2:]
        ch["arr"] = bf(aa[c2:, :])
        ch["b0"] = ioff - aa[:c2, :c2]
    for sc in sup:
        sc["b"] = jnp.concatenate([sc["c0"]["b0"], sc["c1"]["b0"]], axis=1)
        sc["kr"] = jnp.concatenate([sc["c0"]["kr"], sc["c1"]["kr"]], axis=1)
    for _ in range(6):
        for sc in sup:
            b = sc["b"]
            xbd = jnp.where(diag, b, 0.0)
            sc["b"] = _dot(bf(xbd), blockdiag(bf(b))) + (b - xbd)
    for sc in sup:
        auk = jnp.concatenate([sc["c0"]["auk"], sc["c1"]["auk"]], axis=1)
        vs2 = jnp.concatenate([sc["c0"]["vs"], sc["c1"]["vs"]], axis=1)
        sc["av"] = _dot(bf(auk), blockdiag(vs2))
    for sc in sup:
        sc["tsw"] = bf(jnp.where(diag, 0.0, sc["b"]))
    groups = {}
    for sc in sup:
        groups.setdefault((sc["d"], sc["q"]), []).append(sc)
    state = {(d, p): state_ref[d // 2, d % 2, p] for (d, q) in groups for p in (2 * q, 2 * q + 1)}
    nsub = len(next(iter(groups.values())))
    ys = {}
    for step in range(nsub):
        cur = {key: sorted(g, key=lambda sc: sc["s"], reverse=bool(key[0] % 2))[step]
               for key, g in groups.items()}
        for (d, q), sc in cur.items():
            s2 = jnp.concatenate([bf(state[(d, 2 * q)]), bf(state[(d, 2 * q + 1)])], axis=1)
            sc["ksrs"] = _dot_nt(sc["kr"], blockdiag(s2))
        for (d, q), sc in cur.items():
            g = sc["av"] + sc["ksrs"][:c2]
            sc["ub"] = bf(-_dot(sc["tsw"], blockdiag(bf(swap_halves(g)))))
        for (d, q), sc in cur.items():
            for i, ch in enumerate((sc["c0"], sc["c1"])):
                key = (d, 2 * q + i)
                uv = jnp.concatenate([sc["ub"][:, i * c2:(i + 1) * c2], ch["vs"]], axis=0)
                y = sc["ksrs"][c2:, i * c2:(i + 1) * c2] + _dot(ch["arr"], uv)
                state[key] = (state[key] + _dot_tn(uv, ch["bk"])) * ch["pc"]
                ys[(d, sc["s"], key[1])] = y[:c] + y[c:]
    for key, s_new in state.items():
        state_ref[key[0] // 2, key[0] % 2, key[1]] = s_new
    dirs = sorted({d for d, _ in state})
    n_pairs = len(state) // len(dirs)
    return [jnp.concatenate([jnp.concatenate([ys[(d, s, p)] for p in range(n_pairs)], axis=1)
                             for s in range(nsub)], axis=0) for d in dirs]


def _split_dot_left(m_bf16, x, passes):
    acc = None
    rem = x
    for _ in range(passes):
        piece = rem.astype(BF16)
        term = _dot(m_bf16, piece)
        acc = term if acc is None else acc + term
        rem = rem - piece.astype(F32)
    return acc


def _rwkv_solve_kernel(opf_ref, opb_ref, vf_ref, vb_ref, pcf_ref, pcb_ref, s0_ref, keep_ref, ioff_ref,
                       yf_ref, yb_ref, state_ref):
    @pl.when(pl.program_id(1) == 0)
    def _():
        state_ref[...] = s0_ref[...]

    chains = []
    for bi in range(opf_ref.shape[0]):
        chains += _rwkv_chains(opf_ref, vf_ref, pcf_ref, keep_ref[0], bi, 0)
        chains += _rwkv_chains(opb_ref, vb_ref, pcb_ref, keep_ref[1], bi, 1)
    ys = _rwkv_solve(chains, ioff_ref[...], state_ref, CHUNK)
    for bi in range(opf_ref.shape[0]):
        yf_ref[bi] = ys[2 * bi]
        yb_ref[bi] = ys[2 * bi + 1]


def _rwkv_masks(c):
    tt = (np.arange(4 * c) % c)[:, None]
    ss = (np.arange(4 * c) % c)[None, :]
    incl = (np.arange(4 * c) >= 2 * c)[:, None]
    keep = np.stack([np.where(incl, ss <= tt, ss < tt), np.where(incl, ss >= tt, ss > tt)])
    ioff = np.kron(np.array([[0.0, 1.0], [1.0, 0.0]]), np.eye(c))
    return jnp.asarray(keep.astype(np.float32)), jnp.asarray(ioff, F32)


def _rwkv_solve_call(opf, opb, vb, pcs, s0):
    b, t, hd = vb.shape
    rows = RWKV_BLOCK_CHUNKS * CHUNK
    keep, ioff = _rwkv_masks(CHUNK)
    nblk = t // rows
    nbt = RWKV_BATCH_ROWS if b % RWKV_BATCH_ROWS == 0 else 1
    fwd = lambda bb, j: (bb, j, 0)
    bwd = lambda bb, j: (bb, nblk - 1 - j, 0)
    st = pl.BlockSpec((nbt,) + s0.shape[1:], lambda bb, j: (bb, 0, 0, 0, 0))
    spec = lambda width, idx: pl.BlockSpec((nbt, rows, width), idx)
    pcspec = lambda idx: pl.BlockSpec((nbt, 8 * RWKV_BLOCK_CHUNKS, 2 * hd), idx)
    return pl.pallas_call(
        _rwkv_solve_kernel,
        grid=(b // nbt, nblk),
        in_specs=[spec(4 * hd, fwd), spec(4 * hd, bwd), spec(hd, fwd), spec(hd, bwd),
                  pcspec(fwd), pcspec(bwd), st,
                  pl.BlockSpec(keep.shape, lambda bb, j: (0, 0, 0)),
                  pl.BlockSpec(ioff.shape, lambda bb, j: (0, 0))],
        out_specs=[spec(hd, fwd), spec(hd, bwd), st],
        out_shape=[jax.ShapeDtypeStruct((b, t, hd), F32)] * 2 + [jax.ShapeDtypeStruct(s0.shape, F32)],
        compiler_params=_cparams(("parallel", "arbitrary")),
    )(opf, opb, vb, vb, pcs, pcs, s0, keep, ioff)


def _readout_kernel(att_ref, yf_ref, yb_ref, bonus_ref, gate_ref, x_ref, lg_ref, lb_ref, bd_ref,
                    wo_ref, gt_ref, g2_ref, sh_ref, sc_ref, xo_ref, h_ref):
    bd = bd_ref[...]
    inv_n = 1.0 / RWKV_N
    y = yf_ref[...] + yb_ref[...]
    tm = y.shape[0]
    ysq = y * y
    y_hi = y.astype(BF16)
    q_hi = ysq.astype(BF16)
    parts = jnp.concatenate([y_hi, (y - y_hi.astype(F32)).astype(BF16),
                             q_hi, (ysq - q_hi.astype(F32)).astype(BF16)], axis=0)
    st = _dot(parts, bd) * inv_n
    mean = st[:tm] + st[tm:2 * tm]
    var = st[2 * tm:3 * tm] + st[3 * tm:] - mean * mean
    yn = (y - mean) * lax.rsqrt(var + GN_EPS) * lg_ref[...] + lb_ref[...]
    rw = (yn + bonus_ref[...]) * gate_ref[...]
    cat = jnp.concatenate([att_ref[...], rw.astype(BF16)], axis=1)
    xm = x_ref[...] + gt_ref[0] * _dot(cat, wo_ref[...])
    xo_ref[...] = xm
    h_ref[...] = _rms_mod(xm, g2_ref[...], sh_ref[0], sc_ref[0]).astype(BF16)


def _readout_call(att, yf, yb, bonus, gate, x2, lnx_g, lnx_b, bd, w_out, gt1, g2, sh2, sc2,
                  rows_per_group, tm):
    r, d = x2.shape
    hd = yf.shape[1]
    per_group = rows_per_group // tm
    row = lambda i: (i, 0)
    c2 = lambda i: (0, 0)
    grp = lambda i: (i // per_group, 0, 0)
    half = pl.BlockSpec((tm, hd), row)
    full = pl.BlockSpec((tm, d), row)
    vec = pl.BlockSpec((1, 1, d), grp)
    return pl.pallas_call(
        _readout_kernel,
        grid=(r // tm,),
        in_specs=[half, half, half, half, half, full,
                  pl.BlockSpec((1, hd), c2), pl.BlockSpec((1, hd), c2), pl.BlockSpec(bd.shape, c2),
                  pl.BlockSpec(w_out.shape, c2), vec, pl.BlockSpec((1, d), c2), vec, vec],
        out_specs=[full, full],
        out_shape=[jax.ShapeDtypeStruct((r, d), F32), jax.ShapeDtypeStruct((r, d), BF16)],
        compiler_params=_cparams(("parallel",)),
    )(att, yf, yb, bonus, gate, x2, lnx_g, lnx_b, bd, w_out, gt1, g2, sh2, sc2)


def _mlp_kernel(h_ref, x_ref, w1_ref, w2_ref, gt_ref, gn_ref, shn_ref, scn_ref, xo_ref, ho_ref,
                acc_ref, *, final):
    j = pl.program_id(1)

    @pl.when(j == 0)
    def _():
        acc_ref[...] = jnp.zeros_like(acc_ref)

    a = jnp.maximum(_dot(h_ref[...], w1_ref[...]), 0.0)
    acc_ref[...] += _dot((a * a).astype(BF16), w2_ref[...])

    @pl.when(j == pl.num_programs(1) - 1)
    def _():
        xo = x_ref[...] + gt_ref[0] * acc_ref[...]
        if final:
            ms = jnp.mean(xo * xo, axis=-1, keepdims=True)
            xo_ref[...] = xo * lax.rsqrt(ms + NORM_EPS) * gn_ref[...]
            ho_ref[...] = jnp.zeros_like(ho_ref)
        else:
            xo_ref[...] = xo
            ho_ref[...] = _rms_mod(xo, gn_ref[...], shn_ref[0], scn_ref[0]).astype(BF16)


def _mlp_call(h, x2, w1, w2, gt2, g_next, sh_next, sc_next, rows_per_group, tm, tf, final):
    r, d = x2.shape
    ff = w1.shape[1]
    per_group = rows_per_group // tm
    row = lambda i, j: (i, 0)
    grp = lambda i, j: (i // per_group, 0, 0)
    vec = pl.BlockSpec((1, 1, d), grp)
    return pl.pallas_call(
        functools.partial(_mlp_kernel, final=final),
        grid=(r // tm, ff // tf),
        in_specs=[pl.BlockSpec((tm, d), row), pl.BlockSpec((tm, d), row),
                  pl.BlockSpec((d, tf), lambda i, j: (0, j)), pl.BlockSpec((tf, d), lambda i, j: (j, 0)),
                  vec, pl.BlockSpec((1, d), lambda i, j: (0, 0)), vec, vec],
        out_specs=[pl.BlockSpec((tm, d), row), pl.BlockSpec((tm, d), row)],
        out_shape=[jax.ShapeDtypeStruct((r, d), F32), jax.ShapeDtypeStruct((r, d), BF16)],
        scratch_shapes=[pltpu.VMEM((tm, d), F32)],
        compiler_params=_cparams(("parallel", "arbitrary")),
    )(h, x2, w1, w2, gt2, g_next, sh_next, sc_next)


def _fft1_kernel(h_ref, kr_ref, wc_ref, z_ref):
    l1, nb, d = h_ref.shape[1:]
    rows = l1 * nb
    hf = h_ref[0].reshape(rows, d)
    p = _dot(kr_ref[...], hf)
    gw = d // FOURIER_GROUPS
    wc = wc_ref[...]
    zr, zi = [], []
    for g in range(FOURIER_GROUPS):
        ap = jnp.concatenate([p[:rows, g * gw:(g + 1) * gw], p[rows:, g * gw:(g + 1) * gw]],
                             axis=1).astype(BF16)
        zz = _dot(ap, wc)
        zr.append(zz[:, :gw])
        zi.append(zz[:, gw:])
    z = jnp.concatenate(zr + zi, axis=1).astype(BF16)
    z_ref[0] = z.reshape(l1, nb, 2 * d)


def _fft1_call(h4, kr1, wc, nb):
    b, l1, l2, d = h4.shape
    return pl.pallas_call(
        _fft1_kernel,
        grid=(b, l2 // nb),
        in_specs=[pl.BlockSpec((1, l1, nb, d), lambda bb, j: (bb, 0, j, 0)),
                  pl.BlockSpec(kr1.shape, lambda bb, j: (0, 0)),
                  pl.BlockSpec(wc.shape, lambda bb, j: (0, 0))],
        out_specs=pl.BlockSpec((1, l1, nb, 2 * d), lambda bb, j: (bb, 0, j, 0)),
        out_shape=jax.ShapeDtypeStruct((b, l1, l2, 2 * d), BF16),
        compiler_params=_cparams(("parallel", "parallel")),
    )(h4, kr1, wc)


def _fft2_kernel(z_ref, gk_ref, x_ref, wo_ref, gt_ref, g2_ref, sh_ref, sc_ref, xo_ref, h_ref):
    mb, l2, d2 = z_ref.shape[1:]
    m2b = x_ref.shape[1]
    d = d2 // 2
    z = z_ref[0].reshape(mb * l2, d2)
    rhs = jnp.concatenate([z[:, :d], z[:, d:]], axis=0)
    f = _dot(gk_ref[0], rhs)
    y = _dot(f.astype(BF16), wo_ref[...])
    xm = x_ref[0].reshape(m2b * mb, d) + gt_ref[0] * y
    xo_ref[0] = xm.reshape(m2b, mb, d)
    h_ref[0] = _rms_mod(xm, g2_ref[...], sh_ref[0], sc_ref[0]).astype(BF16).reshape(m2b, mb, d)


def _fft2_call(z4, gk, x4, w_out, gt1, g2, sh2, sc2, mb, m2b):
    b, l1, l2, d2 = z4.shape
    d = d2 // 2
    nblk = l1 // mb
    xspec = pl.BlockSpec((1, m2b, mb, d), lambda m, bb, h: (bb, h, m, 0))
    vec = pl.BlockSpec((1, 1, d), lambda m, bb, h: (bb, 0, 0))
    return pl.pallas_call(
        _fft2_kernel,
        grid=(nblk, b, l2 // m2b),
        in_specs=[pl.BlockSpec((1, mb, l2, d2), lambda m, bb, h: (bb, m, 0, 0)),
                  pl.BlockSpec((1, m2b * mb, gk.shape[2]), lambda m, bb, h: (m, h, 0)),
                  xspec,
                  pl.BlockSpec(w_out.shape, lambda m, bb, h: (0, 0)),
                  vec, pl.BlockSpec((1, d), lambda m, bb, h: (0, 0)), vec, vec],
        out_specs=[xspec, xspec],
        out_shape=[jax.ShapeDtypeStruct((b, l2, l1, d), F32), jax.ShapeDtypeStruct((b, l2, l1, d), BF16)],
        compiler_params=_cparams(("parallel", "parallel", "parallel")),
    )(z4, gk, x4, w_out, gt1, g2, sh2, sc2)


def _rope_tables(t):
    axis_dim = HEAD_DIM // 2
    rows = t // GRID_W
    row = jnp.broadcast_to(jnp.arange(rows, dtype=F32)[:, None], (rows, GRID_W)).reshape(t)
    col = jnp.broadcast_to(jnp.arange(GRID_W, dtype=F32)[None, :], (rows, GRID_W)).reshape(t)
    inv = ROPE_BASE ** (-jnp.arange(0, axis_dim, 2, dtype=F32) / axis_dim)
    ang_r, ang_c = row[:, None] * inv, col[:, None] * inv
    cos = jnp.concatenate([jnp.cos(ang_r), jnp.cos(ang_r), jnp.cos(ang_c), jnp.cos(ang_c)], axis=1)
    sin = jnp.concatenate([-jnp.sin(ang_r), jnp.sin(ang_r), -jnp.sin(ang_c), jnp.sin(ang_c)], axis=1)
    return jnp.tile(cos, (1, 2)), jnp.tile(sin, (1, 2))


def _fft_tables(t, gw, nb, mb):
    l1 = t // FFT_L2
    n1 = np.arange(l1)
    ang1 = 2.0 * np.pi * np.outer(n1, n1) / l1
    eye = np.eye(nb)
    kr1 = np.concatenate([np.kron(np.cos(ang1), eye), np.kron(np.sin(ang1), eye)], axis=0)
    ch = np.arange(gw)
    angc = 2.0 * np.pi * np.outer(ch, ch) / gw
    cg, sg = np.cos(angc), np.sin(angc)
    wc = np.block([[cg, -sg], [-sg, -cg]])
    scale = 1.0 / np.sqrt(float(t) * gw)
    m = np.arange(t)
    n2 = np.arange(FFT_L2)
    theta = 2.0 * np.pi * np.outer(m, n2) / t
    cs = np.stack([np.cos(theta), np.sin(theta)], axis=0) * scale
    cs = cs.reshape(2, FFT_L2, l1 // mb, mb, FFT_L2)
    return (jnp.asarray(kr1, F32).astype(BF16), jnp.asarray(wc, F32).astype(BF16),
            jnp.asarray(cs, F32))


def _expand_gk(cs, mb):
    eye = jnp.eye(mb, dtype=F32)
    g = jnp.einsum("rmbpn,pq->bmprqn", cs, eye)
    nblk = cs.shape[2]
    return g.reshape(nblk, FFT_L2 * mb, 2 * mb * FFT_L2).astype(BF16)


def kernel(x, c, ctx, c_ctx, ada_w, ada_b, norm1_g, norm2_g, mix_w_in, mix_w_out, attn_sink,
           shift_mu_prev, shift_mu_next, decay_w0, decay_w2, iclr_a0, iclr_a2, gate_g2, key_kk,
           key_ka, bonus_rk, lnx_g, lnx_b, fourier_w_out, mlp_w1, mlp_w2, final_g):
    b, t, d = x.shape
    nctx = ctx.shape[1]
    hd = key_kk.shape[1]
    q_dim = d - hd
    n_heads = q_dim // HEAD_DIM
    kv_dim = (n_heads // 4) * HEAD_DIM
    att_cols = q_dim + 2 * kv_dim

    cond = jnp.zeros((8, d), F32).at[:b].set(c).at[b].set(c_ctx)
    mods = _ada_call(cond, ada_w, ada_b)
    lat = [mods[i, :b].reshape(b, N_MOD, 1, d) for i in range(2)]
    cmod = [mods[i, b:b + 1].reshape(1, N_MOD, 1, d) for i in range(2)]
    lm = lambda i, k: lat[i][:, k]
    cm = lambda i, k: cmod[i][:, k]
    row1 = lambda a: a.reshape(1, -1)

    w_in = mix_w_in[0]
    wk = w_in[:, q_dim:q_dim + kv_dim].reshape(d, kv_dim // HEAD_DIM, 1, HEAD_DIM)
    wv = w_in[:, q_dim + kv_dim:att_cols].reshape(d, kv_dim // HEAD_DIM, 1, HEAD_DIM)
    dup = lambda w: jnp.broadcast_to(w, (d, kv_dim // HEAD_DIM, 2, HEAD_DIM)).reshape(d, 2 * kv_dim)
    w_att = jnp.concatenate([w_in[:, :q_dim], dup(wk), dup(wv)], axis=1).astype(BF16)
    w_rw = w_in[:, att_cols:].astype(BF16)
    cos_t, sin_t = _rope_tables(t)
    cos_c, sin_c = jnp.ones((nctx, LANES), F32), jnp.zeros((nctx, LANES), F32)
    g1 = row1(norm1_g[0])
    x2 = x.reshape(b * t, d)
    ctx2 = ctx.reshape(b * nctx, d)
    tm_in = min(512, t)
    q, kd, vd, zrw = _inproj_call(x2, g1, lm(0, 0), lm(0, 1), cos_t, sin_t, w_att, w_rw, t, tm_in)
    qc, kc, vc, zrwc = _inproj_call(ctx2, g1, cm(0, 0), cm(0, 1), cos_c, sin_c, w_att, w_rw,
                                    b * nctx, nctx)
    q, kd, vd = (a.reshape(b, t, -1) for a in (q, kd, vd))
    qc, kc, vc = (a.reshape(b, nctx, -1) for a in (qc, kc, vc))
    sinkb = jnp.broadcast_to(attn_sink[0][:, None] * LOG2E, (n_heads, LANES)).astype(F32)
    att = _attn_call(q, kd, vd, kc, vc, sinkb, 3)
    att_c = _attn_call(qc, kc, vc, kc, vc, sinkb, 0)

    mu = jnp.stack([shift_mu_prev[0], shift_mu_next[0], 1.0 - shift_mu_prev[0] - shift_mu_next[0]])
    zl = jnp.zeros((DECAY_LORA, hd), F32)
    wa = jnp.stack([jnp.concatenate([jnp.concatenate([decay_w2[0, dd], zl], axis=1),
                                     jnp.concatenate([zl, iclr_a2[0, dd]], axis=1)], axis=0)
                    for dd in range(2)])
    wa = (jnp.concatenate([wa[0], wa[1]], axis=1) * LOG2E).astype(BF16)
    w0a0 = jnp.concatenate([decay_w0[0], iclr_a0[0]], axis=1).reshape(1, 4 * hd) * LOG2E
    seg = np.arange(hd) // RWKV_N
    bd = jnp.asarray(seg[:, None] == seg[None, :], F32).astype(BF16)
    k_k, r_k = row1(key_kk[0]), row1(bonus_rk[0])
    k_a = jnp.stack([1.0 - key_ka[0], key_ka[0]])
    g2w = gate_g2[0].astype(BF16)
    s_zero = jnp.zeros((b, 2, hd // LANES, LANES, LANES), F32)
    opf_c, opb_c, vb_c, pcs_c, bonus_c, gate_c = _rwkv_prep_call(zrwc.reshape(b, nctx, -1), mu, wa, w0a0,
                                                                 k_k, k_a, r_k, g2w, bd)
    opf, opb, vb, pcs, bonus, gate = _rwkv_prep_call(zrw.reshape(b, t, -1), mu, wa, w0a0,
                                                     k_k, k_a, r_k, g2w, bd)
    yfc, ybc, s_ctx = _rwkv_solve_call(opf_c, opb_c, vb_c, pcs_c, s_zero)
    yf, yb, _ = _rwkv_solve_call(opf, opb, vb, pcs, s_ctx)

    w_out = mix_w_out[0].astype(BF16)
    n2g = row1(norm2_g[0])
    flat = lambda a: a.reshape(-1, a.shape[-1])
    xm, h2 = _readout_call(flat(att), flat(yf), flat(yb), flat(bonus), flat(gate), x2,
                           row1(lnx_g[0]), row1(lnx_b[0]), bd, w_out, lm(0, 2), n2g,
                           lm(0, 3), lm(0, 4), t, tm_in)
    xmc, h2c = _readout_call(flat(att_c), flat(yfc), flat(ybc), flat(bonus_c), flat(gate_c), ctx2,
                             row1(lnx_g[0]), row1(lnx_b[0]), bd, w_out, cm(0, 2), n2g,
                             cm(0, 3), cm(0, 4), b * nctx, nctx)
    w1 = mlp_w1[0].astype(BF16)
    w2 = mlp_w2[0].astype(BF16)
    g1n = row1(norm1_g[1])
    tm_mlp = min(1024, t)
    x1, h1 = _mlp_call(h2, xm, w1, w2, lm(0, 5), g1n, lm(1, 0), lm(1, 1), t, tm_mlp, MLP_TF, False)
    ctx1, _ = _mlp_call(h2c, xmc, w1, w2, cm(0, 5), g1n, cm(1, 0), cm(1, 1), b * nctx, nctx, MLP_TF,
                        False)
    del ctx1

    l1 = t // FFT_L2
    nb = 8
    mb = min(8, l1)
    gw = d // FOURIER_GROUPS
    kr1, wc, cs = _fft_tables(t, gw, nb, mb)
    gk = _expand_gk(cs, mb)
    z4 = _fft1_call(h1.reshape(b, l1, FFT_L2, d), kr1, wc, nb)
    xm4, h24 = _fft2_call(z4, gk, x1.reshape(b, FFT_L2, l1, d), fourier_w_out[0].astype(BF16),
                          lm(1, 2), row1(norm2_g[1]), lm(1, 3), lm(1, 4), mb, 64)
    out, _ = _mlp_call(h24.reshape(b * t, d), xm4.reshape(b * t, d), mlp_w1[1].astype(BF16),
                       mlp_w2[1].astype(BF16), lm(1, 5), row1(final_g), lm(1, 0), lm(1, 1),
                       t, tm_mlp, MLP_TF, True)
    return out.reshape(b, t, d)
```

```python
import functools

import numpy as np
import jax
import jax.numpy as jnp
from jax import lax
from jax.experimental import pallas as pl
from jax.experimental.pallas import tpu as pltpu

F32 = jnp.float32
BF16 = jnp.bfloat16
HIGHEST = lax.Precision.HIGHEST

HEAD_DIM = 64
WINDOW = 128
QBLK = 128
ATTN_QBLOCKS = 4
GRID_W = 64
ROPE_BASE = 10000.0
RWKV_N = 64
DECAY_LORA = 64
ICLR_LORA = 64
GATE_LORA = 128
FOURIER_GROUPS = 4
N_MOD = 6
NORM_EPS = 1e-6
GN_EPS = 64e-5
NEG_INF = -1e30

CHUNK = 64
RWKV_PREP_ROWS = 256
RWKV_BLOCK_CHUNKS = 4
RWKV_BATCH_ROWS = 1
LANES = 128
FFT_L2 = 128
MLP_TF = 1024
VMEM_LIMIT = 48 * 1024 * 1024
LOG2E = 1.4426950408889634
Q_SCALE = HEAD_DIM ** -0.5 * LOG2E
W_LOG2_OFFSET = -0.5 * LOG2E + float(np.log2(LOG2E))


def _cparams(sem):
    return pltpu.CompilerParams(dimension_semantics=sem, vmem_limit_bytes=VMEM_LIMIT)


def _dot(a, b, **kw):
    return jnp.dot(a, b, preferred_element_type=F32, **kw)


def _dot_nt(a, b):
    return lax.dot_general(a, b, (((1,), (1,)), ((), ())), preferred_element_type=F32)


def _dot_tn(a, b):
    return lax.dot_general(a, b, (((0,), (0,)), ((), ())), preferred_element_type=F32)


def _split_dot(x, m_bf16, passes):
    acc = None
    rem = x
    for _ in range(passes):
        piece = rem.astype(BF16)
        term = _dot(piece, m_bf16)
        acc = term if acc is None else acc + term
        rem = rem - piece.astype(F32)
    return acc


def _rms_mod(x, g, sh, sc):
    ms = jnp.mean(x * x, axis=-1, keepdims=True)
    return (x * lax.rsqrt(ms + NORM_EPS)) * (g * (1.0 + sc)) + sh


def _ada_kernel(cond_ref, w_ref, b_ref, o_ref):
    s = cond_ref[...]
    s = s * jax.nn.sigmoid(s)
    o_ref[0] = _dot(s, w_ref[0], precision=HIGHEST) + b_ref[0]


def _ada_call(cond, ada_w, ada_b):
    depth, d, n = ada_w.shape
    tn = 1536
    return pl.pallas_call(
        _ada_kernel,
        grid=(depth, n // tn),
        in_specs=[pl.BlockSpec((8, d), lambda i, j: (0, 0)),
                  pl.BlockSpec((1, d, tn), lambda i, j: (i, 0, j)),
                  pl.BlockSpec((1, 1, tn), lambda i, j: (i, 0, j))],
        out_specs=pl.BlockSpec((1, 8, tn), lambda i, j: (i, 0, j)),
        out_shape=jax.ShapeDtypeStruct((depth, 8, n), F32),
        compiler_params=_cparams(("parallel", "parallel")),
    )(cond, ada_w, ada_b.reshape(depth, 1, n))


def _inproj_kernel(x_ref, g_ref, sh_ref, sc_ref, cos_ref, sin_ref, wa_ref, wr_ref,
                   q_ref, k_ref, v_ref, z_ref):
    h = _rms_mod(x_ref[...], g_ref[...], sh_ref[0], sc_ref[0]).astype(BF16)
    z_ref[...] = _dot(h, wr_ref[...])
    za = _dot(h, wa_ref[...])
    cos = cos_ref[...]
    sin = sin_ref[...]
    lane = lax.broadcasted_iota(jnp.int32, cos.shape, 1)
    first = (lane % 32) < 16
    nq = q_ref.shape[1] // LANES
    nk = k_ref.shape[1] // LANES
    for c in range(nq + nk):
        s = za[:, c * LANES:(c + 1) * LANES]
        partner = jnp.where(first, pltpu.roll(s, LANES - 16, 1), pltpu.roll(s, 16, 1))
        ro = s * cos + partner * sin
        if c < nq:
            q_ref[:, c * LANES:(c + 1) * LANES] = (ro * Q_SCALE).astype(BF16)
        else:
            k_ref[:, (c - nq) * LANES:(c - nq + 1) * LANES] = ro.astype(BF16)
    v_ref[...] = za[:, (nq + nk) * LANES:].astype(BF16)


def _inproj_call(x2, g1, sh, sc, cos, sin, w_att, w_rw, rows_per_group, tm):
    r, d = x2.shape
    period = cos.shape[0]
    n_per = period // tm
    per_group = rows_per_group // tm
    na = w_att.shape[1]
    nr = w_rw.shape[1]
    nq, nkd = 512, 256
    row = lambda i: (i, 0)
    grp = lambda i: (i // per_group, 0, 0)
    return pl.pallas_call(
        _inproj_kernel,
        grid=(r // tm,),
        in_specs=[pl.BlockSpec((tm, d), row),
                  pl.BlockSpec((1, d), lambda i: (0, 0)),
                  pl.BlockSpec((1, 1, d), grp),
                  pl.BlockSpec((1, 1, d), grp),
                  pl.BlockSpec((tm, LANES), lambda i: (i % n_per, 0)),
                  pl.BlockSpec((tm, LANES), lambda i: (i % n_per, 0)),
                  pl.BlockSpec((d, na), lambda i: (0, 0)),
                  pl.BlockSpec((d, nr), lambda i: (0, 0))],
        out_specs=[pl.BlockSpec((tm, nq), row), pl.BlockSpec((tm, nkd), row),
                   pl.BlockSpec((tm, nkd), row), pl.BlockSpec((tm, nr), row)],
        out_shape=[jax.ShapeDtypeStruct((r, nq), BF16), jax.ShapeDtypeStruct((r, nkd), BF16),
                   jax.ShapeDtypeStruct((r, nkd), BF16), jax.ShapeDtypeStruct((r, nr), F32)],
        compiler_params=_cparams(("parallel",)),
    )(x2, g1, sh, sc, cos, sin, w_att, w_rw)


def _attn_kernel(*refs, local, qpb):
    q_ref = refs[0]
    nk = qpb + 2 if local else 0
    k_refs = refs[1:1 + nk]
    v_refs = refs[1 + nk:1 + 2 * nk]
    kc_ref, vc_ref, sink_ref, o_ref = refs[1 + 2 * nk:]
    i = pl.program_id(1)
    nb = pl.num_programs(1) * qpb
    nctx = kc_ref.shape[1]
    nkeys = (3 * QBLK if local else 0) + nctx
    gq = 4
    rows = gq * QBLK
    low = lax.broadcasted_iota(jnp.int32, (QBLK, LANES), 1) < HEAD_DIM
    if local:
        rq = lax.broadcasted_iota(jnp.int32, (rows, QBLK), 0) % QBLK
        ck = lax.broadcasted_iota(jnp.int32, (rows, QBLK), 1)
    zero = jnp.zeros((QBLK, LANES), BF16)
    ones = jnp.ones((nkeys, LANES), BF16)
    n_groups = q_ref.shape[2] // (gq * HEAD_DIM)
    units = [(qi, g) for qi in range(qpb) for g in range(n_groups)]
    scores, vals, sinks = {}, {}, {}
    for qi, g in units:
        ksl = slice(g * LANES, (g + 1) * LANES)
        keys = jnp.concatenate([kr[0, :, ksl] for kr in k_refs[qi:qi + 3]] + [kc_ref[0, :, ksl]], axis=0)
        vals[qi, g] = jnp.concatenate(
            [jnp.concatenate([vr[0, :, ksl] for vr in v_refs[qi:qi + 3]] + [vc_ref[0, :, ksl]], axis=0), ones],
            axis=1)
        qs = []
        for pp in range(2):
            qp = q_ref[0, qi * QBLK:(qi + 1) * QBLK, (2 * g + pp) * LANES:(2 * g + pp + 1) * LANES]
            qs += [jnp.where(low, qp, zero), jnp.where(low, zero, qp)]
        scores[qi, g] = _dot_nt(jnp.concatenate(qs, axis=0), keys)
        sinks[g] = jnp.concatenate(
            [jnp.broadcast_to(sink_ref[gq * g + h:gq * g + h + 1, 0:1], (QBLK, 1)) for h in range(gq)], axis=0)
    probs, ms = {}, {}
    for qi, g in units:
        s = scores[qi, g]
        if local:
            blk = i * qpb + qi
            mask_prev = ck >= rq + jnp.where(blk >= 1, 0, QBLK)
            mask_next = ck <= rq - jnp.where(blk <= nb - 2, 0, QBLK)
            s = jnp.concatenate([jnp.where(mask_prev, s[:, :QBLK], NEG_INF), s[:, QBLK:2 * QBLK],
                                 jnp.where(mask_next, s[:, 2 * QBLK:3 * QBLK], NEG_INF), s[:, 3 * QBLK:]],
                                axis=1)
        m = jnp.maximum(jnp.max(s, axis=-1, keepdims=True), sinks[g])
        ms[qi, g] = m
        probs[qi, g] = jnp.exp2(s - m).astype(BF16)
    for qi, g in units:
        o = _dot(probs[qi, g], vals[qi, g])
        out = o[:, :LANES] / (o[:, LANES:] + jnp.exp2(sinks[g] - ms[qi, g]))
        for pp in range(2):
            even = out[2 * pp * QBLK:(2 * pp + 1) * QBLK]
            odd = out[(2 * pp + 1) * QBLK:(2 * pp + 2) * QBLK]
            p = 2 * g + pp
            o_ref[0, qi * QBLK:(qi + 1) * QBLK, p * LANES:(p + 1) * LANES] = (
                jnp.where(low, even, odd).astype(BF16))


def _attn_call(q, kd, vd, kc, vc, sinkb, n_loc):
    b, t, nq = q.shape
    nb = t // QBLK
    qpb = min(ATTN_QBLOCKS, nb)
    nctx = kc.shape[1]
    kw = kd.shape[2]
    qspec = pl.BlockSpec((1, qpb * QBLK, nq), lambda bb, i: (bb, i, 0))
    loc = []
    for off in range(-1, qpb + 1) if n_loc else ():
        loc.append(pl.BlockSpec((1, QBLK, kw), functools.partial(
            lambda bb, i, off: (bb, jnp.clip(i * qpb + off, 0, nb - 1), 0), off=off)))
    cspec = pl.BlockSpec((1, nctx, kw), lambda bb, i: (bb, 0, 0))
    args = [q] + [kd] * len(loc) + [vd] * len(loc) + [kc, vc, sinkb]
    return pl.pallas_call(
        functools.partial(_attn_kernel, local=bool(n_loc), qpb=qpb),
        grid=(b, nb // qpb),
        in_specs=[qspec] + loc + loc + [cspec, cspec, pl.BlockSpec(sinkb.shape, lambda bb, i: (0, 0))],
        out_specs=qspec,
        out_shape=jax.ShapeDtypeStruct((b, t, nq), BF16),
        compiler_params=_cparams(("parallel", "parallel")),
    )(*args)


def _rwkv_prep_kernel(z_ref, zp_ref, zn_ref, mu_ref, wa_ref, w0a0_ref, kk_ref, ka_ref, rk_ref,
                      g2_ref, bd_ref, tri_ref, opf_ref, opb_ref, v_ref, pc_ref, bonus_ref, gate_ref):
    j = pl.program_id(1)
    nblk = pl.num_programs(1)
    z = z_ref[0]
    nrows = z.shape[0]
    c = CHUNK
    hd = kk_ref.shape[1]
    mu = mu_ref[...]
    row8 = lax.broadcasted_iota(jnp.int32, (8, z.shape[1]), 0)
    zp = pltpu.roll(z, 1, 0)
    zp = jnp.concatenate([jnp.where(row8 == 0, jnp.where(j == 0, 0.0, zp_ref[0, 7:8]), zp[:8]), zp[8:]],
                         axis=0)
    zn = pltpu.roll(z, nrows - 1, 0)
    zn = jnp.concatenate([zn[:nrows - 8],
                          jnp.where(row8 == 7, jnp.where(j == nblk - 1, 0.0, zn_ref[0, 0:1]), zn[nrows - 8:])],
                         axis=0)
    zs = mu[2:3] * z + mu[0:1] * zp + mu[1:2] * zn
    r = zs[:, 0:hd]
    k = zs[:, hd:2 * hd]
    v = zs[:, 2 * hd:3 * hd]
    wa_in = zs[:, 3 * hd:3 * hd + LANES]
    gl = zs[:, 3 * hd + LANES:]
    bd = bd_ref[...]
    v_ref[0] = v.astype(BF16)
    bonus_ref[0] = _split_dot(r * k * rk_ref[...], bd, 1) * v
    gate_ref[0] = _dot(jax.nn.sigmoid(gl).astype(BF16), g2_ref[...])
    low_r = lax.broadcasted_iota(jnp.int32, (nrows, LANES), 1) < DECAY_LORA
    tw = jnp.where(low_r, jnp.tanh(wa_in), wa_in)
    xwa2 = _split_dot(tw, wa_ref[...], 2) + w0a0_ref[...]
    kkr = k * kk_ref[...]
    kk = kkr * lax.rsqrt(_split_dot(kkr * kkr, bd, 1) + 1e-12)
    nsub = nrows // c
    for d, op_ref in enumerate((opf_ref, opb_ref)):
        xs = xwa2[:, 2 * d * hd:(2 * d + 1) * hd]
        a = 1.0 / (1.0 + jnp.exp2(-xwa2[:, (2 * d + 1) * hd:(2 * d + 2) * hd]))
        w_l2 = jnp.minimum(xs, 0.0) - jnp.log(1.0 + jnp.exp2(-jnp.abs(xs))) * LOG2E + W_LOG2_OFFSET
        lw = -jnp.exp2(w_l2)
        kd = k * (ka_ref[0:1] + a * ka_ref[1:2])
        bb = kk * a
        cum = _split_dot_left(tri_ref[d], lw, 2)
        e_in = jnp.exp2(cum)
        e_ex = jnp.exp2(cum - lw)
        e_neg = jnp.exp2(-cum)
        op_ref[0] = jnp.concatenate([kk * e_ex, r * e_in, bb * e_neg, kd * e_neg], axis=1).astype(BF16)
        for s in range(nsub):
            total = cum[s * c:s * c + 1] if d else cum[s * c + c - 1:s * c + c]
            pc_ref[0, 8 * s:8 * s + 8, d * hd:(d + 1) * hd] = jnp.broadcast_to(jnp.exp2(total), (8, hd))


def _rwkv_prep_call(z, mu, wa2, w0a02, k_k, k_a, r_k, g2, bd):
    b, t, nz = z.shape
    rows = min(RWKV_PREP_ROWS, t)
    nsub = rows // CHUNK
    ti = np.arange(CHUNK)
    tri = np.stack([np.kron(np.eye(nsub), ti[None, :] <= ti[:, None]),
                    np.kron(np.eye(nsub), ti[None, :] >= ti[:, None])]).astype(np.float32)
    tri = jnp.asarray(tri, F32).astype(BF16)
    hd = k_k.shape[1]
    nblk = t // rows
    cb = rows // 8
    nb8 = t // 8
    blk = lambda bb, j: (bb, j, 0)
    const2 = lambda bb, j: (0, 0)
    const3 = lambda bb, j: (0, 0, 0)
    full = lambda a: pl.BlockSpec(a.shape, const2 if a.ndim == 2 else const3)
    return pl.pallas_call(
        _rwkv_prep_kernel,
        grid=(b, nblk),
        in_specs=[pl.BlockSpec((1, rows, nz), blk),
                  pl.BlockSpec((1, 8, nz), lambda bb, j: (bb, jnp.maximum(j * cb - 1, 0), 0)),
                  pl.BlockSpec((1, 8, nz), lambda bb, j: (bb, jnp.minimum((j + 1) * cb, nb8 - 1), 0)),
                  full(mu), full(wa2), full(w0a02), full(k_k), full(k_a), full(r_k), full(g2), full(bd),
                  full(tri)],
        out_specs=[pl.BlockSpec((1, rows, 4 * hd), blk), pl.BlockSpec((1, rows, 4 * hd), blk),
                   pl.BlockSpec((1, rows, hd), blk), pl.BlockSpec((1, 8 * nsub, 2 * hd), blk),
                   pl.BlockSpec((1, rows, hd), blk), pl.BlockSpec((1, rows, hd), blk)],
        out_shape=[jax.ShapeDtypeStruct((b, t, 4 * hd), BF16), jax.ShapeDtypeStruct((b, t, 4 * hd), BF16),
                   jax.ShapeDtypeStruct((b, t, hd), BF16),
                   jax.ShapeDtypeStruct((b, 8 * (t // CHUNK), 2 * hd), F32),
                   jax.ShapeDtypeStruct((b, t, hd), F32), jax.ShapeDtypeStruct((b, t, hd), F32)],
        compiler_params=_cparams(("parallel", "parallel")),
    )(z, z, z, mu, wa2, w0a02, k_k, k_a, r_k, g2, bd, tri)


def _rwkv_chains(op_ref, v_ref, pc_ref, keep, bi, d):
    c = CHUNK
    hd = v_ref.shape[2]
    nsub = op_ref.shape[1] // c
    low = lax.broadcasted_iota(jnp.int32, (c, LANES), 1) < RWKV_N

    def stack(xp):
        zero = jnp.zeros_like(xp)
        return jnp.concatenate([jnp.where(low, xp, zero), jnp.where(low, zero, xp)], axis=0)

    chains = []
    for s in range(nsub):
        rows = slice(s * c, (s + 1) * c)
        for p in range(hd // LANES):
            sb = lambda i: stack(op_ref[bi, rows, i * hd + p * LANES:i * hd + (p + 1) * LANES])
            chains.append(dict(
                d=2 * bi + d, p=p, s=s, keep=keep,
                pc=pc_ref[bi, 8 * s:8 * s + 1, d * hd + p * LANES:d * hd + (p + 1) * LANES],
                kts=sb(0), rtb=sb(1), bts=sb(2), kdts=sb(3),
                vs=stack(v_ref[bi, rows, p * LANES:(p + 1) * LANES])))
    return chains


def _rwkv_solve(chains, ioff, state_ref, c):
    c2 = 2 * c
    bf = lambda a: a.astype(BF16)
    rr = lax.broadcasted_iota(jnp.int32, (c2, 2 * c2), 0)
    cc = lax.broadcasted_iota(jnp.int32, (c2, 2 * c2), 1)
    diag = (rr // c) == ((cc // c) % 2)

    def blockdiag(ab):
        n = ab.shape[1] // 2
        z = jnp.zeros((ab.shape[0], n), ab.dtype)
        return jnp.concatenate([jnp.concatenate([ab[:, :n], z], axis=1),
                                jnp.concatenate([z, ab[:, n:]], axis=1)], axis=0)

    def swap_halves(g):
        return jnp.concatenate([g[c:], g[:c]], axis=0)

    by_key = {(ch["d"], ch["s"], ch["p"]): ch for ch in chains}
    supers = [(by_key[(d, s, p)], by_key[(d, s, p + 1)])
              for (d, s, p) in sorted(by_key) if p % 2 == 0]
    sup = [dict(c0=a, c1=b, d=a["d"], s=a["s"], q=a["p"] // 2) for a, b in supers]
    for ch in chains:
        ch["kr"] = jnp.concatenate([ch["kts"], ch["rtb"]], axis=0)
        ch["bk"] = jnp.concatenate([ch["bts"], ch["kdts"]], axis=0)
        aa = bf(_dot_nt(ch["kr"], ch["bk"])) * ch["keep"]
        ch["auk"] = aa[:c2, c2:]
        ch["arr"] = aa[c2:, :]
        ch["b0"] = ioff - aa[:c2, :c2]
    zero_b = jnp.zeros((c2, 2 * c2), BF16)
    for sc in sup:
        sc["b"] = jnp.concatenate([sc["c0"]["b0"], sc["c1"]["b0"]], axis=1)
        sc["kr"] = jnp.concatenate([sc["c0"]["kr"], sc["c1"]["kr"]], axis=1)
    for _ in range(6):
        for sc in sup:
            b = sc["b"]
            sc["b"] = bf(_dot(jnp.where(diag, b, zero_b), blockdiag(b))) + jnp.where(diag, zero_b, b)
    for sc in sup:
        auk = jnp.concatenate([sc["c0"]["auk"], sc["c1"]["auk"]], axis=1)
        vs2 = jnp.concatenate([sc["c0"]["vs"], sc["c1"]["vs"]], axis=1)
        sc["av"] = _dot(bf(auk), blockdiag(vs2))
    for sc in sup:
        sc["tsw"] = jnp.where(diag, zero_b, sc["b"])
    groups = {}
    for sc in sup:
        groups.setdefault((sc["d"], sc["q"]), []).append(sc)
    state = {(d, p): state_ref[d // 2, d % 2, p] for (d, q) in groups for p in (2 * q, 2 * q + 1)}
    nsub = len(next(iter(groups.values())))
    ys = {}
    for step in range(nsub):
        cur = {key: sorted(g, key=lambda sc: sc["s"], reverse=bool(key[0] % 2))[step]
               for key, g in groups.items()}
        for (d, q), sc in cur.items():
            s2 = jnp.concatenate([bf(state[(d, 2 * q)]), bf(state[(d, 2 * q + 1)])], axis=1)
            sc["ksrs"] = _dot_nt(sc["kr"], blockdiag(s2))
        for (d, q), sc in cur.items():
            g = sc["av"] + sc["ksrs"][:c2]
            sc["ub"] = bf(-_dot(sc["tsw"], blockdiag(bf(swap_halves(g)))))
        for (d, q), sc in cur.items():
            for i, ch in enumerate((sc["c0"], sc["c1"])):
                key = (d, 2 * q + i)
                uv = jnp.concatenate([sc["ub"][:, i * c2:(i + 1) * c2], ch["vs"]], axis=0)
                y = sc["ksrs"][c2:, i * c2:(i + 1) * c2] + _dot(ch["arr"], uv)
                state[key] = (state[key] + _dot_tn(uv, ch["bk"])) * ch["pc"]
                ys[(d, sc["s"], key[1])] = y[:c] + y[c:]
    for key, s_new in state.items():
        state_ref[key[0] // 2, key[0] % 2, key[1]] = s_new
    dirs = sorted({d for d, _ in state})
    n_pairs = len(state) // len(dirs)
    return [jnp.concatenate([jnp.concatenate([ys[(d, s, p)] for p in range(n_pairs)], axis=1)
                             for s in range(nsub)], axis=0) for d in dirs]


def _split_dot_left(m_bf16, x, passes):
    acc = None
    rem = x
    for _ in range(passes):
        piece = rem.astype(BF16)
        term = _dot(m_bf16, piece)
        acc = term if acc is None else acc + term
        rem = rem - piece.astype(F32)
    return acc


def _rwkv_solve_kernel(opf_ref, opb_ref, vf_ref, vb_ref, pcf_ref, pcb_ref, s0_ref, keep_ref, ioff_ref,
                       yf_ref, yb_ref, state_ref):
    @pl.when(pl.program_id(1) == 0)
    def _():
        state_ref[...] = s0_ref[...]

    chains = []
    for bi in range(opf_ref.shape[0]):
        chains += _rwkv_chains(opf_ref, vf_ref, pcf_ref, keep_ref[0], bi, 0)
        chains += _rwkv_chains(opb_ref, vb_ref, pcb_ref, keep_ref[1], bi, 1)
    ys = _rwkv_solve(chains, ioff_ref[...], state_ref, CHUNK)
    for bi in range(opf_ref.shape[0]):
        yf_ref[bi] = ys[2 * bi]
        yb_ref[bi] = ys[2 * bi + 1]


def _rwkv_masks(c):
    tt = (np.arange(4 * c) % c)[:, None]
    ss = (np.arange(4 * c) % c)[None, :]
    incl = (np.arange(4 * c) >= 2 * c)[:, None]
    keep = np.stack([np.where(incl, ss <= tt, ss < tt), np.where(incl, ss >= tt, ss > tt)])
    ioff = np.kron(np.array([[0.0, 1.0], [1.0, 0.0]]), np.eye(c))
    return jnp.asarray(keep.astype(np.float32)).astype(BF16), jnp.asarray(ioff, F32).astype(BF16)


def _rwkv_solve_call(opf, opb, vb, pcs, s0):
    b, t, hd = vb.shape
    rows = RWKV_BLOCK_CHUNKS * CHUNK
    keep, ioff = _rwkv_masks(CHUNK)
    nblk = t // rows
    nbt = RWKV_BATCH_ROWS if b % RWKV_BATCH_ROWS == 0 else 1
    fwd = lambda bb, j: (bb, j, 0)
    bwd = lambda bb, j: (bb, nblk - 1 - j, 0)
    st = pl.BlockSpec((nbt,) + s0.shape[1:], lambda bb, j: (bb, 0, 0, 0, 0))
    spec = lambda width, idx: pl.BlockSpec((nbt, rows, width), idx)
    pcspec = lambda idx: pl.BlockSpec((nbt, 8 * RWKV_BLOCK_CHUNKS, 2 * hd), idx)
    return pl.pallas_call(
        _rwkv_solve_kernel,
        grid=(b // nbt, nblk),
        in_specs=[spec(4 * hd, fwd), spec(4 * hd, bwd), spec(hd, fwd), spec(hd, bwd),
                  pcspec(fwd), pcspec(bwd), st,
                  pl.BlockSpec(keep.shape, lambda bb, j: (0, 0, 0)),
                  pl.BlockSpec(ioff.shape, lambda bb, j: (0, 0))],
        out_specs=[spec(hd, fwd), spec(hd, bwd), st],
        out_shape=[jax.ShapeDtypeStruct((b, t, hd), F32)] * 2 + [jax.ShapeDtypeStruct(s0.shape, F32)],
        compiler_params=_cparams(("parallel", "arbitrary")),
    )(opf, opb, vb, vb, pcs, pcs, s0, keep, ioff)


def _readout_kernel(att_ref, yf_ref, yb_ref, bonus_ref, gate_ref, x_ref, lg_ref, lb_ref, bd_ref,
                    wo_ref, gt_ref, g2_ref, sh_ref, sc_ref, xo_ref, h_ref):
    bd = bd_ref[...]
    y = yf_ref[...] + yb_ref[...]
    tm = y.shape[0]
    ysq = y * y
    y_hi = y.astype(BF16)
    q_hi = ysq.astype(BF16)
    parts = jnp.concatenate([y_hi, (y - y_hi.astype(F32)).astype(BF16),
                             q_hi, (ysq - q_hi.astype(F32)).astype(BF16)], axis=0)
    st = _dot(parts, bd)
    mean = st[:tm] + st[tm:2 * tm]
    var = st[2 * tm:3 * tm] + st[3 * tm:] - mean * mean
    yn = (y - mean) * lax.rsqrt(var + GN_EPS) * lg_ref[...] + lb_ref[...]
    rw = (yn + bonus_ref[...]) * gate_ref[...]
    cat = jnp.concatenate([att_ref[...], rw.astype(BF16)], axis=1)
    xm = x_ref[...] + gt_ref[0] * _dot(cat, wo_ref[...])
    xo_ref[...] = xm
    h_ref[...] = _rms_mod(xm, g2_ref[...], sh_ref[0], sc_ref[0]).astype(BF16)


def _readout_call(att, yf, yb, bonus, gate, x2, lnx_g, lnx_b, bd, w_out, gt1, g2, sh2, sc2,
                  rows_per_group, tm):
    r, d = x2.shape
    hd = yf.shape[1]
    per_group = rows_per_group // tm
    row = lambda i: (i, 0)
    c2 = lambda i: (0, 0)
    grp = lambda i: (i // per_group, 0, 0)
    half = pl.BlockSpec((tm, hd), row)
    full = pl.BlockSpec((tm, d), row)
    vec = pl.BlockSpec((1, 1, d), grp)
    return pl.pallas_call(
        _readout_kernel,
        grid=(r // tm,),
        in_specs=[half, half, half, half, half, full,
                  pl.BlockSpec((1, hd), c2), pl.BlockSpec((1, hd), c2), pl.BlockSpec(bd.shape, c2),
                  pl.BlockSpec(w_out.shape, c2), vec, pl.BlockSpec((1, d), c2), vec, vec],
        out_specs=[full, full],
        out_shape=[jax.ShapeDtypeStruct((r, d), F32), jax.ShapeDtypeStruct((r, d), BF16)],
        compiler_params=_cparams(("parallel",)),
    )(att, yf, yb, bonus, gate, x2, lnx_g, lnx_b, bd, w_out, gt1, g2, sh2, sc2)


def _mlp_kernel(h_ref, x_ref, w1_ref, w2_ref, gt_ref, gn_ref, shn_ref, scn_ref, xo_ref, ho_ref,
                acc_ref, *, final):
    j = pl.program_id(1)

    @pl.when(j == 0)
    def _():
        acc_ref[...] = jnp.zeros_like(acc_ref)

    a = jnp.maximum(_dot(h_ref[...], w1_ref[...]), 0.0)
    acc_ref[...] += _dot((a * a).astype(BF16), w2_ref[...])

    @pl.when(j == pl.num_programs(1) - 1)
    def _():
        xo = x_ref[...] + gt_ref[0] * acc_ref[...]
        if final:
            ms = jnp.mean(xo * xo, axis=-1, keepdims=True)
            xo_ref[...] = xo * lax.rsqrt(ms + NORM_EPS) * gn_ref[...]
            ho_ref[...] = jnp.zeros_like(ho_ref)
        else:
            xo_ref[...] = xo
            ho_ref[...] = _rms_mod(xo, gn_ref[...], shn_ref[0], scn_ref[0]).astype(BF16)


def _mlp_call(h, x2, w1, w2, gt2, g_next, sh_next, sc_next, rows_per_group, tm, tf, final):
    r, d = x2.shape
    ff = w1.shape[1]
    per_group = rows_per_group // tm
    row = lambda i, j: (i, 0)
    grp = lambda i, j: (i // per_group, 0, 0)
    vec = pl.BlockSpec((1, 1, d), grp)
    return pl.pallas_call(
        functools.partial(_mlp_kernel, final=final),
        grid=(r // tm, ff // tf),
        in_specs=[pl.BlockSpec((tm, d), row), pl.BlockSpec((tm, d), row),
                  pl.BlockSpec((d, tf), lambda i, j: (0, j)), pl.BlockSpec((tf, d), lambda i, j: (j, 0)),
                  vec, pl.BlockSpec((1, d), lambda i, j: (0, 0)), vec, vec],
        out_specs=[pl.BlockSpec((tm, d), row), pl.BlockSpec((tm, d), row)],
        out_shape=[jax.ShapeDtypeStruct((r, d), F32), jax.ShapeDtypeStruct((r, d), BF16)],
        scratch_shapes=[pltpu.VMEM((tm, d), F32)],
        compiler_params=_cparams(("parallel", "arbitrary")),
    )(h, x2, w1, w2, gt2, g_next, sh_next, sc_next)


def _fft1_kernel(h_ref, kr_ref, wc_ref, z_ref):
    l1, nb, d = h_ref.shape[1:]
    rows = l1 * nb
    hf = h_ref[0].reshape(rows, d)
    p = _dot(kr_ref[...], hf)
    gw = d // FOURIER_GROUPS
    wc = wc_ref[...]
    zr, zi = [], []
    for g in range(FOURIER_GROUPS):
        ap = jnp.concatenate([p[:rows, g * gw:(g + 1) * gw], p[rows:, g * gw:(g + 1) * gw]],
                             axis=1).astype(BF16)
        zz = _dot(ap, wc)
        zr.append(zz[:, :gw])
        zi.append(zz[:, gw:])
    z = jnp.concatenate(zr + zi, axis=1).astype(BF16)
    z_ref[0] = z.reshape(l1, nb, 2 * d)


def _fft1_call(h4, kr1, wc, nb):
    b, l1, l2, d = h4.shape
    return pl.pallas_call(
        _fft1_kernel,
        grid=(b, l2 // nb),
        in_specs=[pl.BlockSpec((1, l1, nb, d), lambda bb, j: (bb, 0, j, 0)),
                  pl.BlockSpec(kr1.shape, lambda bb, j: (0, 0)),
                  pl.BlockSpec(wc.shape, lambda bb, j: (0, 0))],
        out_specs=pl.BlockSpec((1, l1, nb, 2 * d), lambda bb, j: (bb, 0, j, 0)),
        out_shape=jax.ShapeDtypeStruct((b, l1, l2, 2 * d), BF16),
        compiler_params=_cparams(("parallel", "parallel")),
    )(h4, kr1, wc)


def _fft2_kernel(z_ref, gk_ref, x_ref, wo_ref, gt_ref, g2_ref, sh_ref, sc_ref, xo_ref, h_ref):
    mb, l2, d2 = z_ref.shape[1:]
    m2b = x_ref.shape[1]
    d = d2 // 2
    z = z_ref[0].reshape(mb * l2, d2)
    rhs = jnp.concatenate([z[:, :d], z[:, d:]], axis=0)
    f = _dot(gk_ref[0], rhs)
    y = _dot(f.astype(BF16), wo_ref[...])
    xm = x_ref[0].reshape(m2b * mb, d) + gt_ref[0] * y
    xo_ref[0] = xm.reshape(m2b, mb, d)
    h_ref[0] = _rms_mod(xm, g2_ref[...], sh_ref[0], sc_ref[0]).astype(BF16).reshape(m2b, mb, d)


def _fft2_call(z4, gk, x4, w_out, gt1, g2, sh2, sc2, mb, m2b):
    b, l1, l2, d2 = z4.shape
    d = d2 // 2
    nblk = l1 // mb
    xspec = pl.BlockSpec((1, m2b, mb, d), lambda m, bb, h: (bb, h, m, 0))
    vec = pl.BlockSpec((1, 1, d), lambda m, bb, h: (bb, 0, 0))
    return pl.pallas_call(
        _fft2_kernel,
        grid=(nblk, b, l2 // m2b),
        in_specs=[pl.BlockSpec((1, mb, l2, d2), lambda m, bb, h: (bb, m, 0, 0)),
                  pl.BlockSpec((1, m2b * mb, gk.shape[2]), lambda m, bb, h: (m, h, 0)),
                  xspec,
                  pl.BlockSpec(w_out.shape, lambda m, bb, h: (0, 0)),
                  vec, pl.BlockSpec((1, d), lambda m, bb, h: (0, 0)), vec, vec],
        out_specs=[xspec, xspec],
        out_shape=[jax.ShapeDtypeStruct((b, l2, l1, d), F32), jax.ShapeDtypeStruct((b, l2, l1, d), BF16)],
        compiler_params=_cparams(("parallel", "parallel", "parallel")),
    )(z4, gk, x4, w_out, gt1, g2, sh2, sc2)


def _rope_tables(t):
    axis_dim = HEAD_DIM // 2
    rows = t // GRID_W
    row = jnp.broadcast_to(jnp.arange(rows, dtype=F32)[:, None], (rows, GRID_W)).reshape(t)
    col = jnp.broadcast_to(jnp.arange(GRID_W, dtype=F32)[None, :], (rows, GRID_W)).reshape(t)
    inv = ROPE_BASE ** (-jnp.arange(0, axis_dim, 2, dtype=F32) / axis_dim)
    ang_r, ang_c = row[:, None] * inv, col[:, None] * inv
    cos = jnp.concatenate([jnp.cos(ang_r), jnp.cos(ang_r), jnp.cos(ang_c), jnp.cos(ang_c)], axis=1)
    sin = jnp.concatenate([-jnp.sin(ang_r), jnp.sin(ang_r), -jnp.sin(ang_c), jnp.sin(ang_c)], axis=1)
    return jnp.tile(cos, (1, 2)), jnp.tile(sin, (1, 2))


def _fft_tables(t, gw, nb, mb):
    l1 = t // FFT_L2
    n1 = np.arange(l1)
    ang1 = 2.0 * np.pi * np.outer(n1, n1) / l1
    eye = np.eye(nb)
    kr1 = np.concatenate([np.kron(np.cos(ang1), eye), np.kron(np.sin(ang1), eye)], axis=0)
    ch = np.arange(gw)
    angc = 2.0 * np.pi * np.outer(ch, ch) / gw
    cg, sg = np.cos(angc), np.sin(angc)
    wc = np.block([[cg, -sg], [-sg, -cg]])
    scale = 1.0 / np.sqrt(float(t) * gw)
    m = np.arange(t)
    n2 = np.arange(FFT_L2)
    theta = 2.0 * np.pi * np.outer(m, n2) / t
    cs = np.stack([np.cos(theta), np.sin(theta)], axis=0) * scale
    cs = cs.reshape(2, FFT_L2, l1 // mb, mb, FFT_L2)
    return (jnp.asarray(kr1, F32).astype(BF16), jnp.asarray(wc, F32).astype(BF16),
            jnp.asarray(cs, F32))


def _expand_gk(cs, mb):
    eye = jnp.eye(mb, dtype=F32)
    g = jnp.einsum("rmbpn,pq->bmprqn", cs, eye)
    nblk = cs.shape[2]
    return g.reshape(nblk, FFT_L2 * mb, 2 * mb * FFT_L2).astype(BF16)


def kernel(x, c, ctx, c_ctx, ada_w, ada_b, norm1_g, norm2_g, mix_w_in, mix_w_out, attn_sink,
           shift_mu_prev, shift_mu_next, decay_w0, decay_w2, iclr_a0, iclr_a2, gate_g2, key_kk,
           key_ka, bonus_rk, lnx_g, lnx_b, fourier_w_out, mlp_w1, mlp_w2, final_g):
    b, t, d = x.shape
    nctx = ctx.shape[1]
    hd = key_kk.shape[1]
    q_dim = d - hd
    n_heads = q_dim // HEAD_DIM
    kv_dim = (n_heads // 4) * HEAD_DIM
    att_cols = q_dim + 2 * kv_dim

    cond = jnp.zeros((8, d), F32).at[:b].set(c).at[b].set(c_ctx)
    mods = _ada_call(cond, ada_w, ada_b)
    lat = [mods[i, :b].reshape(b, N_MOD, 1, d) for i in range(2)]
    cmod = [mods[i, b:b + 1].reshape(1, N_MOD, 1, d) for i in range(2)]
    lm = lambda i, k: lat[i][:, k]
    cm = lambda i, k: cmod[i][:, k]
    row1 = lambda a: a.reshape(1, -1)

    w_in = mix_w_in[0]
    wk = w_in[:, q_dim:q_dim + kv_dim].reshape(d, kv_dim // HEAD_DIM, 1, HEAD_DIM)
    wv = w_in[:, q_dim + kv_dim:att_cols].reshape(d, kv_dim // HEAD_DIM, 1, HEAD_DIM)
    dup = lambda w: jnp.broadcast_to(w, (d, kv_dim // HEAD_DIM, 2, HEAD_DIM)).reshape(d, 2 * kv_dim)
    w_att = jnp.concatenate([w_in[:, :q_dim], dup(wk), dup(wv)], axis=1).astype(BF16)
    w_rw = w_in[:, att_cols:].astype(BF16)
    cos_t, sin_t = _rope_tables(t)
    cos_c, sin_c = jnp.ones((nctx, LANES), F32), jnp.zeros((nctx, LANES), F32)
    g1 = row1(norm1_g[0])
    x2 = x.reshape(b * t, d)
    ctx2 = ctx.reshape(b * nctx, d)
    tm_in = min(512, t)
    q, kd, vd, zrw = _inproj_call(x2, g1, lm(0, 0), lm(0, 1), cos_t, sin_t, w_att, w_rw, t, tm_in)
    qc, kc, vc, zrwc = _inproj_call(ctx2, g1, cm(0, 0), cm(0, 1), cos_c, sin_c, w_att, w_rw,
                                    b * nctx, nctx)
    q, kd, vd = (a.reshape(b, t, -1) for a in (q, kd, vd))
    qc, kc, vc = (a.reshape(b, nctx, -1) for a in (qc, kc, vc))
    sinkb = jnp.broadcast_to(attn_sink[0][:, None] * LOG2E, (n_heads, LANES)).astype(F32)
    att = _attn_call(q, kd, vd, kc, vc, sinkb, 3)
    att_c = _attn_call(qc, kc, vc, kc, vc, sinkb, 0)

    mu = jnp.stack([shift_mu_prev[0], shift_mu_next[0], 1.0 - shift_mu_prev[0] - shift_mu_next[0]])
    zl = jnp.zeros((DECAY_LORA, hd), F32)
    wa = jnp.stack([jnp.concatenate([jnp.concatenate([decay_w2[0, dd], zl], axis=1),
                                     jnp.concatenate([zl, iclr_a2[0, dd]], axis=1)], axis=0)
                    for dd in range(2)])
    wa = (jnp.concatenate([wa[0], wa[1]], axis=1) * LOG2E).astype(BF16)
    w0a0 = jnp.concatenate([decay_w0[0], iclr_a0[0]], axis=1).reshape(1, 4 * hd) * LOG2E
    seg = np.arange(hd) // RWKV_N
    bd = jnp.asarray(seg[:, None] == seg[None, :], F32).astype(BF16)
    k_k, r_k = row1(key_kk[0]), row1(bonus_rk[0])
    k_a = jnp.stack([1.0 - key_ka[0], key_ka[0]])
    g2w = gate_g2[0].astype(BF16)
    s_zero = jnp.zeros((b, 2, hd // LANES, LANES, LANES), F32)
    opf_c, opb_c, vb_c, pcs_c, bonus_c, gate_c = _rwkv_prep_call(zrwc.reshape(b, nctx, -1), mu, wa, w0a0,
                                                                 k_k, k_a, r_k, g2w, bd)
    opf, opb, vb, pcs, bonus, gate = _rwkv_prep_call(zrw.reshape(b, t, -1), mu, wa, w0a0,
                                                     k_k, k_a, r_k, g2w, bd)
    yfc, ybc, s_ctx = _rwkv_solve_call(opf_c, opb_c, vb_c, pcs_c, s_zero)
    yf, yb, _ = _rwkv_solve_call(opf, opb, vb, pcs, s_ctx)

    w_out = mix_w_out[0].astype(BF16)
    n2g = row1(norm2_g[0])
    flat = lambda a: a.reshape(-1, a.shape[-1])
    bd_mean = (jnp.asarray(seg[:, None] == seg[None, :], F32) * (1.0 / RWKV_N)).astype(BF16)
    xm, h2 = _readout_call(flat(att), flat(yf), flat(yb), flat(bonus), flat(gate), x2,
                           row1(lnx_g[0]), row1(lnx_b[0]), bd_mean, w_out, lm(0, 2), n2g,
                           lm(0, 3), lm(0, 4), t, tm_in)
    xmc, h2c = _readout_call(flat(att_c), flat(yfc), flat(ybc), flat(bonus_c), flat(gate_c), ctx2,
                             row1(lnx_g[0]), row1(lnx_b[0]), bd_mean, w_out, cm(0, 2), n2g,
                             cm(0, 3), cm(0, 4), b * nctx, nctx)
    w1 = mlp_w1[0].astype(BF16)
    w2 = mlp_w2[0].astype(BF16)
    g1n = row1(norm1_g[1])
    tm_mlp = min(1024, t)
    x1, h1 = _mlp_call(h2, xm, w1, w2, lm(0, 5), g1n, lm(1, 0), lm(1, 1), t, tm_mlp, MLP_TF, False)
    ctx1, _ = _mlp_call(h2c, xmc, w1, w2, cm(0, 5), g1n, cm(1, 0), cm(1, 1), b * nctx, nctx, MLP_TF,
                        False)
    del ctx1

    l1 = t // FFT_L2
    nb = 8
    mb = min(8, l1)
    gw = d // FOURIER_GROUPS
    kr1, wc, cs = _fft_tables(t, gw, nb, mb)
    gk = _expand_gk(cs, mb)
    z4 = _fft1_call(h1.reshape(b, l1, FFT_L2, d), kr1, wc, nb)
    xm4, h24 = _fft2_call(z4, gk, x1.reshape(b, FFT_L2, l1, d), fourier_w_out[0].astype(BF16),
                          lm(1, 2), row1(norm2_g[1]), lm(1, 3), lm(1, 4), mb, 64)
    out, _ = _mlp_call(h24.reshape(b * t, d), xm4.reshape(b * t, d), mlp_w1[1].astype(BF16),
                       mlp_w2[1].astype(BF16), lm(1, 5), row1(final_g), lm(1, 0), lm(1, 1),
                       t, tm_mlp, MLP_TF, True)
    return out.reshape(b, t, d)
```

```python
import functools

import numpy as np
import jax
import jax.numpy as jnp
from jax import lax
from jax.experimental import pallas as pl
from jax.experimental.pallas import tpu as pltpu

F32 = jnp.float32
BF16 = jnp.bfloat16
HIGHEST = lax.Precision.HIGHEST

HEAD_DIM = 64
WINDOW = 128
QBLK = 128
ATTN_QBLOCKS = 4
GRID_W = 64
ROPE_BASE = 10000.0
RWKV_N = 64
DECAY_LORA = 64
ICLR_LORA = 64
GATE_LORA = 128
FOURIER_GROUPS = 4
N_MOD = 6
NORM_EPS = 1e-6
GN_EPS = 64e-5
NEG_INF = -1e30

CHUNK = 64
RWKV_PREP_ROWS = 512
RWKV_PREP_SUB = 256
RWKV_BLOCK_CHUNKS = 4
RWKV_BATCH_ROWS = 1
LANES = 128
FFT_L2 = 128
MLP_TF = 1024
VMEM_LIMIT = 48 * 1024 * 1024
LOG2E = 1.4426950408889634
Q_SCALE = HEAD_DIM ** -0.5 * LOG2E
W_LOG2_OFFSET = -0.5 * LOG2E + float(np.log2(LOG2E))


def _cparams(sem):
    return pltpu.CompilerParams(dimension_semantics=sem, vmem_limit_bytes=VMEM_LIMIT)


def _dot(a, b, **kw):
    return jnp.dot(a, b, preferred_element_type=F32, **kw)


def _dot_nt(a, b):
    return lax.dot_general(a, b, (((1,), (1,)), ((), ())), preferred_element_type=F32)


def _dot_tn(a, b):
    return lax.dot_general(a, b, (((0,), (0,)), ((), ())), preferred_element_type=F32)


def _split_dot(x, m_bf16, passes):
    acc = None
    rem = x
    for _ in range(passes):
        piece = rem.astype(BF16)
        term = _dot(piece, m_bf16)
        acc = term if acc is None else acc + term
        rem = rem - piece.astype(F32)
    return acc


def _rms_mod(x, g, sh, sc):
    ms = jnp.mean(x * x, axis=-1, keepdims=True)
    return (x * lax.rsqrt(ms + NORM_EPS)) * (g * (1.0 + sc)) + sh


def _ada_kernel(cond_ref, w_ref, b_ref, o_ref):
    s = cond_ref[...]
    s = s * jax.nn.sigmoid(s)
    o_ref[0] = _dot(s, w_ref[0], precision=HIGHEST) + b_ref[0]


def _ada_call(cond, ada_w, ada_b):
    depth, d, n = ada_w.shape
    tn = 1536
    return pl.pallas_call(
        _ada_kernel,
        grid=(depth, n // tn),
        in_specs=[pl.BlockSpec((8, d), lambda i, j: (0, 0)),
                  pl.BlockSpec((1, d, tn), lambda i, j: (i, 0, j)),
                  pl.BlockSpec((1, 1, tn), lambda i, j: (i, 0, j))],
        out_specs=pl.BlockSpec((1, 8, tn), lambda i, j: (i, 0, j)),
        out_shape=jax.ShapeDtypeStruct((depth, 8, n), F32),
        compiler_params=_cparams(("parallel", "parallel")),
    )(cond, ada_w, ada_b.reshape(depth, 1, n))


def _inproj_kernel(x_ref, g_ref, sh_ref, sc_ref, cos_ref, sin_ref, wa_ref, wr_ref,
                   q_ref, k_ref, v_ref, z_ref):
    h = _rms_mod(x_ref[...], g_ref[...], sh_ref[0], sc_ref[0]).astype(BF16)
    z_ref[...] = _dot(h, wr_ref[...])
    za = _dot(h, wa_ref[...])
    cos = cos_ref[...]
    sin = sin_ref[...]
    lane = lax.broadcasted_iota(jnp.int32, cos.shape, 1)
    first = (lane % 32) < 16
    nq = q_ref.shape[1] // LANES
    nk = k_ref.shape[1] // LANES
    for c in range(nq + nk):
        s = za[:, c * LANES:(c + 1) * LANES]
        partner = jnp.where(first, pltpu.roll(s, LANES - 16, 1), pltpu.roll(s, 16, 1))
        ro = s * cos + partner * sin
        if c < nq:
            q_ref[:, c * LANES:(c + 1) * LANES] = (ro * Q_SCALE).astype(BF16)
        else:
            k_ref[:, (c - nq) * LANES:(c - nq + 1) * LANES] = ro.astype(BF16)
    v_ref[...] = za[:, (nq + nk) * LANES:].astype(BF16)


def _inproj_call(x2, g1, sh, sc, cos, sin, w_att, w_rw, rows_per_group, tm):
    r, d = x2.shape
    period = cos.shape[0]
    n_per = period // tm
    per_group = rows_per_group // tm
    na = w_att.shape[1]
    nr = w_rw.shape[1]
    nq, nkd = 512, 256
    row = lambda i: (i, 0)
    grp = lambda i: (i // per_group, 0, 0)
    return pl.pallas_call(
        _inproj_kernel,
        grid=(r // tm,),
        in_specs=[pl.BlockSpec((tm, d), row),
                  pl.BlockSpec((1, d), lambda i: (0, 0)),
                  pl.BlockSpec((1, 1, d), grp),
                  pl.BlockSpec((1, 1, d), grp),
                  pl.BlockSpec((tm, LANES), lambda i: (i % n_per, 0)),
                  pl.BlockSpec((tm, LANES), lambda i: (i % n_per, 0)),
                  pl.BlockSpec((d, na), lambda i: (0, 0)),
                  pl.BlockSpec((d, nr), lambda i: (0, 0))],
        out_specs=[pl.BlockSpec((tm, nq), row), pl.BlockSpec((tm, nkd), row),
                   pl.BlockSpec((tm, nkd), row), pl.BlockSpec((tm, nr), row)],
        out_shape=[jax.ShapeDtypeStruct((r, nq), BF16), jax.ShapeDtypeStruct((r, nkd), BF16),
                   jax.ShapeDtypeStruct((r, nkd), BF16), jax.ShapeDtypeStruct((r, nr), F32)],
        compiler_params=_cparams(("parallel",)),
    )(x2, g1, sh, sc, cos, sin, w_att, w_rw)


def _attn_kernel(*refs, local, qpb):
    q_ref = refs[0]
    nk = qpb + 2 if local else 0
    k_refs = refs[1:1 + nk]
    v_refs = refs[1 + nk:1 + 2 * nk]
    kc_ref, vc_ref, sink_ref, o_ref = refs[1 + 2 * nk:]
    i = pl.program_id(1)
    nb = pl.num_programs(1) * qpb
    nctx = kc_ref.shape[1]
    nkeys = (3 * QBLK if local else 0) + nctx
    gq = 4
    rows = gq * QBLK
    low = lax.broadcasted_iota(jnp.int32, (QBLK, LANES), 1) < HEAD_DIM
    if local:
        rq = lax.broadcasted_iota(jnp.int32, (rows, QBLK), 0) % QBLK
        ck = lax.broadcasted_iota(jnp.int32, (rows, QBLK), 1)
    zero = jnp.zeros((QBLK, LANES), BF16)
    ones = jnp.ones((nkeys, LANES), BF16)
    n_groups = q_ref.shape[2] // (gq * HEAD_DIM)
    units = [(qi, g) for qi in range(qpb) for g in range(n_groups)]
    scores, vals, sinks = {}, {}, {}
    for qi, g in units:
        ksl = slice(g * LANES, (g + 1) * LANES)
        keys = jnp.concatenate([kr[0, :, ksl] for kr in k_refs[qi:qi + 3]] + [kc_ref[0, :, ksl]], axis=0)
        vals[qi, g] = jnp.concatenate(
            [jnp.concatenate([vr[0, :, ksl] for vr in v_refs[qi:qi + 3]] + [vc_ref[0, :, ksl]], axis=0), ones],
            axis=1)
        qs = []
        for pp in range(2):
            qp = q_ref[0, qi * QBLK:(qi + 1) * QBLK, (2 * g + pp) * LANES:(2 * g + pp + 1) * LANES]
            qs += [jnp.where(low, qp, zero), jnp.where(low, zero, qp)]
        scores[qi, g] = _dot_nt(jnp.concatenate(qs, axis=0), keys)
        sinks[g] = jnp.concatenate(
            [jnp.broadcast_to(sink_ref[gq * g + h:gq * g + h + 1, 0:1], (QBLK, 1)) for h in range(gq)], axis=0)
    probs, ms = {}, {}
    for qi, g in units:
        s = scores[qi, g]
        if local:
            blk = i * qpb + qi
            mask_prev = ck >= rq + jnp.where(blk >= 1, 0, QBLK)
            mask_next = ck <= rq - jnp.where(blk <= nb - 2, 0, QBLK)
            s = jnp.concatenate([jnp.where(mask_prev, s[:, :QBLK], NEG_INF), s[:, QBLK:2 * QBLK],
                                 jnp.where(mask_next, s[:, 2 * QBLK:3 * QBLK], NEG_INF), s[:, 3 * QBLK:]],
                                axis=1)
        m = jnp.maximum(jnp.max(s, axis=-1, keepdims=True), sinks[g])
        ms[qi, g] = m
        probs[qi, g] = jnp.exp2(s - m).astype(BF16)
    for qi, g in units:
        o = _dot(probs[qi, g], vals[qi, g])
        out = o[:, :LANES] / (o[:, LANES:] + jnp.exp2(sinks[g] - ms[qi, g]))
        for pp in range(2):
            even = out[2 * pp * QBLK:(2 * pp + 1) * QBLK]
            odd = out[(2 * pp + 1) * QBLK:(2 * pp + 2) * QBLK]
            p = 2 * g + pp
            o_ref[0, qi * QBLK:(qi + 1) * QBLK, p * LANES:(p + 1) * LANES] = (
                jnp.where(low, even, odd).astype(BF16))


def _attn_call(q, kd, vd, kc, vc, sinkb, n_loc):
    b, t, nq = q.shape
    nb = t // QBLK
    qpb = min(ATTN_QBLOCKS, nb)
    nctx = kc.shape[1]
    kw = kd.shape[2]
    qspec = pl.BlockSpec((1, qpb * QBLK, nq), lambda bb, i: (bb, i, 0))
    loc = []
    for off in range(-1, qpb + 1) if n_loc else ():
        loc.append(pl.BlockSpec((1, QBLK, kw), functools.partial(
            lambda bb, i, off: (bb, jnp.clip(i * qpb + off, 0, nb - 1), 0), off=off)))
    cspec = pl.BlockSpec((1, nctx, kw), lambda bb, i: (bb, 0, 0))
    args = [q] + [kd] * len(loc) + [vd] * len(loc) + [kc, vc, sinkb]
    return pl.pallas_call(
        functools.partial(_attn_kernel, local=bool(n_loc), qpb=qpb),
        grid=(b, nb // qpb),
        in_specs=[qspec] + loc + loc + [cspec, cspec, pl.BlockSpec(sinkb.shape, lambda bb, i: (0, 0))],
        out_specs=qspec,
        out_shape=jax.ShapeDtypeStruct((b, t, nq), BF16),
        compiler_params=_cparams(("parallel", "parallel")),
    )(*args)


def _rwkv_prep_kernel(z_ref, zp_ref, zn_ref, mu_ref, wa_ref, w0a0_ref, kk_ref, ka_ref, rk_ref,
                      g2_ref, bd_ref, tri_ref, opf_ref, opb_ref, v_ref, pc_ref, bonus_ref, gate_ref):
    j = pl.program_id(1)
    nblk = pl.num_programs(1)
    z = z_ref[0]
    nrows = z.shape[0]
    c = CHUNK
    hd = kk_ref.shape[1]
    mu = mu_ref[...]
    row8 = lax.broadcasted_iota(jnp.int32, (8, z.shape[1]), 0)
    zp = pltpu.roll(z, 1, 0)
    zp = jnp.concatenate([jnp.where(row8 == 0, jnp.where(j == 0, 0.0, zp_ref[0, 7:8]), zp[:8]), zp[8:]],
                         axis=0)
    zn = pltpu.roll(z, nrows - 1, 0)
    zn = jnp.concatenate([zn[:nrows - 8],
                          jnp.where(row8 == 7, jnp.where(j == nblk - 1, 0.0, zn_ref[0, 0:1]), zn[nrows - 8:])],
                         axis=0)
    zs = mu[2:3] * z + mu[0:1] * zp + mu[1:2] * zn
    bd = bd_ref[...]
    sub = tri_ref.shape[1]
    nsub = sub // c
    parts = []
    for h in range(nrows // sub):
        rows = slice(h * sub, (h + 1) * sub)
        r = zs[rows, 0:hd]
        k = zs[rows, hd:2 * hd]
        v = zs[rows, 2 * hd:3 * hd]
        wa_in = zs[rows, 3 * hd:3 * hd + LANES]
        gl = zs[rows, 3 * hd + LANES:]
        v_ref[0, rows] = v.astype(BF16)
        bonus_ref[0, rows] = _split_dot(r * k * rk_ref[...], bd, 1) * v
        gate_ref[0, rows] = _dot(jax.nn.sigmoid(gl).astype(BF16), g2_ref[...])
        low_r = lax.broadcasted_iota(jnp.int32, (sub, LANES), 1) < DECAY_LORA
        tw = jnp.where(low_r, jnp.tanh(wa_in), wa_in)
        xwa2 = _split_dot(tw, wa_ref[...], 2) + w0a0_ref[...]
        kkr = k * kk_ref[...]
        kk = kkr * lax.rsqrt(_split_dot(kkr * kkr, bd, 1) + 1e-12)
        parts.append(dict(rows=rows, h=h, r=r, k=k, kk=kk, xwa2=xwa2))
    for pt in parts:
        pt["dirs"] = []
        for d in range(2):
            xs = pt["xwa2"][:, 2 * d * hd:(2 * d + 1) * hd]
            a = 0.5 + 0.5 * jnp.tanh(pt["xwa2"][:, (2 * d + 1) * hd:(2 * d + 2) * hd])
            w_l2 = jnp.minimum(xs, 0.0) - jnp.log(1.0 + jnp.exp2(-jnp.abs(xs))) * LOG2E + W_LOG2_OFFSET
            lw = -jnp.exp2(w_l2)
            kd = pt["k"] * (ka_ref[0:1] + a * ka_ref[1:2])
            bb = pt["kk"] * a
            cum = _split_dot_left(tri_ref[d], lw, 2)
            pt["dirs"].append((lw, kd, bb, cum))
    for pt in parts:
        for d, op_ref in enumerate((opf_ref, opb_ref)):
            lw, kd, bb, cum = pt["dirs"][d]
            e_in = jnp.exp2(cum)
            e_ex = jnp.exp2(cum - lw)
            e_neg = jnp.exp2(-cum)
            op_ref[0, pt["rows"]] = jnp.concatenate(
                [pt["kk"] * e_ex, pt["r"] * e_in, bb * e_neg, kd * e_neg], axis=1).astype(BF16)
            for s in range(nsub):
                total = cum[s * c:s * c + 1] if d else cum[s * c + c - 1:s * c + c]
                row0 = 8 * (pt["h"] * nsub + s)
                pc_ref[0, row0:row0 + 8, d * hd:(d + 1) * hd] = jnp.broadcast_to(jnp.exp2(total), (8, hd))


def _rwkv_prep_call(z, mu, wa2, w0a02, k_k, k_a, r_k, g2, bd):
    b, t, nz = z.shape
    rows = min(RWKV_PREP_ROWS, t)
    sub_chunks = min(RWKV_PREP_SUB, rows) // CHUNK
    nsub = rows // CHUNK
    ti = np.arange(CHUNK)
    tri = np.stack([np.kron(np.eye(sub_chunks), ti[None, :] <= ti[:, None]),
                    np.kron(np.eye(sub_chunks), ti[None, :] >= ti[:, None])]).astype(np.float32)
    tri = jnp.asarray(tri, F32).astype(BF16)
    hd = k_k.shape[1]
    nblk = t // rows
    cb = rows // 8
    nb8 = t // 8
    blk = lambda bb, j: (bb, j, 0)
    const2 = lambda bb, j: (0, 0)
    const3 = lambda bb, j: (0, 0, 0)
    full = lambda a: pl.BlockSpec(a.shape, const2 if a.ndim == 2 else const3)
    return pl.pallas_call(
        _rwkv_prep_kernel,
        grid=(b, nblk),
        in_specs=[pl.BlockSpec((1, rows, nz), blk),
                  pl.BlockSpec((1, 8, nz), lambda bb, j: (bb, jnp.maximum(j * cb - 1, 0), 0)),
                  pl.BlockSpec((1, 8, nz), lambda bb, j: (bb, jnp.minimum((j + 1) * cb, nb8 - 1), 0)),
                  full(mu), full(wa2), full(w0a02), full(k_k), full(k_a), full(r_k), full(g2), full(bd),
                  full(tri)],
        out_specs=[pl.BlockSpec((1, rows, 4 * hd), blk), pl.BlockSpec((1, rows, 4 * hd), blk),
                   pl.BlockSpec((1, rows, hd), blk), pl.BlockSpec((1, 8 * nsub, 2 * hd), blk),
                   pl.BlockSpec((1, rows, hd), blk), pl.BlockSpec((1, rows, hd), blk)],
        out_shape=[jax.ShapeDtypeStruct((b, t, 4 * hd), BF16), jax.ShapeDtypeStruct((b, t, 4 * hd), BF16),
                   jax.ShapeDtypeStruct((b, t, hd), BF16),
                   jax.ShapeDtypeStruct((b, 8 * (t // CHUNK), 2 * hd), F32),
                   jax.ShapeDtypeStruct((b, t, hd), F32), jax.ShapeDtypeStruct((b, t, hd), F32)],
        compiler_params=_cparams(("parallel", "parallel")),
    )(z, z, z, mu, wa2, w0a02, k_k, k_a, r_k, g2, bd, tri)


def _rwkv_chains(op_ref, v_ref, pc_ref, keep, bi, d):
    c = CHUNK
    hd = v_ref.shape[2]
    nsub = op_ref.shape[1] // c
    low = lax.broadcasted_iota(jnp.int32, (c, LANES), 1) < RWKV_N

    def stack(xp):
        zero = jnp.zeros_like(xp)
        return jnp.concatenate([jnp.where(low, xp, zero), jnp.where(low, zero, xp)], axis=0)

    chains = []
    for s in range(nsub):
        rows = slice(s * c, (s + 1) * c)
        for p in range(hd // LANES):
            sb = lambda i: stack(op_ref[bi, rows, i * hd + p * LANES:i * hd + (p + 1) * LANES])
            chains.append(dict(
                d=2 * bi + d, p=p, s=s, keep=keep,
                pc=pc_ref[bi, 8 * s:8 * s + 1, d * hd + p * LANES:d * hd + (p + 1) * LANES],
                kts=sb(0), rtb=sb(1), bts=sb(2), kdts=sb(3),
                vs=stack(v_ref[bi, rows, p * LANES:(p + 1) * LANES])))
    return chains


def _rwkv_solve(chains, ioff, state_ref, c):
    c2 = 2 * c
    bf = lambda a: a.astype(BF16)
    rr = lax.broadcasted_iota(jnp.int32, (c2, 2 * c2), 0)
    cc = lax.broadcasted_iota(jnp.int32, (c2, 2 * c2), 1)
    diag = (rr // c) == ((cc // c) % 2)

    def blockdiag(ab):
        n = ab.shape[1] // 2
        z = jnp.zeros((ab.shape[0], n), ab.dtype)
        return jnp.concatenate([jnp.concatenate([ab[:, :n], z], axis=1),
                                jnp.concatenate([z, ab[:, n:]], axis=1)], axis=0)

    def swap_halves(g):
        return jnp.concatenate([g[c:], g[:c]], axis=0)

    by_key = {(ch["d"], ch["s"], ch["p"]): ch for ch in chains}
    supers = [(by_key[(d, s, p)], by_key[(d, s, p + 1)])
              for (d, s, p) in sorted(by_key) if p % 2 == 0]
    sup = [dict(c0=a, c1=b, d=a["d"], s=a["s"], q=a["p"] // 2) for a, b in supers]
    for ch in chains:
        ch["kr"] = jnp.concatenate([ch["kts"], ch["rtb"]], axis=0)
        ch["bk"] = jnp.concatenate([ch["bts"], ch["kdts"]], axis=0)
        aa = bf(_dot_nt(ch["kr"], ch["bk"])) * ch["keep"]
        ch["auk"] = aa[:c2, c2:]
        ch["arr"] = aa[c2:, :]
        ch["b0"] = ioff - aa[:c2, :c2]
    zero_b = jnp.zeros((c2, 2 * c2), BF16)
    for sc in sup:
        sc["b"] = jnp.concatenate([sc["c0"]["b0"], sc["c1"]["b0"]], axis=1)
        sc["kr"] = jnp.concatenate([sc["c0"]["kr"], sc["c1"]["kr"]], axis=1)
    for _ in range(6):
        for sc in sup:
            b = sc["b"]
            sc["b"] = bf(_dot(jnp.where(diag, b, zero_b), blockdiag(b))) + jnp.where(diag, zero_b, b)
    for sc in sup:
        auk = jnp.concatenate([sc["c0"]["auk"], sc["c1"]["auk"]], axis=1)
        vs2 = jnp.concatenate([sc["c0"]["vs"], sc["c1"]["vs"]], axis=1)
        sc["av"] = _dot(bf(auk), blockdiag(vs2))
    for sc in sup:
        sc["tsw"] = jnp.where(diag, zero_b, sc["b"])
    groups = {}
    for sc in sup:
        groups.setdefault((sc["d"], sc["q"]), []).append(sc)
    state = {(d, p): state_ref[d // 2, d % 2, p] for (d, q) in groups for p in (2 * q, 2 * q + 1)}
    nsub = len(next(iter(groups.values())))
    ys = {}
    for step in range(nsub):
        cur = {key: sorted(g, key=lambda sc: sc["s"], reverse=bool(key[0] % 2))[step]
               for key, g in groups.items()}
        for (d, q), sc in cur.items():
            s2 = jnp.concatenate([bf(state[(d, 2 * q)]), bf(state[(d, 2 * q + 1)])], axis=1)
            sc["ksrs"] = _dot_nt(sc["kr"], blockdiag(s2))
        for (d, q), sc in cur.items():
            g = sc["av"] + sc["ksrs"][:c2]
            sc["ub"] = bf(-_dot(sc["tsw"], blockdiag(bf(swap_halves(g)))))
        for (d, q), sc in cur.items():
            for i, ch in enumerate((sc["c0"], sc["c1"])):
                key = (d, 2 * q + i)
                uv = jnp.concatenate([sc["ub"][:, i * c2:(i + 1) * c2], ch["vs"]], axis=0)
                y = sc["ksrs"][c2:, i * c2:(i + 1) * c2] + _dot(ch["arr"], uv)
                state[key] = (state[key] + _dot_tn(uv, ch["bk"])) * ch["pc"]
                ys[(d, sc["s"], key[1])] = y[:c] + y[c:]
    for key, s_new in state.items():
        state_ref[key[0] // 2, key[0] % 2, key[1]] = s_new
    dirs = sorted({d for d, _ in state})
    n_pairs = len(state) // len(dirs)
    return [jnp.concatenate([jnp.concatenate([ys[(d, s, p)] for p in range(n_pairs)], axis=1)
                             for s in range(nsub)], axis=0) for d in dirs]


def _split_dot_left(m_bf16, x, passes):
    acc = None
    rem = x
    for _ in range(passes):
        piece = rem.astype(BF16)
        term = _dot(m_bf16, piece)
        acc = term if acc is None else acc + term
        rem = rem - piece.astype(F32)
    return acc


def _rwkv_solve_kernel(opf_ref, opb_ref, vf_ref, vb_ref, pcf_ref, pcb_ref, s0_ref, keep_ref, ioff_ref,
                       yf_ref, yb_ref, state_ref):
    @pl.when(pl.program_id(1) == 0)
    def _():
        state_ref[...] = s0_ref[...]

    chains = []
    for bi in range(opf_ref.shape[0]):
        chains += _rwkv_chains(opf_ref, vf_ref, pcf_ref, keep_ref[0], bi, 0)
        chains += _rwkv_chains(opb_ref, vb_ref, pcb_ref, keep_ref[1], bi, 1)
    ys = _rwkv_solve(chains, ioff_ref[...], state_ref, CHUNK)
    for bi in range(opf_ref.shape[0]):
        yf_ref[bi] = ys[2 * bi]
        yb_ref[bi] = ys[2 * bi + 1]


def _rwkv_masks(c):
    tt = (np.arange(4 * c) % c)[:, None]
    ss = (np.arange(4 * c) % c)[None, :]
    incl = (np.arange(4 * c) >= 2 * c)[:, None]
    keep = np.stack([np.where(incl, ss <= tt, ss < tt), np.where(incl, ss >= tt, ss > tt)])
    ioff = np.kron(np.array([[0.0, 1.0], [1.0, 0.0]]), np.eye(c))
    return jnp.asarray(keep.astype(np.float32)).astype(BF16), jnp.asarray(ioff, F32).astype(BF16)


def _rwkv_solve_call(opf, opb, vb, pcs, s0):
    b, t, hd = vb.shape
    rows = RWKV_BLOCK_CHUNKS * CHUNK
    keep, ioff = _rwkv_masks(CHUNK)
    nblk = t // rows
    nbt = RWKV_BATCH_ROWS if b % RWKV_BATCH_ROWS == 0 else 1
    fwd = lambda bb, j: (bb, j, 0)
    bwd = lambda bb, j: (bb, nblk - 1 - j, 0)
    st = pl.BlockSpec((nbt,) + s0.shape[1:], lambda bb, j: (bb, 0, 0, 0, 0))
    spec = lambda width, idx: pl.BlockSpec((nbt, rows, width), idx)
    pcspec = lambda idx: pl.BlockSpec((nbt, 8 * RWKV_BLOCK_CHUNKS, 2 * hd), idx)
    return pl.pallas_call(
        _rwkv_solve_kernel,
        grid=(b // nbt, nblk),
        in_specs=[spec(4 * hd, fwd), spec(4 * hd, bwd), spec(hd, fwd), spec(hd, bwd),
                  pcspec(fwd), pcspec(bwd), st,
                  pl.BlockSpec(keep.shape, lambda bb, j: (0, 0, 0)),
                  pl.BlockSpec(ioff.shape, lambda bb, j: (0, 0))],
        out_specs=[spec(hd, fwd), spec(hd, bwd), st],
        out_shape=[jax.ShapeDtypeStruct((b, t, hd), F32)] * 2 + [jax.ShapeDtypeStruct(s0.shape, F32)],
        compiler_params=_cparams(("parallel", "arbitrary")),
    )(opf, opb, vb, vb, pcs, pcs, s0, keep, ioff)


def _readout_kernel(att_ref, yf_ref, yb_ref, bonus_ref, gate_ref, x_ref, lg_ref, lb_ref, bd_ref,
                    wo_ref, gt_ref, g2_ref, sh_ref, sc_ref, xo_ref, h_ref):
    bd = bd_ref[...]
    y = yf_ref[...] + yb_ref[...]
    tm = y.shape[0]
    ysq = y * y
    y_hi = y.astype(BF16)
    q_hi = ysq.astype(BF16)
    parts = jnp.concatenate([y_hi, (y - y_hi.astype(F32)).astype(BF16),
                             q_hi, (ysq - q_hi.astype(F32)).astype(BF16)], axis=0)
    st = _dot(parts, bd)
    mean = st[:tm] + st[tm:2 * tm]
    var = st[2 * tm:3 * tm] + st[3 * tm:] - mean * mean
    yn = (y - mean) * lax.rsqrt(var + GN_EPS) * lg_ref[...] + lb_ref[...]
    rw = (yn + bonus_ref[...]) * gate_ref[...]
    cat = jnp.concatenate([att_ref[...], rw.astype(BF16)], axis=1)
    xm = x_ref[...] + gt_ref[0] * _dot(cat, wo_ref[...])
    xo_ref[...] = xm
    h_ref[...] = _rms_mod(xm, g2_ref[...], sh_ref[0], sc_ref[0]).astype(BF16)


def _readout_call(att, yf, yb, bonus, gate, x2, lnx_g, lnx_b, bd, w_out, gt1, g2, sh2, sc2,
                  rows_per_group, tm):
    r, d = x2.shape
    hd = yf.shape[1]
    per_group = rows_per_group // tm
    row = lambda i: (i, 0)
    c2 = lambda i: (0, 0)
    grp = lambda i: (i // per_group, 0, 0)
    half = pl.BlockSpec((tm, hd), row)
    full = pl.BlockSpec((tm, d), row)
    vec = pl.BlockSpec((1, 1, d), grp)
    return pl.pallas_call(
        _readout_kernel,
        grid=(r // tm,),
        in_specs=[half, half, half, half, half, full,
                  pl.BlockSpec((1, hd), c2), pl.BlockSpec((1, hd), c2), pl.BlockSpec(bd.shape, c2),
                  pl.BlockSpec(w_out.shape, c2), vec, pl.BlockSpec((1, d), c2), vec, vec],
        out_specs=[full, full],
        out_shape=[jax.ShapeDtypeStruct((r, d), F32), jax.ShapeDtypeStruct((r, d), BF16)],
        compiler_params=_cparams(("parallel",)),
    )(att, yf, yb, bonus, gate, x2, lnx_g, lnx_b, bd, w_out, gt1, g2, sh2, sc2)


def _mlp_kernel(h_ref, x_ref, w1_ref, w2_ref, gt_ref, gn_ref, shn_ref, scn_ref, *out_refs, final):
    xo_ref, acc_ref = out_refs[0], out_refs[-1]
    j = pl.program_id(1)

    @pl.when(j == 0)
    def _():
        acc_ref[...] = jnp.zeros_like(acc_ref)

    a = jnp.maximum(_dot(h_ref[...], w1_ref[...]), 0.0)
    acc_ref[...] += _dot((a * a).astype(BF16), w2_ref[...])

    @pl.when(j == pl.num_programs(1) - 1)
    def _():
        xo = x_ref[...] + gt_ref[0] * acc_ref[...]
        if final:
            ms = jnp.mean(xo * xo, axis=-1, keepdims=True)
            xo_ref[...] = xo * lax.rsqrt(ms + NORM_EPS) * gn_ref[...]
        else:
            xo_ref[...] = xo
            out_refs[1][...] = _rms_mod(xo, gn_ref[...], shn_ref[0], scn_ref[0]).astype(BF16)


def _mlp_call(h, x2, w1, w2, gt2, g_next, sh_next, sc_next, rows_per_group, tm, tf, final):
    r, d = x2.shape
    ff = w1.shape[1]
    per_group = rows_per_group // tm
    row = lambda i, j: (i, 0)
    grp = lambda i, j: (i // per_group, 0, 0)
    vec = pl.BlockSpec((1, 1, d), grp)
    n_out = 1 if final else 2
    return pl.pallas_call(
        functools.partial(_mlp_kernel, final=final),
        grid=(r // tm, ff // tf),
        in_specs=[pl.BlockSpec((tm, d), row), pl.BlockSpec((tm, d), row),
                  pl.BlockSpec((d, tf), lambda i, j: (0, j)), pl.BlockSpec((tf, d), lambda i, j: (j, 0)),
                  vec, pl.BlockSpec((1, d), lambda i, j: (0, 0)), vec, vec],
        out_specs=[pl.BlockSpec((tm, d), row), pl.BlockSpec((tm, d), row)][:n_out],
        out_shape=[jax.ShapeDtypeStruct((r, d), F32), jax.ShapeDtypeStruct((r, d), BF16)][:n_out],
        scratch_shapes=[pltpu.VMEM((tm, d), F32)],
        compiler_params=_cparams(("parallel", "arbitrary")),
    )(h, x2, w1, w2, gt2, g_next, sh_next, sc_next)


def _fft1_kernel(h_ref, kr_ref, wc_ref, z_ref):
    l1, nb, d = h_ref.shape[1:]
    rows = l1 * nb
    hf = h_ref[0].reshape(rows, d)
    p = _dot(kr_ref[...], hf)
    gw = d // FOURIER_GROUPS
    wc = wc_ref[...]
    zr, zi = [], []
    for g in range(FOURIER_GROUPS):
        ap = jnp.concatenate([p[:rows, g * gw:(g + 1) * gw], p[rows:, g * gw:(g + 1) * gw]],
                             axis=1).astype(BF16)
        zz = _dot(ap, wc)
        zr.append(zz[:, :gw])
        zi.append(zz[:, gw:])
    z = jnp.concatenate(zr + zi, axis=1).astype(BF16)
    z_ref[0] = z.reshape(l1, nb, 2 * d)


def _fft1_call(h4, kr1, wc, nb):
    b, l1, l2, d = h4.shape
    return pl.pallas_call(
        _fft1_kernel,
        grid=(b, l2 // nb),
        in_specs=[pl.BlockSpec((1, l1, nb, d), lambda bb, j: (bb, 0, j, 0)),
                  pl.BlockSpec(kr1.shape, lambda bb, j: (0, 0)),
                  pl.BlockSpec(wc.shape, lambda bb, j: (0, 0))],
        out_specs=pl.BlockSpec((1, l1, nb, 2 * d), lambda bb, j: (bb, 0, j, 0)),
        out_shape=jax.ShapeDtypeStruct((b, l1, l2, 2 * d), BF16),
        compiler_params=_cparams(("parallel", "parallel")),
    )(h4, kr1, wc)


def _fft2_kernel(z_ref, gk_ref, x_ref, wo_ref, gt_ref, g2_ref, sh_ref, sc_ref, xo_ref, h_ref):
    mb, l2, d2 = z_ref.shape[1:]
    m2b = x_ref.shape[1]
    d = d2 // 2
    z = z_ref[0].reshape(mb * l2, d2)
    rhs = jnp.concatenate([z[:, :d], z[:, d:]], axis=0)
    f = _dot(gk_ref[0], rhs)
    y = _dot(f.astype(BF16), wo_ref[...])
    xm = x_ref[0].reshape(m2b * mb, d) + gt_ref[0] * y
    xo_ref[0] = xm.reshape(m2b, mb, d)
    h_ref[0] = _rms_mod(xm, g2_ref[...], sh_ref[0], sc_ref[0]).astype(BF16).reshape(m2b, mb, d)


def _fft2_call(z4, gk, x4, w_out, gt1, g2, sh2, sc2, mb, m2b):
    b, l1, l2, d2 = z4.shape
    d = d2 // 2
    nblk = l1 // mb
    xspec = pl.BlockSpec((1, m2b, mb, d), lambda m, bb, h: (bb, h, m, 0))
    vec = pl.BlockSpec((1, 1, d), lambda m, bb, h: (bb, 0, 0))
    return pl.pallas_call(
        _fft2_kernel,
        grid=(nblk, b, l2 // m2b),
        in_specs=[pl.BlockSpec((1, mb, l2, d2), lambda m, bb, h: (bb, m, 0, 0)),
                  pl.BlockSpec((1, m2b * mb, gk.shape[2]), lambda m, bb, h: (m, h, 0)),
                  xspec,
                  pl.BlockSpec(w_out.shape, lambda m, bb, h: (0, 0)),
                  vec, pl.BlockSpec((1, d), lambda m, bb, h: (0, 0)), vec, vec],
        out_specs=[xspec, xspec],
        out_shape=[jax.ShapeDtypeStruct((b, l2, l1, d), F32), jax.ShapeDtypeStruct((b, l2, l1, d), BF16)],
        compiler_params=_cparams(("parallel", "parallel", "parallel")),
    )(z4, gk, x4, w_out, gt1, g2, sh2, sc2)


def _rope_tables(t):
    axis_dim = HEAD_DIM // 2
    rows = t // GRID_W
    row = jnp.broadcast_to(jnp.arange(rows, dtype=F32)[:, None], (rows, GRID_W)).reshape(t)
    col = jnp.broadcast_to(jnp.arange(GRID_W, dtype=F32)[None, :], (rows, GRID_W)).reshape(t)
    inv = ROPE_BASE ** (-jnp.arange(0, axis_dim, 2, dtype=F32) / axis_dim)
    ang_r, ang_c = row[:, None] * inv, col[:, None] * inv
    cos = jnp.concatenate([jnp.cos(ang_r), jnp.cos(ang_r), jnp.cos(ang_c), jnp.cos(ang_c)], axis=1)
    sin = jnp.concatenate([-jnp.sin(ang_r), jnp.sin(ang_r), -jnp.sin(ang_c), jnp.sin(ang_c)], axis=1)
    return jnp.tile(cos, (1, 2)), jnp.tile(sin, (1, 2))


def _fft_tables(t, gw, nb, mb):
    l1 = t // FFT_L2
    n1 = np.arange(l1)
    ang1 = 2.0 * np.pi * np.outer(n1, n1) / l1
    eye = np.eye(nb)
    kr1 = np.concatenate([np.kron(np.cos(ang1), eye), np.kron(np.sin(ang1), eye)], axis=0)
    ch = np.arange(gw)
    angc = 2.0 * np.pi * np.outer(ch, ch) / gw
    cg, sg = np.cos(angc), np.sin(angc)
    wc = np.block([[cg, -sg], [-sg, -cg]])
    scale = 1.0 / np.sqrt(float(t) * gw)
    m = np.arange(t)
    n2 = np.arange(FFT_L2)
    theta = 2.0 * np.pi * np.outer(m, n2) / t
    cs = np.stack([np.cos(theta), np.sin(theta)], axis=0) * scale
    cs = cs.reshape(2, FFT_L2, l1 // mb, mb, FFT_L2)
    return (jnp.asarray(kr1, F32).astype(BF16), jnp.asarray(wc, F32).astype(BF16),
            jnp.asarray(cs, F32))


def _expand_gk(cs, mb):
    eye = jnp.eye(mb, dtype=F32)
    g = jnp.einsum("rmbpn,pq->bmprqn", cs, eye)
    nblk = cs.shape[2]
    return g.reshape(nblk, FFT_L2 * mb, 2 * mb * FFT_L2).astype(BF16)


def kernel(x, c, ctx, c_ctx, ada_w, ada_b, norm1_g, norm2_g, mix_w_in, mix_w_out, attn_sink,
           shift_mu_prev, shift_mu_next, decay_w0, decay_w2, iclr_a0, iclr_a2, gate_g2, key_kk,
           key_ka, bonus_rk, lnx_g, lnx_b, fourier_w_out, mlp_w1, mlp_w2, final_g):
    b, t, d = x.shape
    nctx = ctx.shape[1]
    hd = key_kk.shape[1]
    q_dim = d - hd
    n_heads = q_dim // HEAD_DIM
    kv_dim = (n_heads // 4) * HEAD_DIM
    att_cols = q_dim + 2 * kv_dim

    cond = jnp.zeros((8, d), F32).at[:b].set(c).at[b].set(c_ctx)
    mods = _ada_call(cond, ada_w, ada_b)
    lat = [mods[i, :b].reshape(b, N_MOD, 1, d) for i in range(2)]
    cmod = [mods[i, b:b + 1].reshape(1, N_MOD, 1, d) for i in range(2)]
    lm = lambda i, k: lat[i][:, k]
    cm = lambda i, k: cmod[i][:, k]
    row1 = lambda a: a.reshape(1, -1)

    w_in = mix_w_in[0]
    wk = w_in[:, q_dim:q_dim + kv_dim].reshape(d, kv_dim // HEAD_DIM, 1, HEAD_DIM)
    wv = w_in[:, q_dim + kv_dim:att_cols].reshape(d, kv_dim // HEAD_DIM, 1, HEAD_DIM)
    dup = lambda w: jnp.broadcast_to(w, (d, kv_dim // HEAD_DIM, 2, HEAD_DIM)).reshape(d, 2 * kv_dim)
    w_att = jnp.concatenate([w_in[:, :q_dim], dup(wk), dup(wv)], axis=1).astype(BF16)
    w_rw = w_in[:, att_cols:].astype(BF16)
    cos_t, sin_t = _rope_tables(t)
    cos_c, sin_c = jnp.ones((nctx, LANES), F32), jnp.zeros((nctx, LANES), F32)
    g1 = row1(norm1_g[0])
    x2 = x.reshape(b * t, d)
    ctx2 = ctx.reshape(b * nctx, d)
    tm_in = min(512, t)
    q, kd, vd, zrw = _inproj_call(x2, g1, lm(0, 0), lm(0, 1), cos_t, sin_t, w_att, w_rw, t, tm_in)
    qc, kc, vc, zrwc = _inproj_call(ctx2, g1, cm(0, 0), cm(0, 1), cos_c, sin_c, w_att, w_rw,
                                    b * nctx, nctx)
    q, kd, vd = (a.reshape(b, t, -1) for a in (q, kd, vd))
    qc, kc, vc = (a.reshape(b, nctx, -1) for a in (qc, kc, vc))
    sinkb = jnp.broadcast_to(attn_sink[0][:, None] * LOG2E, (n_heads, LANES)).astype(F32)
    att = _attn_call(q, kd, vd, kc, vc, sinkb, 3)
    att_c = _attn_call(qc, kc, vc, kc, vc, sinkb, 0)

    mu = jnp.stack([shift_mu_prev[0], shift_mu_next[0], 1.0 - shift_mu_prev[0] - shift_mu_next[0]])
    zl = jnp.zeros((DECAY_LORA, hd), F32)
    wa = jnp.stack([jnp.concatenate([jnp.concatenate([decay_w2[0, dd], zl], axis=1),
                                     jnp.concatenate([zl, iclr_a2[0, dd]], axis=1)], axis=0)
                    for dd in range(2)])
    col_scale = jnp.tile(jnp.concatenate([jnp.full((hd,), LOG2E, F32), jnp.full((hd,), 0.5, F32)]), 2)
    wa = (jnp.concatenate([wa[0], wa[1]], axis=1) * col_scale).astype(BF16)
    w0a0 = jnp.concatenate([decay_w0[0], iclr_a0[0]], axis=1).reshape(1, 4 * hd) * col_scale
    seg = np.arange(hd) // RWKV_N
    bd = jnp.asarray(seg[:, None] == seg[None, :], F32).astype(BF16)
    k_k, r_k = row1(key_kk[0]), row1(bonus_rk[0])
    k_a = jnp.stack([1.0 - key_ka[0], key_ka[0]])
    g2w = gate_g2[0].astype(BF16)
    s_zero = jnp.zeros((b, 2, hd // LANES, LANES, LANES), F32)
    opf_c, opb_c, vb_c, pcs_c, bonus_c, gate_c = _rwkv_prep_call(zrwc.reshape(b, nctx, -1), mu, wa, w0a0,
                                                                 k_k, k_a, r_k, g2w, bd)
    opf, opb, vb, pcs, bonus, gate = _rwkv_prep_call(zrw.reshape(b, t, -1), mu, wa, w0a0,
                                                     k_k, k_a, r_k, g2w, bd)
    yfc, ybc, s_ctx = _rwkv_solve_call(opf_c, opb_c, vb_c, pcs_c, s_zero)
    yf, yb, _ = _rwkv_solve_call(opf, opb, vb, pcs, s_ctx)

    w_out = mix_w_out[0].astype(BF16)
    n2g = row1(norm2_g[0])
    flat = lambda a: a.reshape(-1, a.shape[-1])
    bd_mean = (jnp.asarray(seg[:, None] == seg[None, :], F32) * (1.0 / RWKV_N)).astype(BF16)
    xm, h2 = _readout_call(flat(att), flat(yf), flat(yb), flat(bonus), flat(gate), x2,
                           row1(lnx_g[0]), row1(lnx_b[0]), bd_mean, w_out, lm(0, 2), n2g,
                           lm(0, 3), lm(0, 4), t, tm_in)
    xmc, h2c = _readout_call(flat(att_c), flat(yfc), flat(ybc), flat(bonus_c), flat(gate_c), ctx2,
                             row1(lnx_g[0]), row1(lnx_b[0]), bd_mean, w_out, cm(0, 2), n2g,
                             cm(0, 3), cm(0, 4), b * nctx, nctx)
    w1 = mlp_w1[0].astype(BF16)
    w2 = mlp_w2[0].astype(BF16)
    g1n = row1(norm1_g[1])
    tm_mlp = min(1024, t)
    x1, h1 = _mlp_call(h2, xm, w1, w2, lm(0, 5), g1n, lm(1, 0), lm(1, 1), t, tm_mlp, MLP_TF, False)
    ctx1, _ = _mlp_call(h2c, xmc, w1, w2, cm(0, 5), g1n, cm(1, 0), cm(1, 1), b * nctx, nctx, MLP_TF,
                        False)
    del ctx1

    l1 = t // FFT_L2
    nb = 8
    mb = min(8, l1)
    gw = d // FOURIER_GROUPS
    kr1, wc, cs = _fft_tables(t, gw, nb, mb)
    gk = _expand_gk(cs, mb)
    z4 = _fft1_call(h1.reshape(b, l1, FFT_L2, d), kr1, wc, nb)
    xm4, h24 = _fft2_call(z4, gk, x1.reshape(b, FFT_L2, l1, d), fourier_w_out[0].astype(BF16),
                          lm(1, 2), row1(norm2_g[1]), lm(1, 3), lm(1, 4), mb, 64)
    out, = _mlp_call(h24.reshape(b * t, d), xm4.reshape(b * t, d), mlp_w1[1].astype(BF16),
                       mlp_w2[1].astype(BF16), lm(1, 5), row1(final_g), lm(1, 0), lm(1, 1),
                       t, tm_mlp, MLP_TF, True)
    return out.reshape(b, t, d)
```

```python
import functools

import numpy as np
import jax
import jax.numpy as jnp
from jax import lax
from jax.experimental import pallas as pl
from jax.experimental.pallas import tpu as pltpu

F32 = jnp.float32
BF16 = jnp.bfloat16
HIGHEST = lax.Precision.HIGHEST

HEAD_DIM = 64
WINDOW = 128
QBLK = 128
ATTN_QBLOCKS = 8
GRID_W = 64
ROPE_BASE = 10000.0
RWKV_N = 64
DECAY_LORA = 64
ICLR_LORA = 64
GATE_LORA = 128
FOURIER_GROUPS = 4
N_MOD = 6
NORM_EPS = 1e-6
GN_EPS = 64e-5
NEG_INF = -1e30

CHUNK = 64
RWKV_PREP_ROWS = 512
RWKV_PREP_SUB = 256
RWKV_BLOCK_CHUNKS = 8
RWKV_BATCH_ROWS = 1
ROW_TILE = 512
MLP_TM = 1024
FFT_NB = 8
FFT_MB = 8
FFT_M2B = 64
LANES = 128
FFT_L2 = 128
MLP_TF = 1024
VMEM_LIMIT = 48 * 1024 * 1024
LOG2E = 1.4426950408889634
Q_SCALE = HEAD_DIM ** -0.5 * LOG2E
W_LOG2_OFFSET = -0.5 * LOG2E + float(np.log2(LOG2E))


def _cparams(sem):
    return pltpu.CompilerParams(dimension_semantics=sem, vmem_limit_bytes=VMEM_LIMIT)


def _dot(a, b, **kw):
    return jnp.dot(a, b, preferred_element_type=F32, **kw)


def _dot_nt(a, b):
    return lax.dot_general(a, b, (((1,), (1,)), ((), ())), preferred_element_type=F32)


def _dot_tn(a, b):
    return lax.dot_general(a, b, (((0,), (0,)), ((), ())), preferred_element_type=F32)


def _split_dot(x, m_bf16, passes):
    acc = None
    rem = x
    for _ in range(passes):
        piece = rem.astype(BF16)
        term = _dot(piece, m_bf16)
        acc = term if acc is None else acc + term
        rem = rem - piece.astype(F32)
    return acc


def _rms_mod(x, g, sh, sc):
    ms = jnp.mean(x * x, axis=-1, keepdims=True)
    return (x * lax.rsqrt(ms + NORM_EPS)) * (g * (1.0 + sc)) + sh


def _ada_kernel(cond_ref, w_ref, b_ref, o_ref):
    s = cond_ref[...]
    s = s * jax.nn.sigmoid(s)
    o_ref[0] = _dot(s, w_ref[0], precision=HIGHEST) + b_ref[0]


def _ada_call(cond, ada_w, ada_b):
    depth, d, n = ada_w.shape
    tn = 1536
    return pl.pallas_call(
        _ada_kernel,
        grid=(depth, n // tn),
        in_specs=[pl.BlockSpec((8, d), lambda i, j: (0, 0)),
                  pl.BlockSpec((1, d, tn), lambda i, j: (i, 0, j)),
                  pl.BlockSpec((1, 1, tn), lambda i, j: (i, 0, j))],
        out_specs=pl.BlockSpec((1, 8, tn), lambda i, j: (i, 0, j)),
        out_shape=jax.ShapeDtypeStruct((depth, 8, n), F32),
        compiler_params=_cparams(("parallel", "parallel")),
    )(cond, ada_w, ada_b.reshape(depth, 1, n))


def _inproj_kernel(x_ref, g_ref, sh_ref, sc_ref, cos_ref, sin_ref, wa_ref, wr_ref,
                   q_ref, k_ref, v_ref, z_ref):
    h = _rms_mod(x_ref[...], g_ref[...], sh_ref[0], sc_ref[0]).astype(BF16)
    z_ref[...] = _dot(h, wr_ref[...])
    za = _dot(h, wa_ref[...])
    cos = cos_ref[...]
    sin = sin_ref[...]
    lane = lax.broadcasted_iota(jnp.int32, cos.shape, 1)
    first = (lane % 32) < 16
    nq = q_ref.shape[1] // LANES
    nk = k_ref.shape[1] // LANES
    for c in range(nq + nk):
        s = za[:, c * LANES:(c + 1) * LANES]
        partner = jnp.where(first, pltpu.roll(s, LANES - 16, 1), pltpu.roll(s, 16, 1))
        ro = s * cos + partner * sin
        if c < nq:
            q_ref[:, c * LANES:(c + 1) * LANES] = (ro * Q_SCALE).astype(BF16)
        else:
            k_ref[:, (c - nq) * LANES:(c - nq + 1) * LANES] = ro.astype(BF16)
    v_ref[...] = za[:, (nq + nk) * LANES:].astype(BF16)


def _inproj_call(x2, g1, sh, sc, cos, sin, w_att, w_rw, rows_per_group, tm):
    r, d = x2.shape
    period = cos.shape[0]
    n_per = period // tm
    per_group = rows_per_group // tm
    na = w_att.shape[1]
    nr = w_rw.shape[1]
    nq, nkd = 512, 256
    row = lambda i: (i, 0)
    grp = lambda i: (i // per_group, 0, 0)
    return pl.pallas_call(
        _inproj_kernel,
        grid=(r // tm,),
        in_specs=[pl.BlockSpec((tm, d), row),
                  pl.BlockSpec((1, d), lambda i: (0, 0)),
                  pl.BlockSpec((1, 1, d), grp),
                  pl.BlockSpec((1, 1, d), grp),
                  pl.BlockSpec((tm, LANES), lambda i: (i % n_per, 0)),
                  pl.BlockSpec((tm, LANES), lambda i: (i % n_per, 0)),
                  pl.BlockSpec((d, na), lambda i: (0, 0)),
                  pl.BlockSpec((d, nr), lambda i: (0, 0))],
        out_specs=[pl.BlockSpec((tm, nq), row), pl.BlockSpec((tm, nkd), row),
                   pl.BlockSpec((tm, nkd), row), pl.BlockSpec((tm, nr), row)],
        out_shape=[jax.ShapeDtypeStruct((r, nq), BF16), jax.ShapeDtypeStruct((r, nkd), BF16),
                   jax.ShapeDtypeStruct((r, nkd), BF16), jax.ShapeDtypeStruct((r, nr), F32)],
        compiler_params=_cparams(("parallel",)),
    )(x2, g1, sh, sc, cos, sin, w_att, w_rw)


def _attn_kernel(*refs, local, qpb):
    q_ref = refs[0]
    nk = qpb + 2 if local else 0
    k_refs = refs[1:1 + nk]
    v_refs = refs[1 + nk:1 + 2 * nk]
    kc_ref, vc_ref, sink_ref, o_ref = refs[1 + 2 * nk:]
    i = pl.program_id(1)
    nb = pl.num_programs(1) * qpb
    nctx = kc_ref.shape[1]
    nkeys = (3 * QBLK if local else 0) + nctx
    gq = 4
    rows = gq * QBLK
    low = lax.broadcasted_iota(jnp.int32, (QBLK, LANES), 1) < HEAD_DIM
    if local:
        rq = lax.broadcasted_iota(jnp.int32, (rows, QBLK), 0) % QBLK
        ck = lax.broadcasted_iota(jnp.int32, (rows, QBLK), 1)
    zero = jnp.zeros((QBLK, LANES), BF16)
    ones = jnp.ones((nkeys, LANES), BF16)
    n_groups = q_ref.shape[2] // (gq * HEAD_DIM)
    units = [(qi, g) for qi in range(qpb) for g in range(n_groups)]
    scores, vals, sinks = {}, {}, {}
    for qi, g in units:
        ksl = slice(g * LANES, (g + 1) * LANES)
        keys = jnp.concatenate([kr[0, :, ksl] for kr in k_refs[qi:qi + 3]] + [kc_ref[0, :, ksl]], axis=0)
        vals[qi, g] = jnp.concatenate(
            [jnp.concatenate([vr[0, :, ksl] for vr in v_refs[qi:qi + 3]] + [vc_ref[0, :, ksl]], axis=0), ones],
            axis=1)
        qs = []
        for pp in range(2):
            qp = q_ref[0, qi * QBLK:(qi + 1) * QBLK, (2 * g + pp) * LANES:(2 * g + pp + 1) * LANES]
            qs += [jnp.where(low, qp, zero), jnp.where(low, zero, qp)]
        scores[qi, g] = _dot_nt(jnp.concatenate(qs, axis=0), keys)
        sinks[g] = jnp.concatenate(
            [jnp.broadcast_to(sink_ref[gq * g + h:gq * g + h + 1, 0:1], (QBLK, 1)) for h in range(gq)], axis=0)
    probs, ms = {}, {}
    for qi, g in units:
        s = scores[qi, g]
        if local:
            blk = i * qpb + qi
            mask_prev = ck >= rq + jnp.where(blk >= 1, 0, QBLK)
            mask_next = ck <= rq - jnp.where(blk <= nb - 2, 0, QBLK)
            s = jnp.concatenate([jnp.where(mask_prev, s[:, :QBLK], NEG_INF), s[:, QBLK:2 * QBLK],
                                 jnp.where(mask_next, s[:, 2 * QBLK:3 * QBLK], NEG_INF), s[:, 3 * QBLK:]],
                                axis=1)
        m = jnp.maximum(jnp.max(s, axis=-1, keepdims=True), sinks[g])
        ms[qi, g] = m
        probs[qi, g] = jnp.exp2(s - m).astype(BF16)
    for qi, g in units:
        o = _dot(probs[qi, g], vals[qi, g])
        out = o[:, :LANES] / (o[:, LANES:] + jnp.exp2(sinks[g] - ms[qi, g]))
        for pp in range(2):
            even = out[2 * pp * QBLK:(2 * pp + 1) * QBLK]
            odd = out[(2 * pp + 1) * QBLK:(2 * pp + 2) * QBLK]
            p = 2 * g + pp
            o_ref[0, qi * QBLK:(qi + 1) * QBLK, p * LANES:(p + 1) * LANES] = (
                jnp.where(low, even, odd).astype(BF16))


def _attn_call(q, kd, vd, kc, vc, sinkb, n_loc):
    b, t, nq = q.shape
    nb = t // QBLK
    qpb = min(ATTN_QBLOCKS, nb)
    nctx = kc.shape[1]
    kw = kd.shape[2]
    qspec = pl.BlockSpec((1, qpb * QBLK, nq), lambda bb, i: (bb, i, 0))
    loc = []
    for off in range(-1, qpb + 1) if n_loc else ():
        loc.append(pl.BlockSpec((1, QBLK, kw), functools.partial(
            lambda bb, i, off: (bb, jnp.clip(i * qpb + off, 0, nb - 1), 0), off=off)))
    cspec = pl.BlockSpec((1, nctx, kw), lambda bb, i: (bb, 0, 0))
    args = [q] + [kd] * len(loc) + [vd] * len(loc) + [kc, vc, sinkb]
    return pl.pallas_call(
        functools.partial(_attn_kernel, local=bool(n_loc), qpb=qpb),
        grid=(b, nb // qpb),
        in_specs=[qspec] + loc + loc + [cspec, cspec, pl.BlockSpec(sinkb.shape, lambda bb, i: (0, 0))],
        out_specs=qspec,
        out_shape=jax.ShapeDtypeStruct((b, t, nq), BF16),
        compiler_params=_cparams(("parallel", "parallel")),
    )(*args)


def _rwkv_prep_kernel(z_ref, zp_ref, zn_ref, mu_ref, wa_ref, w0a0_ref, kk_ref, ka_ref, rk_ref,
                      g2_ref, bd_ref, tri_ref, opf_ref, opb_ref, v_ref, pc_ref, bonus_ref, gate_ref):
    j = pl.program_id(1)
    nblk = pl.num_programs(1)
    z = z_ref[0]
    nrows = z.shape[0]
    c = CHUNK
    hd = kk_ref.shape[1]
    mu = mu_ref[...]
    row8 = lax.broadcasted_iota(jnp.int32, (8, z.shape[1]), 0)
    zp = pltpu.roll(z, 1, 0)
    zp = jnp.concatenate([jnp.where(row8 == 0, jnp.where(j == 0, 0.0, zp_ref[0, 7:8]), zp[:8]), zp[8:]],
                         axis=0)
    zn = pltpu.roll(z, nrows - 1, 0)
    zn = jnp.concatenate([zn[:nrows - 8],
                          jnp.where(row8 == 7, jnp.where(j == nblk - 1, 0.0, zn_ref[0, 0:1]), zn[nrows - 8:])],
                         axis=0)
    zs = mu[2:3] * z + mu[0:1] * zp + mu[1:2] * zn
    bd = bd_ref[...]
    sub = tri_ref.shape[1]
    nsub = sub // c
    parts = []
    for h in range(nrows // sub):
        rows = slice(h * sub, (h + 1) * sub)
        r = zs[rows, 0:hd]
        k = zs[rows, hd:2 * hd]
        v = zs[rows, 2 * hd:3 * hd]
        wa_in = zs[rows, 3 * hd:3 * hd + LANES]
        gl = zs[rows, 3 * hd + LANES:]
        v_ref[0, rows] = v.astype(BF16)
        bonus_ref[0, rows] = _split_dot(r * k * rk_ref[...], bd, 1) * v
        gate_ref[0, rows] = _dot(jax.nn.sigmoid(gl).astype(BF16), g2_ref[...])
        low_r = lax.broadcasted_iota(jnp.int32, (sub, LANES), 1) < DECAY_LORA
        tw = jnp.where(low_r, jnp.tanh(wa_in), wa_in)
        xwa2 = _split_dot(tw, wa_ref[...], 2) + w0a0_ref[...]
        kkr = k * kk_ref[...]
        kk = kkr * lax.rsqrt(_split_dot(kkr * kkr, bd, 1) + 1e-12)
        parts.append(dict(rows=rows, h=h, r=r, k=k, kk=kk, xwa2=xwa2))
    for pt in parts:
        pt["dirs"] = []
        for d in range(2):
            xs = pt["xwa2"][:, 2 * d * hd:(2 * d + 1) * hd]
            a = 0.5 + 0.5 * jnp.tanh(pt["xwa2"][:, (2 * d + 1) * hd:(2 * d + 2) * hd])
            w_l2 = jnp.minimum(xs, 0.0) - jnp.log(1.0 + jnp.exp2(-jnp.abs(xs))) * LOG2E + W_LOG2_OFFSET
            lw = -jnp.exp2(w_l2)
            kd = pt["k"] * (ka_ref[0:1] + a * ka_ref[1:2])
            bb = pt["kk"] * a
            cum = _split_dot_left(tri_ref[d], lw, 2)
            pt["dirs"].append((lw, kd, bb, cum))
    for pt in parts:
        for d, op_ref in enumerate((opf_ref, opb_ref)):
            lw, kd, bb, cum = pt["dirs"][d]
            e_in = jnp.exp2(cum)
            e_ex = jnp.exp2(cum - lw)
            e_neg = jnp.exp2(-cum)
            op_ref[0, pt["rows"]] = jnp.concatenate(
                [pt["kk"] * e_ex, pt["r"] * e_in, bb * e_neg, kd * e_neg], axis=1).astype(BF16)
            for s in range(nsub):
                total = cum[s * c:s * c + 1] if d else cum[s * c + c - 1:s * c + c]
                row0 = 8 * (pt["h"] * nsub + s)
                pc_ref[0, row0:row0 + 8, d * hd:(d + 1) * hd] = jnp.broadcast_to(jnp.exp2(total), (8, hd))


def _rwkv_prep_call(z, mu, wa2, w0a02, k_k, k_a, r_k, g2, bd):
    b, t, nz = z.shape
    rows = min(RWKV_PREP_ROWS, t)
    sub_chunks = min(RWKV_PREP_SUB, rows) // CHUNK
    nsub = rows // CHUNK
    ti = np.arange(CHUNK)
    tri = np.stack([np.kron(np.eye(sub_chunks), ti[None, :] <= ti[:, None]),
                    np.kron(np.eye(sub_chunks), ti[None, :] >= ti[:, None])]).astype(np.float32)
    tri = jnp.asarray(tri, F32).astype(BF16)
    hd = k_k.shape[1]
    nblk = t // rows
    cb = rows // 8
    nb8 = t // 8
    blk = lambda bb, j: (bb, j, 0)
    const2 = lambda bb, j: (0, 0)
    const3 = lambda bb, j: (0, 0, 0)
    full = lambda a: pl.BlockSpec(a.shape, const2 if a.ndim == 2 else const3)
    return pl.pallas_call(
        _rwkv_prep_kernel,
        grid=(b, nblk),
        in_specs=[pl.BlockSpec((1, rows, nz), blk),
                  pl.BlockSpec((1, 8, nz), lambda bb, j: (bb, jnp.maximum(j * cb - 1, 0), 0)),
                  pl.BlockSpec((1, 8, nz), lambda bb, j: (bb, jnp.minimum((j + 1) * cb, nb8 - 1), 0)),
                  full(mu), full(wa2), full(w0a02), full(k_k), full(k_a), full(r_k), full(g2), full(bd),
                  full(tri)],
        out_specs=[pl.BlockSpec((1, rows, 4 * hd), blk), pl.BlockSpec((1, rows, 4 * hd), blk),
                   pl.BlockSpec((1, rows, hd), blk), pl.BlockSpec((1, 8 * nsub, 2 * hd), blk),
                   pl.BlockSpec((1, rows, hd), blk), pl.BlockSpec((1, rows, hd), blk)],
        out_shape=[jax.ShapeDtypeStruct((b, t, 4 * hd), BF16), jax.ShapeDtypeStruct((b, t, 4 * hd), BF16),
                   jax.ShapeDtypeStruct((b, t, hd), BF16),
                   jax.ShapeDtypeStruct((b, 8 * (t // CHUNK), 2 * hd), F32),
                   jax.ShapeDtypeStruct((b, t, hd), F32), jax.ShapeDtypeStruct((b, t, hd), F32)],
        compiler_params=_cparams(("parallel", "parallel")),
    )(z, z, z, mu, wa2, w0a02, k_k, k_a, r_k, g2, bd, tri)


def _rwkv_chains(op_ref, v_ref, pc_ref, keep, bi, d):
    c = CHUNK
    hd = v_ref.shape[2]
    nsub = op_ref.shape[1] // c
    low = lax.broadcasted_iota(jnp.int32, (c, LANES), 1) < RWKV_N

    def stack(xp):
        zero = jnp.zeros_like(xp)
        return jnp.concatenate([jnp.where(low, xp, zero), jnp.where(low, zero, xp)], axis=0)

    chains = []
    for s in range(nsub):
        rows = slice(s * c, (s + 1) * c)
        for p in range(hd // LANES):
            sb = lambda i: stack(op_ref[bi, rows, i * hd + p * LANES:i * hd + (p + 1) * LANES])
            chains.append(dict(
                d=2 * bi + d, p=p, s=s, keep=keep,
                pc=pc_ref[bi, 8 * s:8 * s + 1, d * hd + p * LANES:d * hd + (p + 1) * LANES],
                kts=sb(0), rtb=sb(1), bts=sb(2), kdts=sb(3),
                vs=stack(v_ref[bi, rows, p * LANES:(p + 1) * LANES])))
    return chains


def _rwkv_solve(chains, ioff, state_ref, c):
    c2 = 2 * c
    bf = lambda a: a.astype(BF16)
    rr = lax.broadcasted_iota(jnp.int32, (c2, 2 * c2), 0)
    cc = lax.broadcasted_iota(jnp.int32, (c2, 2 * c2), 1)
    diag = (rr // c) == ((cc // c) % 2)

    def blockdiag(ab):
        n = ab.shape[1] // 2
        z = jnp.zeros((ab.shape[0], n), ab.dtype)
        return jnp.concatenate([jnp.concatenate([ab[:, :n], z], axis=1),
                                jnp.concatenate([z, ab[:, n:]], axis=1)], axis=0)

    def swap_halves(g):
        return jnp.concatenate([g[c:], g[:c]], axis=0)

    by_key = {(ch["d"], ch["s"], ch["p"]): ch for ch in chains}
    supers = [(by_key[(d, s, p)], by_key[(d, s, p + 1)])
              for (d, s, p) in sorted(by_key) if p % 2 == 0]
    sup = [dict(c0=a, c1=b, d=a["d"], s=a["s"], q=a["p"] // 2) for a, b in supers]
    for ch in chains:
        ch["kr"] = jnp.concatenate([ch["kts"], ch["rtb"]], axis=0)
        ch["bk"] = jnp.concatenate([ch["bts"], ch["kdts"]], axis=0)
        aa = bf(_dot_nt(ch["kr"], ch["bk"])) * ch["keep"]
        ch["auk"] = aa[:c2, c2:]
        ch["arr"] = aa[c2:, :]
        ch["b0"] = ioff - aa[:c2, :c2]
    zero_b = jnp.zeros((c2, 2 * c2), BF16)
    for sc in sup:
        sc["b"] = jnp.concatenate([sc["c0"]["b0"], sc["c1"]["b0"]], axis=1)
        sc["kr"] = jnp.concatenate([sc["c0"]["kr"], sc["c1"]["kr"]], axis=1)
    for _ in range(6):
        for sc in sup:
            b = sc["b"]
            sc["b"] = bf(_dot(jnp.where(diag, b, zero_b), blockdiag(b))) + jnp.where(diag, zero_b, b)
    for sc in sup:
        auk = jnp.concatenate([sc["c0"]["auk"], sc["c1"]["auk"]], axis=1)
        vs2 = jnp.concatenate([sc["c0"]["vs"], sc["c1"]["vs"]], axis=1)
        sc["av"] = _dot(bf(auk), blockdiag(vs2))
    for sc in sup:
        sc["tsw"] = jnp.where(diag, zero_b, sc["b"])
    groups = {}
    for sc in sup:
        groups.setdefault((sc["d"], sc["q"]), []).append(sc)
    state = {(d, p): state_ref[d // 2, d % 2, p] for (d, q) in groups for p in (2 * q, 2 * q + 1)}
    nsub = len(next(iter(groups.values())))
    ys = {}
    for step in range(nsub):
        cur = {key: sorted(g, key=lambda sc: sc["s"], reverse=bool(key[0] % 2))[step]
               for key, g in groups.items()}
        for (d, q), sc in cur.items():
            s2 = jnp.concatenate([bf(state[(d, 2 * q)]), bf(state[(d, 2 * q + 1)])], axis=1)
            sc["ksrs"] = _dot_nt(sc["kr"], blockdiag(s2))
        for (d, q), sc in cur.items():
            g = sc["av"] + sc["ksrs"][:c2]
            sc["ub"] = bf(-_dot(sc["tsw"], blockdiag(bf(swap_halves(g)))))
        for (d, q), sc in cur.items():
            for i, ch in enumerate((sc["c0"], sc["c1"])):
                key = (d, 2 * q + i)
                uv = jnp.concatenate([sc["ub"][:, i * c2:(i + 1) * c2], ch["vs"]], axis=0)
                y = sc["ksrs"][c2:, i * c2:(i + 1) * c2] + _dot(ch["arr"], uv)
                state[key] = (state[key] + _dot_tn(uv, ch["bk"])) * ch["pc"]
                ys[(d, sc["s"], key[1])] = y[:c] + y[c:]
    for key, s_new in state.items():
        state_ref[key[0] // 2, key[0] % 2, key[1]] = s_new
    dirs = sorted({d for d, _ in state})
    n_pairs = len(state) // len(dirs)
    return [jnp.concatenate([jnp.concatenate([ys[(d, s, p)] for p in range(n_pairs)], axis=1)
                             for s in range(nsub)], axis=0) for d in dirs]


def _split_dot_left(m_bf16, x, passes):
    acc = None
    rem = x
    for _ in range(passes):
        piece = rem.astype(BF16)
        term = _dot(m_bf16, piece)
        acc = term if acc is None else acc + term
        rem = rem - piece.astype(F32)
    return acc


def _rwkv_solve_kernel(opf_ref, opb_ref, vf_ref, vb_ref, pcf_ref, pcb_ref, s0_ref, keep_ref, ioff_ref,
                       yf_ref, yb_ref, state_ref):
    @pl.when(pl.program_id(1) == 0)
    def _():
        state_ref[...] = s0_ref[...]

    chains = []
    for bi in range(opf_ref.shape[0]):
        chains += _rwkv_chains(opf_ref, vf_ref, pcf_ref, keep_ref[0], bi, 0)
        chains += _rwkv_chains(opb_ref, vb_ref, pcb_ref, keep_ref[1], bi, 1)
    ys = _rwkv_solve(chains, ioff_ref[...], state_ref, CHUNK)
    for bi in range(opf_ref.shape[0]):
        yf_ref[bi] = ys[2 * bi]
        yb_ref[bi] = ys[2 * bi + 1]


def _rwkv_masks(c):
    tt = (np.arange(4 * c) % c)[:, None]
    ss = (np.arange(4 * c) % c)[None, :]
    incl = (np.arange(4 * c) >= 2 * c)[:, None]
    keep = np.stack([np.where(incl, ss <= tt, ss < tt), np.where(incl, ss >= tt, ss > tt)])
    ioff = np.kron(np.array([[0.0, 1.0], [1.0, 0.0]]), np.eye(c))
    return jnp.asarray(keep.astype(np.float32)).astype(BF16), jnp.asarray(ioff, F32).astype(BF16)


def _rwkv_solve_call(opf, opb, vb, pcs, s0):
    b, t, hd = vb.shape
    rows = min(RWKV_BLOCK_CHUNKS * CHUNK, t)
    keep, ioff = _rwkv_masks(CHUNK)
    nblk = t // rows
    nbt = RWKV_BATCH_ROWS if b % RWKV_BATCH_ROWS == 0 else 1
    fwd = lambda bb, j: (bb, j, 0)
    bwd = lambda bb, j: (bb, nblk - 1 - j, 0)
    st = pl.BlockSpec((nbt,) + s0.shape[1:], lambda bb, j: (bb, 0, 0, 0, 0))
    spec = lambda width, idx: pl.BlockSpec((nbt, rows, width), idx)
    pcspec = lambda idx: pl.BlockSpec((nbt, 8 * (rows // CHUNK), 2 * hd), idx)
    return pl.pallas_call(
        _rwkv_solve_kernel,
        grid=(b // nbt, nblk),
        in_specs=[spec(4 * hd, fwd), spec(4 * hd, bwd), spec(hd, fwd), spec(hd, bwd),
                  pcspec(fwd), pcspec(bwd), st,
                  pl.BlockSpec(keep.shape, lambda bb, j: (0, 0, 0)),
                  pl.BlockSpec(ioff.shape, lambda bb, j: (0, 0))],
        out_specs=[spec(hd, fwd), spec(hd, bwd), st],
        out_shape=[jax.ShapeDtypeStruct((b, t, hd), F32)] * 2 + [jax.ShapeDtypeStruct(s0.shape, F32)],
        compiler_params=_cparams(("parallel", "arbitrary")),
    )(opf, opb, vb, vb, pcs, pcs, s0, keep, ioff)


def _readout_kernel(att_ref, yf_ref, yb_ref, bonus_ref, gate_ref, x_ref, lg_ref, lb_ref, bd_ref,
                    wo_ref, gt_ref, g2_ref, sh_ref, sc_ref, xo_ref, h_ref):
    bd = bd_ref[...]
    y = yf_ref[...] + yb_ref[...]
    tm = y.shape[0]
    ysq = y * y
    y_hi = y.astype(BF16)
    q_hi = ysq.astype(BF16)
    parts = jnp.concatenate([y_hi, (y - y_hi.astype(F32)).astype(BF16),
                             q_hi, (ysq - q_hi.astype(F32)).astype(BF16)], axis=0)
    st = _dot(parts, bd)
    mean = st[:tm] + st[tm:2 * tm]
    var = st[2 * tm:3 * tm] + st[3 * tm:] - mean * mean
    yn = (y - mean) * lax.rsqrt(var + GN_EPS) * lg_ref[...] + lb_ref[...]
    rw = (yn + bonus_ref[...]) * gate_ref[...]
    cat = jnp.concatenate([att_ref[...], rw.astype(BF16)], axis=1)
    xm = x_ref[...] + gt_ref[0] * _dot(cat, wo_ref[...])
    xo_ref[...] = xm
    h_ref[...] = _rms_mod(xm, g2_ref[...], sh_ref[0], sc_ref[0]).astype(BF16)


def _readout_call(att, yf, yb, bonus, gate, x2, lnx_g, lnx_b, bd, w_out, gt1, g2, sh2, sc2,
                  rows_per_group, tm):
    r, d = x2.shape
    hd = yf.shape[1]
    per_group = rows_per_group // tm
    row = lambda i: (i, 0)
    c2 = lambda i: (0, 0)
    grp = lambda i: (i // per_group, 0, 0)
    half = pl.BlockSpec((tm, hd), row)
    full = pl.BlockSpec((tm, d), row)
    vec = pl.BlockSpec((1, 1, d), grp)
    return pl.pallas_call(
        _readout_kernel,
        grid=(r // tm,),
        in_specs=[half, half, half, half, half, full,
                  pl.BlockSpec((1, hd), c2), pl.BlockSpec((1, hd), c2), pl.BlockSpec(bd.shape, c2),
                  pl.BlockSpec(w_out.shape, c2), vec, pl.BlockSpec((1, d), c2), vec, vec],
        out_specs=[full, full],
        out_shape=[jax.ShapeDtypeStruct((r, d), F32), jax.ShapeDtypeStruct((r, d), BF16)],
        compiler_params=_cparams(("parallel",)),
    )(att, yf, yb, bonus, gate, x2, lnx_g, lnx_b, bd, w_out, gt1, g2, sh2, sc2)


def _mlp_kernel(h_ref, x_ref, w1_ref, w2_ref, gt_ref, gn_ref, shn_ref, scn_ref, *out_refs, final):
    xo_ref, acc_ref = out_refs[0], out_refs[-1]
    j = pl.program_id(1)

    @pl.when(j == 0)
    def _():
        acc_ref[...] = jnp.zeros_like(acc_ref)

    a = jnp.maximum(_dot(h_ref[...], w1_ref[...]), 0.0)
    acc_ref[...] += _dot((a * a).astype(BF16), w2_ref[...])

    @pl.when(j == pl.num_programs(1) - 1)
    def _():
        xo = x_ref[...] + gt_ref[0] * acc_ref[...]
        if final:
            ms = jnp.mean(xo * xo, axis=-1, keepdims=True)
            xo_ref[...] = xo * lax.rsqrt(ms + NORM_EPS) * gn_ref[...]
        else:
            xo_ref[...] = xo
            out_refs[1][...] = _rms_mod(xo, gn_ref[...], shn_ref[0], scn_ref[0]).astype(BF16)


def _mlp_call(h, x2, w1, w2, gt2, g_next, sh_next, sc_next, rows_per_group, tm, tf, final):
    r, d = x2.shape
    ff = w1.shape[1]
    per_group = rows_per_group // tm
    row = lambda i, j: (i, 0)
    grp = lambda i, j: (i // per_group, 0, 0)
    vec = pl.BlockSpec((1, 1, d), grp)
    n_out = 1 if final else 2
    return pl.pallas_call(
        functools.partial(_mlp_kernel, final=final),
        grid=(r // tm, ff // tf),
        in_specs=[pl.BlockSpec((tm, d), row), pl.BlockSpec((tm, d), row),
                  pl.BlockSpec((d, tf), lambda i, j: (0, j)), pl.BlockSpec((tf, d), lambda i, j: (j, 0)),
                  vec, pl.BlockSpec((1, d), lambda i, j: (0, 0)), vec, vec],
        out_specs=[pl.BlockSpec((tm, d), row), pl.BlockSpec((tm, d), row)][:n_out],
        out_shape=[jax.ShapeDtypeStruct((r, d), F32), jax.ShapeDtypeStruct((r, d), BF16)][:n_out],
        scratch_shapes=[pltpu.VMEM((tm, d), F32)],
        compiler_params=_cparams(("parallel", "arbitrary")),
    )(h, x2, w1, w2, gt2, g_next, sh_next, sc_next)


def _fft1_kernel(h_ref, kr_ref, wc_ref, z_ref):
    l1, nb, d = h_ref.shape[1:]
    rows = l1 * nb
    hf = h_ref[0].reshape(rows, d)
    p = _dot(kr_ref[...], hf)
    gw = d // FOURIER_GROUPS
    wc = wc_ref[...]
    zr, zi = [], []
    for g in range(FOURIER_GROUPS):
        ap = jnp.concatenate([p[:rows, g * gw:(g + 1) * gw], p[rows:, g * gw:(g + 1) * gw]],
                             axis=1).astype(BF16)
        zz = _dot(ap, wc)
        zr.append(zz[:, :gw])
        zi.append(zz[:, gw:])
    z = jnp.concatenate(zr + zi, axis=1).astype(BF16)
    z_ref[0] = z.reshape(l1, nb, 2 * d)


def _fft1_call(h4, kr1, wc, nb):
    b, l1, l2, d = h4.shape
    return pl.pallas_call(
        _fft1_kernel,
        grid=(b, l2 // nb),
        in_specs=[pl.BlockSpec((1, l1, nb, d), lambda bb, j: (bb, 0, j, 0)),
                  pl.BlockSpec(kr1.shape, lambda bb, j: (0, 0)),
                  pl.BlockSpec(wc.shape, lambda bb, j: (0, 0))],
        out_specs=pl.BlockSpec((1, l1, nb, 2 * d), lambda bb, j: (bb, 0, j, 0)),
        out_shape=jax.ShapeDtypeStruct((b, l1, l2, 2 * d), BF16),
        compiler_params=_cparams(("parallel", "parallel")),
    )(h4, kr1, wc)


def _fft2_kernel(z_ref, gk_ref, x_ref, wo_ref, gt_ref, g2_ref, sh_ref, sc_ref, xo_ref, h_ref):
    mb, l2, d2 = z_ref.shape[1:]
    m2b = x_ref.shape[1]
    d = d2 // 2
    z = z_ref[0].reshape(mb * l2, d2)
    rhs = jnp.concatenate([z[:, :d], z[:, d:]], axis=0)
    f = _dot(gk_ref[0], rhs)
    y = _dot(f.astype(BF16), wo_ref[...])
    xm = x_ref[0].reshape(m2b * mb, d) + gt_ref[0] * y
    xo_ref[0] = xm.reshape(m2b, mb, d)
    h_ref[0] = _rms_mod(xm, g2_ref[...], sh_ref[0], sc_ref[0]).astype(BF16).reshape(m2b, mb, d)


def _fft2_call(z4, gk, x4, w_out, gt1, g2, sh2, sc2, mb, m2b):
    b, l1, l2, d2 = z4.shape
    d = d2 // 2
    nblk = l1 // mb
    xspec = pl.BlockSpec((1, m2b, mb, d), lambda m, bb, h: (bb, h, m, 0))
    vec = pl.BlockSpec((1, 1, d), lambda m, bb, h: (bb, 0, 0))
    return pl.pallas_call(
        _fft2_kernel,
        grid=(nblk, b, l2 // m2b),
        in_specs=[pl.BlockSpec((1, mb, l2, d2), lambda m, bb, h: (bb, m, 0, 0)),
                  pl.BlockSpec((1, m2b * mb, gk.shape[2]), lambda m, bb, h: (m, h, 0)),
                  xspec,
                  pl.BlockSpec(w_out.shape, lambda m, bb, h: (0, 0)),
                  vec, pl.BlockSpec((1, d), lambda m, bb, h: (0, 0)), vec, vec],
        out_specs=[xspec, xspec],
        out_shape=[jax.ShapeDtypeStruct((b, l2, l1, d), F32), jax.ShapeDtypeStruct((b, l2, l1, d), BF16)],
        compiler_params=_cparams(("parallel", "parallel", "parallel")),
    )(z4, gk, x4, w_out, gt1, g2, sh2, sc2)


def _rope_tables(t):
    axis_dim = HEAD_DIM // 2
    rows = t // GRID_W
    row = jnp.broadcast_to(jnp.arange(rows, dtype=F32)[:, None], (rows, GRID_W)).reshape(t)
    col = jnp.broadcast_to(jnp.arange(GRID_W, dtype=F32)[None, :], (rows, GRID_W)).reshape(t)
    inv = ROPE_BASE ** (-jnp.arange(0, axis_dim, 2, dtype=F32) / axis_dim)
    ang_r, ang_c = row[:, None] * inv, col[:, None] * inv
    cos = jnp.concatenate([jnp.cos(ang_r), jnp.cos(ang_r), jnp.cos(ang_c), jnp.cos(ang_c)], axis=1)
    sin = jnp.concatenate([-jnp.sin(ang_r), jnp.sin(ang_r), -jnp.sin(ang_c), jnp.sin(ang_c)], axis=1)
    return jnp.tile(cos, (1, 2)), jnp.tile(sin, (1, 2))


def _fft_tables(t, gw, nb, mb):
    l1 = t // FFT_L2
    n1 = np.arange(l1)
    ang1 = 2.0 * np.pi * np.outer(n1, n1) / l1
    eye = np.eye(nb)
    kr1 = np.concatenate([np.kron(np.cos(ang1), eye), np.kron(np.sin(ang1), eye)], axis=0)
    ch = np.arange(gw)
    angc = 2.0 * np.pi * np.outer(ch, ch) / gw
    cg, sg = np.cos(angc), np.sin(angc)
    wc = np.block([[cg, -sg], [-sg, -cg]])
    scale = 1.0 / np.sqrt(float(t) * gw)
    m = np.arange(t)
    n2 = np.arange(FFT_L2)
    theta = 2.0 * np.pi * np.outer(m, n2) / t
    cs = np.stack([np.cos(theta), np.sin(theta)], axis=0) * scale
    cs = cs.reshape(2, FFT_L2, l1 // mb, mb, FFT_L2)
    return (jnp.asarray(kr1, F32).astype(BF16), jnp.asarray(wc, F32).astype(BF16),
            jnp.asarray(cs, F32))


def _expand_gk(cs, mb):
    eye = jnp.eye(mb, dtype=F32)
    g = jnp.einsum("rmbpn,pq->bmprqn", cs, eye)
    nblk = cs.shape[2]
    return g.reshape(nblk, FFT_L2 * mb, 2 * mb * FFT_L2).astype(BF16)


def kernel(x, c, ctx, c_ctx, ada_w, ada_b, norm1_g, norm2_g, mix_w_in, mix_w_out, attn_sink,
           shift_mu_prev, shift_mu_next, decay_w0, decay_w2, iclr_a0, iclr_a2, gate_g2, key_kk,
           key_ka, bonus_rk, lnx_g, lnx_b, fourier_w_out, mlp_w1, mlp_w2, final_g):
    b, t, d = x.shape
    nctx = ctx.shape[1]
    hd = key_kk.shape[1]
    q_dim = d - hd
    n_heads = q_dim // HEAD_DIM
    kv_dim = (n_heads // 4) * HEAD_DIM
    att_cols = q_dim + 2 * kv_dim

    cond = jnp.zeros((8, d), F32).at[:b].set(c).at[b].set(c_ctx)
    mods = _ada_call(cond, ada_w, ada_b)
    lat = [mods[i, :b].reshape(b, N_MOD, 1, d) for i in range(2)]
    cmod = [mods[i, b:b + 1].reshape(1, N_MOD, 1, d) for i in range(2)]
    lm = lambda i, k: lat[i][:, k]
    cm = lambda i, k: cmod[i][:, k]
    row1 = lambda a: a.reshape(1, -1)

    w_in = mix_w_in[0]
    wk = w_in[:, q_dim:q_dim + kv_dim].reshape(d, kv_dim // HEAD_DIM, 1, HEAD_DIM)
    wv = w_in[:, q_dim + kv_dim:att_cols].reshape(d, kv_dim // HEAD_DIM, 1, HEAD_DIM)
    dup = lambda w: jnp.broadcast_to(w, (d, kv_dim // HEAD_DIM, 2, HEAD_DIM)).reshape(d, 2 * kv_dim)
    w_att = jnp.concatenate([w_in[:, :q_dim], dup(wk), dup(wv)], axis=1).astype(BF16)
    w_rw = w_in[:, att_cols:].astype(BF16)
    cos_t, sin_t = _rope_tables(t)
    cos_c, sin_c = jnp.ones((nctx, LANES), F32), jnp.zeros((nctx, LANES), F32)
    g1 = row1(norm1_g[0])
    x2 = x.reshape(b * t, d)
    ctx2 = ctx.reshape(b * nctx, d)
    tm_in = min(ROW_TILE, t)
    q, kd, vd, zrw = _inproj_call(x2, g1, lm(0, 0), lm(0, 1), cos_t, sin_t, w_att, w_rw, t, tm_in)
    qc, kc, vc, zrwc = _inproj_call(ctx2, g1, cm(0, 0), cm(0, 1), cos_c, sin_c, w_att, w_rw,
                                    b * nctx, nctx)
    q, kd, vd = (a.reshape(b, t, -1) for a in (q, kd, vd))
    qc, kc, vc = (a.reshape(b, nctx, -1) for a in (qc, kc, vc))
    sinkb = jnp.broadcast_to(attn_sink[0][:, None] * LOG2E, (n_heads, LANES)).astype(F32)
    att = _attn_call(q, kd, vd, kc, vc, sinkb, 3)
    att_c = _attn_call(qc, kc, vc, kc, vc, sinkb, 0)

    mu = jnp.stack([shift_mu_prev[0], shift_mu_next[0], 1.0 - shift_mu_prev[0] - shift_mu_next[0]])
    zl = jnp.zeros((DECAY_LORA, hd), F32)
    wa = jnp.stack([jnp.concatenate([jnp.concatenate([decay_w2[0, dd], zl], axis=1),
                                     jnp.concatenate([zl, iclr_a2[0, dd]], axis=1)], axis=0)
                    for dd in range(2)])
    col_scale = jnp.tile(jnp.concatenate([jnp.full((hd,), LOG2E, F32), jnp.full((hd,), 0.5, F32)]), 2)
    wa = (jnp.concatenate([wa[0], wa[1]], axis=1) * col_scale).astype(BF16)
    w0a0 = jnp.concatenate([decay_w0[0], iclr_a0[0]], axis=1).reshape(1, 4 * hd) * col_scale
    seg = np.arange(hd) // RWKV_N
    bd = jnp.asarray(seg[:, None] == seg[None, :], F32).astype(BF16)
    k_k, r_k = row1(key_kk[0]), row1(bonus_rk[0])
    k_a = jnp.stack([1.0 - key_ka[0], key_ka[0]])
    g2w = gate_g2[0].astype(BF16)
    s_zero = jnp.zeros((b, 2, hd // LANES, LANES, LANES), F32)
    opf_c, opb_c, vb_c, pcs_c, bonus_c, gate_c = _rwkv_prep_call(zrwc.reshape(b, nctx, -1), mu, wa, w0a0,
                                                                 k_k, k_a, r_k, g2w, bd)
    opf, opb, vb, pcs, bonus, gate = _rwkv_prep_call(zrw.reshape(b, t, -1), mu, wa, w0a0,
                                                     k_k, k_a, r_k, g2w, bd)
    yfc, ybc, s_ctx = _rwkv_solve_call(opf_c, opb_c, vb_c, pcs_c, s_zero)
    yf, yb, _ = _rwkv_solve_call(opf, opb, vb, pcs, s_ctx)

    w_out = mix_w_out[0].astype(BF16)
    n2g = row1(norm2_g[0])
    flat = lambda a: a.reshape(-1, a.shape[-1])
    bd_mean = (jnp.asarray(seg[:, None] == seg[None, :], F32) * (1.0 / RWKV_N)).astype(BF16)
    xm, h2 = _readout_call(flat(att), flat(yf), flat(yb), flat(bonus), flat(gate), x2,
                           row1(lnx_g[0]), row1(lnx_b[0]), bd_mean, w_out, lm(0, 2), n2g,
                           lm(0, 3), lm(0, 4), t, tm_in)
    xmc, h2c = _readout_call(flat(att_c), flat(yfc), flat(ybc), flat(bonus_c), flat(gate_c), ctx2,
                             row1(lnx_g[0]), row1(lnx_b[0]), bd_mean, w_out, cm(0, 2), n2g,
                             cm(0, 3), cm(0, 4), b * nctx, nctx)
    w1 = mlp_w1[0].astype(BF16)
    w2 = mlp_w2[0].astype(BF16)
    g1n = row1(norm1_g[1])
    tm_mlp = min(MLP_TM, t)
    x1, h1 = _mlp_call(h2, xm, w1, w2, lm(0, 5), g1n, lm(1, 0), lm(1, 1), t, tm_mlp, MLP_TF, False)
    ctx1, _ = _mlp_call(h2c, xmc, w1, w2, cm(0, 5), g1n, cm(1, 0), cm(1, 1), b * nctx, nctx, MLP_TF,
                        False)
    del ctx1

    l1 = t // FFT_L2
    nb = FFT_NB
    mb = min(FFT_MB, l1)
    gw = d // FOURIER_GROUPS
    kr1, wc, cs = _fft_tables(t, gw, nb, mb)
    gk = _expand_gk(cs, mb)
    z4 = _fft1_call(h1.reshape(b, l1, FFT_L2, d), kr1, wc, nb)
    xm4, h24 = _fft2_call(z4, gk, x1.reshape(b, FFT_L2, l1, d), fourier_w_out[0].astype(BF16),
                          lm(1, 2), row1(norm2_g[1]), lm(1, 3), lm(1, 4), mb, FFT_M2B)
    out, = _mlp_call(h24.reshape(b * t, d), xm4.reshape(b * t, d), mlp_w1[1].astype(BF16),
                       mlp_w2[1].astype(BF16), lm(1, 5), row1(final_g), lm(1, 0), lm(1, 1),
                       t, tm_mlp, MLP_TF, True)
    return out.reshape(b, t, d)
```
